```python
import jax, jax.numpy as jnp
from jax import lax
import numpy as np

D_MODEL = 1024
BATCH = 8
SEQ = 2048
DEPTH = 2
DEC_BATCH = 128
DEC_SEQ = 4
PAST_LEN = 16384
PAGE_SIZE = 128

HG_HEADS = 8
HG_DK = 128
HG_DV = 128
HG_WIDTH = HG_HEADS * HG_DK
HG_CHUNK = 64
SC_WIDTH = 1024
CONV_W = 3
N_GROUPS = 4
EXPERTS_PER_GROUP = 8
N_EXPERTS = N_GROUPS * EXPERTS_PER_GROUP
TOP_K = 2
D_EXPERT = 256
PLE_DIM = 256
EPS = 1e-6
IN_SIZES = (HG_WIDTH, HG_WIDTH, HG_WIDTH, HG_WIDTH, SC_WIDTH, SC_WIDTH, SC_WIDTH, D_MODEL, D_MODEL)
IN_COLS = 4 * HG_WIDTH + 3 * SC_WIDTH + 2 * D_MODEL

kernel_name = 'hgrn2_shortconv_hiermoe_ple_decode_step'


def rmsnorm(x, g):
    xf = x.astype(jnp.float32)
    r = lax.rsqrt(jnp.mean(xf * xf, axis=-1, keepdims=True) + EPS)
    return (xf * r).astype(x.dtype) * g


def hgrn_lower_bounds(lb_param):
    s = jax.nn.softmax(lb_param.astype(jnp.float32), axis=0)
    c = jnp.cumsum(s, axis=0)
    return c - c[0:1]


def hgrn2_scan(q, k, v, logf, s0):
    b, l = q.shape[0], q.shape[1]
    c = HG_CHUNK if l % HG_CHUNK == 0 else l
    n = l // c

    def to_chunks(t):
        return t.reshape(b, n, c, HG_HEADS, t.shape[-1]).transpose(1, 0, 3, 2, 4)

    qc, kc, vc, gc = to_chunks(q), to_chunks(k), to_chunks(v), to_chunks(logf)
    causal = jnp.tril(jnp.ones((c, c), dtype=bool))

    def step(s, inp):
        qi, ki, vi, gi = inp
        cum = jnp.cumsum(gi, axis=2)
        diff = cum[:, :, :, None, :] - cum[:, :, None, :, :]
        decay = jnp.exp(jnp.where(causal[:, :, None], diff, -jnp.inf))
        attn = jnp.sum(qi[:, :, :, None, :] * ki[:, :, None, :, :] * decay, axis=-1)
        o = (jnp.einsum('bhts,bhsv->bhtv', attn, vi)
             + jnp.einsum('bhtd,bhdv->bhtv', qi * jnp.exp(cum), s))
        last = cum[:, :, -1:, :]
        s_new = (jnp.exp(last[:, :, 0, :])[..., None] * s
                 + jnp.einsum('bhsd,bhsv->bhdv', ki * jnp.exp(last - cum), vi))
        return s_new, o

    s_fin, o = lax.scan(step, s0, (qc, kc, vc, gc))
    o = o.transpose(1, 0, 3, 2, 4).reshape(b, l, HG_HEADS, HG_DV)
    return o, s_fin


def hier_moe(h, w_rg, b_rg, w_re, b_re, w_gate, w_up, w_down):
    f32 = jnp.float32
    shp = h.shape
    t = h.reshape(-1, D_MODEL)
    n = t.shape[0]
    g_logits = (t @ w_rg).astype(f32) + b_rg.astype(f32)
    g_prob = jax.nn.softmax(g_logits, axis=-1)
    g_idx = jnp.argmax(g_logits, axis=-1)
    g_w = jnp.take_along_axis(g_prob, g_idx[:, None], axis=-1)
    e_logits = ((t @ w_re).astype(f32) + b_re.astype(f32)).reshape(n, N_GROUPS, EXPERTS_PER_GROUP)
    e_logits = jnp.take_along_axis(e_logits, g_idx[:, None, None], axis=1)[:, 0]
    e_prob = jax.nn.softmax(e_logits, axis=-1)
    top_p, top_i = lax.top_k(e_prob, TOP_K)
    top_p = top_p / jnp.sum(top_p, axis=-1, keepdims=True) * g_w
    within = jnp.sum(jax.nn.one_hot(top_i, EXPERTS_PER_GROUP, dtype=f32) * top_p[..., None], axis=1)
    combine = (jax.nn.one_hot(g_idx, N_GROUPS, dtype=f32)[:, :, None] * within[:, None, :])
    combine = combine.reshape(n, N_EXPERTS).astype(h.dtype)
    hid = jax.nn.silu(jnp.einsum('td,edf->tef', t, w_gate)) * jnp.einsum('td,edf->tef', t, w_up)
    out = jnp.einsum('tef,te,efd->td', hid, combine, w_down)
    return out.reshape(shp)


def trunk_layer(x, p, s_hg, conv_buf, lb, g_mix, w_in, g_hg_out, w_br_a, w_conv, w_br_b, w_out,
                g_ffn, w_rg, b_rg, w_re, b_re, w_gate, w_up, w_down, g_ple, w_ple_gate, w_ple_proj):
    f32 = jnp.float32
    bsz, l = x.shape[0], x.shape[1]
    h = rmsnorm(x, g_mix)
    proj = h @ w_in
    offs = np.cumsum(IN_SIZES)[:-1].tolist()
    q, fg, inp, og, cb, cc, ch, ga, gb = jnp.split(proj, offs, axis=-1)

    def heads(t):
        return t.reshape(bsz, l, HG_HEADS, -1).astype(f32)
    lbh = lb.reshape(HG_HEADS, HG_DK).astype(f32)
    fp = heads(fg)
    logf = jnp.logaddexp(jnp.log(lbh), jnp.log1p(-lbh) + jax.nn.log_sigmoid(fp))
    kf = (1.0 - lbh) * jax.nn.sigmoid(-fp)
    o, s_new = hgrn2_scan(jax.nn.silu(heads(q)), kf, heads(inp), logf, s_hg.astype(f32))
    o = rmsnorm(o, g_hg_out.reshape(HG_HEADS, HG_DV).astype(f32)).reshape(bsz, l, HG_WIDTH)
    o = (o * jax.nn.silu(og.astype(f32))).astype(x.dtype)
    y_a = o @ w_br_a

    u = cc * ch
    ext = jnp.concatenate([conv_buf.astype(u.dtype), u], axis=1)
    conv = sum(ext[:, j:j + l] * w_conv[j] for j in range(CONV_W))
    y_b = (cb * conv) @ w_br_b
    new_buf = ext[:, l:]

    x = x + (jax.nn.sigmoid(ga) * y_a + jax.nn.sigmoid(gb) * y_b) @ w_out

    x = x + hier_moe(rmsnorm(x, g_ffn), w_rg, b_rg, w_re, b_re, w_gate, w_up, w_down)

    x = x + jax.nn.sigmoid(rmsnorm(x, g_ple) @ w_ple_gate) * (p @ w_ple_proj)
    return x, s_new.astype(s_hg.dtype), new_buf


def run_trunk(x, p, s_hg, conv_buf, lbs, layer_params, g_final):
    hs, cs = [], []
    for li in range(DEPTH):
        x, s, c = trunk_layer(x, p[li], s_hg[li], conv_buf[li], lbs[li], *[w[li] for w in layer_params])
        hs.append(s)
        cs.append(c)
    return rmsnorm(x, g_final), jnp.stack(hs), jnp.stack(cs)


def setup_inputs(seed: int = 0) -> dict:
    key = jax.random.key(seed)
    ks = jax.random.split(key, 32)
    nrm = lambda k, shape, s=1.0: jax.random.normal(k, shape, jnp.float32) * s
    gain = lambda k, shape: 1.0 + 0.01 * jax.random.normal(k, shape, jnp.float32)
    return {
        'x_prompt': nrm(ks[0], (BATCH, SEQ, D_MODEL)),
        'x_sample': nrm(ks[1], (DEC_BATCH, DEC_SEQ, D_MODEL)),
        'state_hgrn': nrm(ks[2], (DEPTH, DEC_BATCH, HG_HEADS, HG_DK, HG_DV), 0.5),
        'state_conv': nrm(ks[3], (DEPTH, DEC_BATCH, CONV_W - 1, SC_WIDTH)),
        'p_prompt': nrm(ks[4], (DEPTH, BATCH, SEQ, PLE_DIM)),
        'p_sample': nrm(ks[5], (DEPTH, DEC_BATCH, DEC_SEQ, PLE_DIM)),
        'g_mix': gain(ks[6], (DEPTH, D_MODEL)),
        'w_in': nrm(ks[7], (DEPTH, D_MODEL, IN_COLS), D_MODEL ** -0.5),
        'hg_lower': nrm(ks[8], (DEPTH, HG_WIDTH)),
        'g_hg_out': gain(ks[9], (DEPTH, HG_WIDTH)),
        'w_br_a': nrm(ks[10], (DEPTH, HG_WIDTH, D_MODEL), HG_WIDTH ** -0.5),
        'w_conv': nrm(ks[11], (DEPTH, CONV_W, SC_WIDTH), CONV_W ** -0.5),
        'w_br_b': nrm(ks[12], (DEPTH, SC_WIDTH, D_MODEL), SC_WIDTH ** -0.5),
        'w_out': nrm(ks[13], (DEPTH, D_MODEL, D_MODEL), D_MODEL ** -0.5),
        'g_ffn': gain(ks[14], (DEPTH, D_MODEL)),
        'w_router_group': nrm(ks[15], (DEPTH, D_MODEL, N_GROUPS), D_MODEL ** -0.5),
        'b_router_group': nrm(ks[16], (DEPTH, N_GROUPS), 0.01),
        'w_router_expert': nrm(ks[17], (DEPTH, D_MODEL, N_EXPERTS), D_MODEL ** -0.5),
        'b_router_expert': nrm(ks[18], (DEPTH, N_EXPERTS), 0.01),
        'w_gate': nrm(ks[19], (DEPTH, N_EXPERTS, D_MODEL, D_EXPERT), D_MODEL ** -0.5),
        'w_up': nrm(ks[20], (DEPTH, N_EXPERTS, D_MODEL, D_EXPERT), D_MODEL ** -0.5),
        'w_down': nrm(ks[21], (DEPTH, N_EXPERTS, D_EXPERT, D_MODEL), D_EXPERT ** -0.5),
        'g_ple': gain(ks[22], (DEPTH, D_MODEL)),
        'w_ple_gate': nrm(ks[23], (DEPTH, D_MODEL, D_MODEL), D_MODEL ** -0.5),
        'w_ple_proj': nrm(ks[24], (DEPTH, PLE_DIM, D_MODEL), PLE_DIM ** -0.5),
        'g_final': gain(ks[25], (D_MODEL,)),
    }


def reference(x_prompt, x_sample, state_hgrn, state_conv, p_prompt, p_sample, g_mix, w_in, hg_lower,
              g_hg_out, w_br_a, w_conv, w_br_b, w_out, g_ffn, w_router_group, b_router_group,
              w_router_expert, b_router_expert, w_gate, w_up, w_down, g_ple, w_ple_gate, w_ple_proj,
              g_final):
    lbs = hgrn_lower_bounds(hg_lower)
    layer_params = (g_mix, w_in, g_hg_out, w_br_a, w_conv, w_br_b, w_out, g_ffn, w_router_group,
                    b_router_group, w_router_expert, b_router_expert, w_gate, w_up, w_down, g_ple,
                    w_ple_gate, w_ple_proj)
    bp = x_prompt.shape[0]
    s0 = jnp.zeros((DEPTH, bp, HG_HEADS, HG_DK, HG_DV), state_hgrn.dtype)
    c0 = jnp.zeros((DEPTH, bp, CONV_W - 1, SC_WIDTH), x_prompt.dtype)
    y_prompt, hgrn_prompt, conv_prompt = run_trunk(x_prompt, p_prompt, s0, c0, lbs, layer_params, g_final)
    y_sample, hgrn_sample, conv_sample = run_trunk(x_sample, p_sample, state_hgrn, state_conv, lbs,
                                                   layer_params, g_final)
    return (y_prompt, y_sample, hgrn_prompt, conv_prompt, hgrn_sample, conv_sample)
```

```python
import functools

import jax
import jax.numpy as jnp
from jax import lax
from jax.experimental import pallas as pl
from jax.experimental.pallas import tpu as pltpu

F32 = jnp.float32
BF16 = jnp.bfloat16
I32 = jnp.int32

D_MODEL = 1024
N_HEADS = 8
HEAD_DIM = 128
CONV_W = 3
N_GROUPS = 4
EXPERTS_PER_GROUP = 8
N_EXPERTS = N_GROUPS * EXPERTS_PER_GROUP
D_EXPERT = 256
EPS = 1e-6

LANES = 128
BF16_ROWS = 16
VMEM_LIMIT = 56 * 1024 * 1024

ROW_TILE = 256
SCAN_CHUNK = 256
TAIL_ROWS = 256
SAMPLE_SEQS = 8
MOE_TILE = 256
MOE_SORTED_ROWS = 1024
MOE_CHUNKS = MOE_SORTED_ROWS // BF16_ROWS
MOE_SUPER = 8
MOE_BLOCK_CHUNKS = 16
GROUP_LANE0 = N_EXPERTS

_NN = (((1,), (0,)), ((), ()))
_NT = (((1,), (1,)), ((), ()))
_TN = (((0,), (0,)), ((), ()))


def _params(sem):
    return pltpu.CompilerParams(dimension_semantics=sem, vmem_limit_bytes=VMEM_LIMIT)


def _resident(shape):
    nd = len(shape)
    return pl.BlockSpec(shape, lambda *_: (0,) * nd, pipeline_mode=pl.Buffered(1))


def _rmsnorm(x, g):
    r = lax.rsqrt(jnp.mean(x * x, axis=-1, keepdims=True) + EPS)
    return (x * r) * g


def _dg(a, b, dims=_NN):
    return lax.dot_general(a, b, dims, preferred_element_type=F32)


def _split2(x):
    hi = x.astype(BF16)
    return hi, (x - hi.astype(F32)).astype(BF16)


def _mm(a, b, precise, dims=_NN):
    if not precise:
        return _dg(a.astype(BF16), b.astype(BF16), dims)
    ah, al = _split2(a.astype(F32))
    bh, bl = _split2(b.astype(F32))
    return _dg(ah, bh, dims) + (_dg(al, bh, dims) + _dg(ah, bl, dims))


def _mmw(a, w_hi, w_lo, precise):
    if not precise:
        return _dg(a.astype(BF16), w_hi)
    ah, al = _split2(a.astype(F32))
    return _dg(ah, w_hi) + (_dg(al, w_hi) + _dg(ah, w_lo))


def _split3(x):
    hi = x.astype(BF16)
    r1 = x - hi.astype(F32)
    mid = r1.astype(BF16)
    lo = (r1 - mid.astype(F32)).astype(BF16)
    return hi, mid, lo


def _dot_exact_lhs(m01, x):
    hi, mid, lo = _split3(x)
    return _dg(m01, hi) + _dg(m01, mid) + _dg(m01, lo)


def _either(precise_pred, body):
    if precise_pred is None:
        body(False)
    else:
        pl.when(precise_pred)(lambda: body(True))
        pl.when(jnp.logical_not(precise_pred))(lambda: body(False))


def _tail_tile(i, tiles_per_seq, tail_tiles, prompt_tiles):
    return (i >= prompt_tiles) | ((i % tiles_per_seq) >= tiles_per_seq - tail_tiles)


def _w_slice(ref, sl):
    return None if ref is None else ref[:, sl]


def _inproj_kernel(*refs, tail):
    if tail is None:
        x_ref, g_ref, lb_ref, w_ref, qs_ref, lf_ref, kf_ref, v_ref, sog_ref, u_ref = refs
        wl_ref = None
    else:
        x_ref, g_ref, lb_ref, w_ref, wl_ref, qs_ref, lf_ref, kf_ref, v_ref, sog_ref, u_ref = refs
    d = D_MODEL

    def body(precise):
        h = _rmsnorm(x_ref[...], g_ref[...])

        def seg(j):
            sl = slice(j * d, (j + 1) * d)
            return _mmw(h, w_ref[:, sl], _w_slice(wl_ref, sl), precise)

        q = seg(0)
        qs_ref[...] = q * jax.nn.sigmoid(q)
        fp = seg(1)
        lb = lb_ref[...]
        log_sig = jnp.minimum(fp, 0.0) - jnp.log1p(jnp.exp(-jnp.abs(fp)))
        a = jnp.log(lb)
        b = jnp.log1p(-lb) + log_sig
        lf_ref[...] = jnp.maximum(a, b) + jnp.log1p(jnp.exp(-jnp.abs(a - b)))
        kf_ref[...] = (1.0 - lb) * jax.nn.sigmoid(-fp)
        v_ref[...] = seg(2).astype(v_ref.dtype)
        og = seg(3)
        sog_ref[...] = (og * jax.nn.sigmoid(og)).astype(sog_ref.dtype)
        u_ref[...] = seg(4) * seg(5)

    _either(None if tail is None else _tail_tile(pl.program_id(0), *tail), body)


def _inproj(x, g, lb, w, w_lo, tail, act_dtype):
    t, d = x.shape
    tm = ROW_TILE
    assert t % tm == 0 and d == D_MODEL
    row = pl.BlockSpec((tm, d), lambda i: (i, 0))
    ws = [w] if tail is None else [w, w_lo]
    return pl.pallas_call(
        functools.partial(_inproj_kernel, tail=tail),
        grid=(t // tm,),
        in_specs=[row, _resident((1, d)), _resident((1, d))] + [_resident(a.shape) for a in ws],
        out_specs=[row] * 6,
        out_shape=[jax.ShapeDtypeStruct((t, d), dt) for dt in (F32, F32, F32, act_dtype, act_dtype, F32)],
        compiler_params=_params(("parallel",)),
        name="inproj",
    )(x, g, lb, *ws)


def _intra_scores(q, k, g, cum, n_rows, block, precise):
    ri = lax.broadcasted_iota(I32, (n_rows, LANES), 0)
    xr = (lax.broadcasted_iota(I32, (n_rows, n_rows), 0)
          ^ lax.broadcasted_iota(I32, (n_rows, n_rows), 1))
    s = jnp.where(xr == 0, jnp.sum(q * k, axis=1, keepdims=True), 0.0)
    b = 1
    while b < block:
        upper = (ri & b) != 0
        if b == 1:
            z = jnp.where(upper, g, 0.0)
        elif b == 2:
            g_prev = pltpu.roll(g, 1, 0)
            g_next = pltpu.roll(g, n_rows - 1, 0)
            m4 = ri & 3
            z = jnp.where(m4 == 2, g, jnp.where(m4 == 3, g + g_prev, jnp.where(m4 == 0, g_next, 0.0)))
        else:
            c3 = cum.reshape(n_rows // (2 * b), 2 * b, LANES)
            mid = jnp.broadcast_to(c3[:, b - 1:b, :], c3.shape).reshape(n_rows, LANES)
            z = jnp.where(upper, cum - mid, mid - cum)
        e = jnp.exp(z)
        qt = jnp.where(upper, q * e, 0.0)
        kt = jnp.where(upper, 0.0, k * e)
        s = s + jnp.where((xr >= b) & (xr < 2 * b), _mm(qt, kt, precise, _NT), 0.0)
        b *= 2
    return s


def _decay_columns(e_row):
    hi = e_row.astype(BF16).astype(F32)
    lo = e_row - hi
    r = lax.broadcasted_iota(I32, (BF16_ROWS, LANES), 0)
    stacked = jnp.where(r == 0, hi, jnp.where(r == 1, lo, 0.0)).astype(BF16)
    return _dg(stacked, jnp.ones((BF16_ROWS, LANES), BF16), _TN)


def _head_out(o, gh, sog, dtype):
    on = o * lax.rsqrt(jnp.mean(o * o, axis=-1, keepdims=True) + EPS) * gh
    return (on * sog.astype(F32)).astype(dtype)


def _scan_prompt_kernel(qs_ref, lf_ref, kf_ref, v_ref, sog_ref, gh_ref, og_ref, sfin_ref, s_scr, *, tail_chunks):
    c = pl.program_id(1)
    nc = pl.num_programs(1)
    n = SCAN_CHUNK

    @pl.when(c == 0)
    def _():
        s_scr[...] = jnp.zeros_like(s_scr)

    def body(precise):
        tri = (lax.broadcasted_iota(I32, (n, n), 0) >= lax.broadcasted_iota(I32, (n, n), 1)).astype(BF16)

        def head(h, carry):
            hs = pl.ds(pl.multiple_of(h * HEAD_DIM, HEAD_DIM), HEAD_DIM)
            q, g, k, v = qs_ref[:, hs], lf_ref[:, hs], kf_ref[:, hs], v_ref[:, hs]
            cum = _dot_exact_lhs(tri, g)
            tot = cum[n - 1:n, :]
            s_old = s_scr[h]
            o = _mm(q * jnp.exp(cum), s_old, precise)
            o = o + _mm(_intra_scores(q, k, g, cum, n, n, precise), v, precise)
            ke = k * jnp.exp(tot - cum)
            s_scr[h] = _decay_columns(jnp.exp(tot)) * s_old + _mm(ke, v, precise, _TN)
            og_ref[:, hs] = _head_out(o, gh_ref[:, hs], sog_ref[:, hs], og_ref.dtype)
            return carry

        lax.fori_loop(0, N_HEADS, head, 0)

    _either(None if tail_chunks is None else c >= nc - tail_chunks, body)

    @pl.when(c == nc - 1)
    def _():
        sfin_ref[0] = s_scr[...]


def _scan_prompt(qs, lf, kf, v, sog, gh, batch, seq, tail_chunks):
    t, d = qs.shape
    n = SCAN_CHUNK
    assert seq % n == 0
    nc = seq // n
    row = pl.BlockSpec((n, d), lambda b, c: (b * nc + c, 0))
    return pl.pallas_call(
        functools.partial(_scan_prompt_kernel, tail_chunks=tail_chunks),
        grid=(batch, nc),
        in_specs=[row, row, row, row, row, _resident((1, d))],
        out_specs=[row, pl.BlockSpec((1, N_HEADS, HEAD_DIM, HEAD_DIM), lambda b, c: (b, 0, 0, 0))],
        out_shape=[jax.ShapeDtypeStruct((t, d), v.dtype),
                   jax.ShapeDtypeStruct((batch, N_HEADS, HEAD_DIM, HEAD_DIM), F32)],
        scratch_shapes=[pltpu.VMEM((N_HEADS, HEAD_DIM, HEAD_DIM), F32)],
        compiler_params=_params(("parallel", "arbitrary")),
        name="scan_prompt",
    )(qs, lf, kf, v, sog, gh)


def _scan_sample_kernel(qs_ref, lf_ref, kf_ref, v_ref, sog_ref, gh_ref, s_ref, og_in_ref, og_ref, snew_ref, *,
                        steps, precise):
    del og_in_ref
    n = SAMPLE_SEQS * steps
    shift = steps.bit_length() - 1
    ri = lax.broadcasted_iota(I32, (n, n), 0)
    ci = lax.broadcasted_iota(I32, (n, n), 1)
    same = (ri >> shift) == (ci >> shift)
    tri = (same & (ri >= ci)).astype(BF16)
    tri_after = (same & (ci > ri)).astype(BF16)
    seq_of_row = lax.broadcasted_iota(I32, (n, LANES), 0) >> shift

    def head(h, carry):
        hs = pl.ds(pl.multiple_of(h * HEAD_DIM, HEAD_DIM), HEAD_DIM)
        q, g, k, v = qs_ref[:, hs], lf_ref[:, hs], kf_ref[:, hs], v_ref[:, hs]
        cum = _dot_exact_lhs(tri, g)
        after = _dot_exact_lhs(tri_after, g)
        o = _mm(_intra_scores(q, k, g, cum, n, steps, precise), v, precise)
        qe = q * jnp.exp(cum)
        ke = k * jnp.exp(after)
        for s in range(SAMPLE_SEQS):
            mine = seq_of_row == s
            s_old = s_ref[s, h]
            o = o + jnp.where(mine, _mm(qe, s_old, precise), 0.0)
            kv = _mm(jnp.where(mine, ke, 0.0), v, precise, _TN)
            last = (s + 1) * steps - 1
            snew_ref[s, h] = _decay_columns(jnp.exp(cum[last:last + 1, :])) * s_old + kv
        og_ref[:, hs] = _head_out(o, gh_ref[:, hs], sog_ref[:, hs], og_ref.dtype)
        return carry

    lax.fori_loop(0, N_HEADS, head, 0)


def _scan_sample(qs, lf, kf, v, sog, gh, state, og_all, row0, nseq, steps, precise):
    t, d = qs.shape
    assert steps & (steps - 1) == 0 and steps >= 4, "sample block must be a power of two >= 4"
    n = SAMPLE_SEQS * steps
    assert nseq % SAMPLE_SEQS == 0 and row0 % n == 0 and n % BF16_ROWS == 0
    b0 = row0 // n
    row = pl.BlockSpec((n, d), lambda i: (b0 + i, 0))
    st = pl.BlockSpec((SAMPLE_SEQS, N_HEADS, HEAD_DIM, HEAD_DIM), lambda i: (i, 0, 0, 0))
    return pl.pallas_call(
        functools.partial(_scan_sample_kernel, steps=steps, precise=precise),
        grid=(nseq // SAMPLE_SEQS,),
        in_specs=[row, row, row, row, row, _resident((1, d)), st, pl.BlockSpec(memory_space=pl.ANY)],
        out_specs=[row, st],
        out_shape=[jax.ShapeDtypeStruct((t, d), og_all.dtype), jax.ShapeDtypeStruct(state.shape, F32)],
        input_output_aliases={7: 0},
        compiler_params=_params(("parallel",)),
        name="scan_sample",
    )(qs, lf, kf, v, sog, gh, state, og_all)


def _mix_kernel(*refs, tiles_per_seq, prompt_tiles, steps, tail):
    if tail is None:
        (x_ref, og_ref, u_ref, halo_ref, e1_ref, e2_ref, g_ref, wc_ref,
         w_cbg_ref, wa_ref, wb_ref, wo_ref, x1_ref) = refs
        l_cbg_ref = la_ref = lb_ref = lo_ref = None
    else:
        (x_ref, og_ref, u_ref, halo_ref, e1_ref, e2_ref, g_ref, wc_ref,
         w_cbg_ref, wa_ref, wb_ref, wo_ref, l_cbg_ref, la_ref, lb_ref, lo_ref, x1_ref) = refs
    d = D_MODEL
    tm = ROW_TILE
    i = pl.program_id(0)

    def full(ref):
        return None if ref is None else ref[...]

    def body(precise):
        x = x_ref[...]
        h = _rmsnorm(x, g_ref[...])
        u = u_ref[...]
        ri = lax.broadcasted_iota(I32, (tm, d), 0)
        r1 = pltpu.roll(u, 1, 0)
        r2 = pltpu.roll(u, 2, 0)
        keep = jnp.where(i % tiles_per_seq == 0, 0.0, 1.0)
        halo = halo_ref[...]
        h_last = halo[7:8, :] * keep
        h_prev = halo[6:7, :] * keep
        is_sample = i >= prompt_tiles
        t_in_seq = ri & (steps - 1)
        prev1 = jnp.where(is_sample,
                          jnp.where(t_in_seq == 0, e1_ref[...], r1),
                          jnp.where(ri == 0, h_last, r1))
        prev2 = jnp.where(is_sample,
                          jnp.where(t_in_seq < 2, e2_ref[...], r2),
                          jnp.where(ri == 0, h_prev, jnp.where(ri == 1, h_last, r2)))
        wc = wc_ref[...]
        conv = prev2 * wc[0:1, :] + prev1 * wc[1:2, :] + u * wc[2:3, :]

        def seg(j):
            sl = slice(j * d, (j + 1) * d)
            return _mmw(h, w_cbg_ref[:, sl], _w_slice(l_cbg_ref, sl), precise)

        y_b = _mmw(seg(0) * conv, wb_ref[...], full(lb_ref), precise)
        y_a = _mmw(og_ref[...], wa_ref[...], full(la_ref), precise)
        m = jax.nn.sigmoid(seg(1)) * y_a + jax.nn.sigmoid(seg(2)) * y_b
        x1_ref[...] = x + _mmw(m, wo_ref[...], full(lo_ref), precise)

    _either(None if tail is None else _tail_tile(i, *tail), body)


def _mix(x, og, u, e1, e2, g, wc, ws, ws_lo, prompt_rows, seq, steps, tail):
    t, d = x.shape
    tm = ROW_TILE
    assert seq % tm == 0 and prompt_rows % tm == 0 and (t - prompt_rows) % tm == 0 and tm % steps == 0
    pt = prompt_rows // tm
    row = pl.BlockSpec((tm, d), lambda i: (i, 0))
    halo = pl.BlockSpec((8, d), lambda i: (jnp.maximum(i * (tm // 8) - 1, 0), 0))
    erow = pl.BlockSpec((tm, d), lambda i: (jnp.maximum(i - pt, 0), 0))
    weights = list(ws) + ([] if tail is None else list(ws_lo))
    return pl.pallas_call(
        functools.partial(_mix_kernel, tiles_per_seq=seq // tm, prompt_tiles=pt, steps=steps, tail=tail),
        grid=(t // tm,),
        in_specs=[row, row, row, halo, erow, erow, _resident((1, d)), _resident(wc.shape)]
        + [_resident(a.shape) for a in weights],
        out_specs=row,
        out_shape=jax.ShapeDtypeStruct((t, d), F32),
        compiler_params=_params(("parallel",)),
        name="mix",
    )(x, og, u, u, e1, e2, g, wc, *weights)


def _route_kernel(x_ref, g_ref, wr_ref, br_ref, xs_ref, rinfo_ref, cinfo_ref):
    tm = MOE_TILE
    rows = MOE_SORTED_ROWS
    hn = _rmsnorm(x_ref[...], g_ref[...])
    hn_hi, hn_lo = _split2(hn)
    wr_hi, wr_lo = _split2(wr_ref[...])
    logits = _dg(hn_hi, wr_hi) + _dg(hn_lo, wr_hi) + _dg(hn_hi, wr_lo) + br_ref[...]

    lane = lax.broadcasted_iota(I32, (tm, LANES), 1)
    neg = -jnp.inf
    big = jnp.int32(LANES)

    def first_lane(mask):
        return jnp.min(jnp.where(mask, lane, big), axis=1, keepdims=True)

    is_group = (lane >= GROUP_LANE0) & (lane < GROUP_LANE0 + N_GROUPS)
    gl = jnp.where(is_group, logits, neg)
    gmax = jnp.max(gl, axis=1, keepdims=True)
    g_idx = first_lane(gl == gmax) - GROUP_LANE0
    g_w = 1.0 / jnp.sum(jnp.exp(gl - gmax), axis=1, keepdims=True)

    in_group = (lane < N_EXPERTS) & ((lane >> (EXPERTS_PER_GROUP.bit_length() - 1)) == g_idx)
    el = jnp.where(in_group, logits, neg)
    emax = jnp.max(el, axis=1, keepdims=True)
    ee = jnp.exp(el - emax)
    prob = ee / jnp.sum(ee, axis=1, keepdims=True)
    prob = jnp.where(in_group, prob, -1.0)
    p1 = jnp.max(prob, axis=1, keepdims=True)
    i1 = first_lane(prob == p1)
    prob2 = jnp.where(lane == i1, -1.0, prob)
    p2 = jnp.max(prob2, axis=1, keepdims=True)
    i2 = first_lane(prob2 == p2)
    psum = p1 + p2
    w1 = p1 / psum * g_w
    w2 = p2 / psum * g_w

    oh1 = (lane == i1).astype(BF16)
    oh2 = (lane == i2).astype(BF16)
    before = (lax.broadcasted_iota(I32, (tm, tm), 1) < lax.broadcasted_iota(I32, (tm, tm), 0)).astype(BF16)
    c1 = _dg(before, oh1)
    c2 = _dg(before, oh2)
    oh1f = oh1.astype(F32)
    oh2f = oh2.astype(F32)
    cnt1 = jnp.sum(oh1f, axis=0, keepdims=True)
    cnt2 = jnp.sum(oh2f, axis=0, keepdims=True)
    chunks = jnp.floor((cnt1 + cnt2 + (BF16_ROWS - 1)) * (1.0 / BF16_ROWS))
    excl = (lax.broadcasted_iota(I32, (LANES, LANES), 0) < lax.broadcasted_iota(I32, (LANES, LANES), 1)).astype(BF16)
    chunk0 = _dg(jnp.broadcast_to(chunks, (BF16_ROWS, LANES)).astype(BF16), excl)[0:1, :]
    base = chunk0 * BF16_ROWS
    pos1 = jnp.sum(oh1f * (base + c1), axis=1, keepdims=True)
    pos2 = jnp.sum(oh2f * (base + cnt1 + c2), axis=1, keepdims=True)

    slot = lax.broadcasted_iota(I32, (tm, rows), 1)
    place = ((slot == pos1.astype(I32)) | (slot == pos2.astype(I32))).astype(BF16)
    xs_ref[0] = _dg(place, hn_hi, _TN).astype(BF16)

    rinfo_ref[...] = jnp.where(lane == 0, pos1, jnp.where(lane == 1, pos2, jnp.where(lane == 2, w1, jnp.where(lane == 3, w2, 0.0))))
    r8 = lax.broadcasted_iota(I32, (8, LANES), 0)
    cinfo_ref[0] = jnp.where(r8 == 0, chunk0, jnp.where(r8 == 1, chunks, 0.0)).astype(I32)


def _route(x1, g, wr, br):
    t, d = x1.shape
    tm = MOE_TILE
    assert t % tm == 0
    nt = t // tm
    ns = -(-nt // MOE_SUPER)
    return pl.pallas_call(
        _route_kernel,
        grid=(nt,),
        in_specs=[pl.BlockSpec((tm, d), lambda i: (i, 0)), _resident((1, d)), _resident(wr.shape), _resident(br.shape)],
        out_specs=[pl.BlockSpec((1, MOE_SORTED_ROWS, d), lambda i: (i, 0, 0)),
                   pl.BlockSpec((tm, LANES), lambda i: (i, 0)),
                   pl.BlockSpec((1, 8, LANES), lambda i: (i, 0, 0))],
        out_shape=[jax.ShapeDtypeStruct((ns * MOE_SUPER, MOE_SORTED_ROWS, d), BF16),
                   jax.ShapeDtypeStruct((t, LANES), F32),
                   jax.ShapeDtypeStruct((nt, 8, LANES), I32)],
        compiler_params=_params(("parallel",)),
        name="route",
    )(x1, g, wr, br)


def _ffn_kernel(src_ref, start_ref, count_ref, xs_ref, wgu_ref, wd_ref, out_ref, lhs_scr, res_scr):
    s = pl.program_id(0)
    e = pl.program_id(1)
    rows = BF16_ROWS
    nsrc = MOE_SUPER * MOE_CHUNKS

    @pl.when(e == 0)
    def _():
        out_ref[...] = jnp.zeros_like(out_ref)

    n = count_ref[s * N_EXPERTS + e]
    k0 = start_ref[s * N_EXPERTS + e]

    def block(b, carry):
        srcs = []
        for i in range(MOE_BLOCK_CHUNKS):
            k = b * MOE_BLOCK_CHUNKS + i
            valid = k < n
            src = src_ref[s * nsrc + k0 + jnp.where(valid, k, 0)]
            off = pl.multiple_of(src * rows, rows)
            lhs_scr[i * rows:(i + 1) * rows, :] = xs_ref[0, pl.ds(off, rows), :]
            srcs.append((valid, off))
        lhs = lhs_scr[...]
        gu = _dg(lhs, wgu_ref[0])
        gate = gu[:, :D_EXPERT]
        hid = gate * jax.nn.sigmoid(gate) * gu[:, D_EXPERT:]
        res_scr[...] = _dg(hid.astype(BF16), wd_ref[0]).astype(BF16)
        for i, (valid, off) in enumerate(srcs):
            @pl.when(valid)
            def _():
                out_ref[0, pl.ds(off, rows), :] = res_scr[i * rows:(i + 1) * rows, :]
        return carry

    lax.fori_loop(0, (n + MOE_BLOCK_CHUNKS - 1) // MOE_BLOCK_CHUNKS, block, 0)


def _ffn(xs, chunk_src, e_start, e_count, wgu, wd):
    ns, srows, d = xs.shape
    assert srows == MOE_SUPER * MOE_SORTED_ROWS
    blk = pl.BlockSpec((1, srows, d), lambda s, e, *_: (s, 0, 0), pipeline_mode=pl.Buffered(1))
    brows = MOE_BLOCK_CHUNKS * BF16_ROWS
    return pl.pallas_call(
        _ffn_kernel,
        grid_spec=pltpu.PrefetchScalarGridSpec(
            num_scalar_prefetch=3,
            grid=(ns, N_EXPERTS),
            in_specs=[blk,
                      pl.BlockSpec((1, d, 2 * D_EXPERT), lambda s, e, *_: (e, 0, 0)),
                      pl.BlockSpec((1, D_EXPERT, d), lambda s, e, *_: (e, 0, 0))],
            out_specs=blk,
            scratch_shapes=[pltpu.VMEM((brows, d), BF16), pltpu.VMEM((brows, d), BF16)],
        ),
        out_shape=jax.ShapeDtypeStruct(xs.shape, BF16),
        compiler_params=_params(("parallel", "arbitrary")),
        name="expert_ffn",
    )(chunk_src, e_start, e_count, xs, wgu, wd)


def _chunk_tables(cinfo):
    nt = cinfo.shape[0]
    ns = -(-nt // MOE_SUPER)
    pad = ns * MOE_SUPER - nt
    chunk0 = jnp.pad(cinfo[:, 0, :N_EXPERTS], ((0, pad), (0, 0)))
    chunks = jnp.pad(cinfo[:, 1, :N_EXPERTS], ((0, pad), (0, 0)))
    ends = chunk0 + chunks
    c = jnp.arange(MOE_CHUNKS, dtype=I32)
    label = jnp.sum((c[None, :, None] >= ends[:, None, :]).astype(I32), axis=-1)
    label = label.reshape(ns, MOE_SUPER * MOE_CHUNKS)
    chunk_src = jnp.argsort(label, axis=-1, stable=True).astype(I32)
    e_count = chunks.reshape(ns, MOE_SUPER, N_EXPERTS).sum(axis=1).astype(I32)
    e_start = (jnp.cumsum(e_count, axis=-1) - e_count).astype(I32)
    return chunk_src.reshape(-1), e_start.reshape(-1), e_count.reshape(-1)


def _ple_kernel(x1_ref, ys_ref, rinfo_ref, p_ref, g_ref, wg_ref, wp_ref, gfin_ref, out_ref, *, final):
    tm = MOE_TILE
    rows = MOE_SORTED_ROWS
    rinfo = rinfo_ref[...]
    pos1 = rinfo[:, 0:1].astype(I32)
    pos2 = rinfo[:, 1:2].astype(I32)
    w1 = rinfo[:, 2:3]
    w2 = rinfo[:, 3:4]
    slot = lax.broadcasted_iota(I32, (tm, rows), 1)
    ys = ys_ref[0]
    y1 = _dg((slot == pos1).astype(BF16), ys)
    y2 = _dg((slot == pos2).astype(BF16), ys)
    x = x1_ref[...] + (w1 * y1 + w2 * y2)
    gate = jax.nn.sigmoid(_dg(_rmsnorm(x, g_ref[...]).astype(BF16), wg_ref[...]))
    x = x + gate * _dg(p_ref[...].astype(BF16), wp_ref[...])
    if final:
        x = _rmsnorm(x, gfin_ref[...])
    out_ref[...] = x


def _ple(x1, ys, rinfo, p, g, wg, wp, gfin, final):
    t, d = x1.shape
    tm = MOE_TILE
    nt = t // tm
    ys = ys.reshape(-1, MOE_SORTED_ROWS, d)
    return pl.pallas_call(
        functools.partial(_ple_kernel, final=final),
        grid=(nt,),
        in_specs=[pl.BlockSpec((tm, d), lambda i: (i, 0)),
                  pl.BlockSpec((1, MOE_SORTED_ROWS, d), lambda i: (i, 0, 0)),
                  pl.BlockSpec((tm, LANES), lambda i: (i, 0)),
                  pl.BlockSpec((tm, p.shape[1]), lambda i: (i, 0)),
                  _resident((1, d)), _resident(wg.shape), _resident(wp.shape), _resident((1, d))],
        out_specs=pl.BlockSpec((tm, d), lambda i: (i, 0)),
        out_shape=jax.ShapeDtypeStruct((t, d), F32),
        compiler_params=_params(("parallel",)),
        name="ple",
    )(x1, ys, rinfo, p, g, wg, wp, gfin)


def _lower_bounds(lb_param):
    s = jax.nn.softmax(lb_param.astype(F32), axis=0)
    c = jnp.cumsum(s, axis=0)
    return c - c[0:1]


def _split_kernel(w_ref, hi_ref, lo_ref):
    hi_ref[...], lo_ref[...] = _split2(w_ref[...])


def _hi_lo(w):
    k, n = w.shape
    tk = ROW_TILE
    assert k % tk == 0
    spec = pl.BlockSpec((tk, n), lambda i: (i, 0))
    return pl.pallas_call(
        _split_kernel,
        grid=(k // tk,),
        in_specs=[spec],
        out_specs=[spec, spec],
        out_shape=[jax.ShapeDtypeStruct((k, n), BF16)] * 2,
        compiler_params=_params(("parallel",)),
        name="split_weight",
    )(w)


def kernel(x_prompt, x_sample, state_hgrn, state_conv, p_prompt, p_sample, g_mix, w_in, hg_lower, g_hg_out, w_br_a, w_conv, w_br_b, w_out, g_ffn, w_router_group, b_router_group, w_router_expert, b_router_expert, w_gate, w_up, w_down, g_ple, w_ple_gate, w_ple_proj, g_final):
    depth = w_in.shape[0]
    bp, seq, d = x_prompt.shape
    bs, steps, _ = x_sample.shape
    assert d == D_MODEL and w_conv.shape[1] == CONV_W
    assert TAIL_ROWS % ROW_TILE == 0 and TAIL_ROWS % SCAN_CHUNK == 0 and seq >= TAIL_ROWS
    tp = bp * seq
    ts = bs * steps
    hw = N_HEADS * HEAD_DIM

    x = jnp.concatenate([x_prompt.reshape(tp, d), x_sample.reshape(ts, d)], axis=0)
    lbs = _lower_bounds(hg_lower)
    row = lambda a: a.reshape(1, -1)

    hg_p, cv_p, hg_s, cv_s = [], [], [], []
    for li in range(depth):
        careful = li < depth - 1
        tail = (seq // ROW_TILE, TAIL_ROWS // ROW_TILE, tp // ROW_TILE) if careful else None
        act = F32 if careful else BF16
        wi = w_in[li]
        w_scan = jnp.concatenate([wi[:, :4 * hw], wi[:, 4 * hw + d:4 * hw + 3 * d]], axis=1)
        w_cbg = jnp.concatenate([wi[:, 4 * hw:4 * hw + d], wi[:, 4 * hw + 3 * d:]], axis=1)
        mix_w = (w_cbg, w_br_a[li], w_br_b[li], w_out[li])
        if careful:
            w_scan, w_scan_lo = _hi_lo(w_scan)
            mix_w, mix_w_lo = zip(*[_hi_lo(w) for w in mix_w])
        else:
            w_scan, w_scan_lo = w_scan.astype(BF16), None
            mix_w, mix_w_lo = [w.astype(BF16) for w in mix_w], None

        qs, lf, kf, v, sog, u = _inproj(x, row(g_mix[li]), row(lbs[li]), w_scan, w_scan_lo, tail, act)

        gh = row(g_hg_out[li])
        og, s_p = _scan_prompt(qs, lf, kf, v, sog, gh, bp, seq, TAIL_ROWS // SCAN_CHUNK if careful else None)
        og, s_s = _scan_sample(qs, lf, kf, v, sog, gh, state_hgrn[li], og, tp, bs, steps, careful)
        hg_p.append(s_p)
        hg_s.append(s_s)
        cv_p.append(u[:tp].reshape(bp, seq, d)[:, seq - (CONV_W - 1):])
        cv_s.append(jnp.concatenate([state_conv[li], u[tp:].reshape(bs, steps, d)], axis=1)[:, steps:])

        buf = state_conv[li]
        zero = jnp.zeros((bs, steps - 1, d), F32)
        e1 = jnp.concatenate([buf[:, 1:2], zero], axis=1).reshape(ts, d)
        e2 = jnp.concatenate([buf, zero[:, 1:]], axis=1).reshape(ts, d)
        x1 = _mix(x, og, u, e1, e2, row(g_mix[li]), w_conv[li], mix_w, mix_w_lo, tp, seq, steps, tail)

        wr = jnp.zeros((d, LANES), F32).at[:, :N_EXPERTS].set(w_router_expert[li])
        wr = wr.at[:, GROUP_LANE0:GROUP_LANE0 + N_GROUPS].set(w_router_group[li])
        br = jnp.zeros((1, LANES), F32).at[0, :N_EXPERTS].set(b_router_expert[li])
        br = br.at[0, GROUP_LANE0:GROUP_LANE0 + N_GROUPS].set(b_router_group[li])
        xs, rinfo, cinfo = _route(x1, row(g_ffn[li]), wr, br)
        chunk_src, e_start, e_count = _chunk_tables(cinfo)
        xs = xs.reshape(-1, MOE_SUPER * MOE_SORTED_ROWS, d)
        wgu = jnp.concatenate([w_gate[li], w_up[li]], axis=-1).astype(BF16)
        ys = _ffn(xs, chunk_src, e_start, e_count, wgu, w_down[li].astype(BF16))

        p = jnp.concatenate([p_prompt[li].reshape(tp, -1), p_sample[li].reshape(ts, -1)], axis=0)
        x = _ple(x1, ys, rinfo, p, row(g_ple[li]), w_ple_gate[li].astype(BF16), w_ple_proj[li].astype(BF16),
                 row(g_final), li == depth - 1)

    y_prompt = x[:tp].reshape(bp, seq, d)
    y_sample = x[tp:].reshape(bs, steps, d)
    return (y_prompt, y_sample, jnp.stack(hg_p), jnp.stack(cv_p), jnp.stack(hg_s), jnp.stack(cv_s))
```

```python
import functools

import jax
import jax.numpy as jnp
from jax import lax
from jax.experimental import pallas as pl
from jax.experimental.pallas import tpu as pltpu

F32 = jnp.float32
BF16 = jnp.bfloat16
I32 = jnp.int32

D_MODEL = 1024
N_HEADS = 8
HEAD_DIM = 128
CONV_W = 3
N_GROUPS = 4
EXPERTS_PER_GROUP = 8
N_EXPERTS = N_GROUPS * EXPERTS_PER_GROUP
D_EXPERT = 256
EPS = 1e-6

LANES = 128
SUBLANES = 8
BF16_ROWS = 16
VMEM_LIMIT = 56 * 1024 * 1024

ROW_TILE = 256
SCAN_CHUNK = 128
SCAN_STEP_CHUNKS = 2
SCAN_HEAD_UNROLL = 2
TAIL_ROWS = 256
SAMPLE_SEQS = 8
MOE_TILE = 256
MOE_SORTED_ROWS = 1024
MOE_CHUNKS = MOE_SORTED_ROWS // BF16_ROWS
MOE_SUPER = 8
MOE_BLOCK_CHUNKS = 16
GROUP_LANE0 = N_EXPERTS

_NN = (((1,), (0,)), ((), ()))
_NT = (((1,), (1,)), ((), ()))
_TN = (((0,), (0,)), ((), ()))


def _params(sem):
    return pltpu.CompilerParams(dimension_semantics=sem, vmem_limit_bytes=VMEM_LIMIT)


def _resident(shape):
    nd = len(shape)
    return pl.BlockSpec(shape, lambda *_: (0,) * nd, pipeline_mode=pl.Buffered(1))


def _rmsnorm(x, g):
    r = lax.rsqrt(jnp.mean(x * x, axis=-1, keepdims=True) + EPS)
    return (x * r) * g


def _dg(a, b, dims=_NN):
    return lax.dot_general(a, b, dims, preferred_element_type=F32)


def _split2(x):
    hi = x.astype(BF16)
    return hi, (x - hi.astype(F32)).astype(BF16)


def _mm(a, b, precise, dims=_NN):
    if not precise:
        return _dg(a.astype(BF16), b.astype(BF16), dims)
    ah, al = _split2(a.astype(F32))
    bh, bl = _split2(b.astype(F32))
    return _dg(ah, bh, dims) + (_dg(al, bh, dims) + _dg(ah, bl, dims))


def _mmw(a, w_hi, w_lo, precise):
    if not precise:
        return _dg(a.astype(BF16), w_hi)
    ah, al = _split2(a.astype(F32))
    return _dg(ah, w_hi) + (_dg(al, w_hi) + _dg(ah, w_lo))


def _split3(x):
    hi = x.astype(BF16)
    r1 = x - hi.astype(F32)
    mid = r1.astype(BF16)
    lo = (r1 - mid.astype(F32)).astype(BF16)
    return hi, mid, lo


def _dot_exact_lhs(m01, x):
    hi, mid, lo = _split3(x)
    return _dg(m01, hi) + _dg(m01, mid) + _dg(m01, lo)


def _either(precise_pred, body):
    if precise_pred is None:
        body(False)
    else:
        pl.when(precise_pred)(lambda: body(True))
        pl.when(jnp.logical_not(precise_pred))(lambda: body(False))


def _tail_tile(i, tiles_per_seq, tail_tiles, prompt_tiles):
    return (i >= prompt_tiles) | ((i % tiles_per_seq) >= tiles_per_seq - tail_tiles)


def _pair_specs(tm, cols, prompt_tiles):
    return [pl.BlockSpec((tm, cols), lambda i: (jnp.minimum(i, prompt_tiles - 1), 0)),
            pl.BlockSpec((tm, cols), lambda i: (jnp.maximum(i - prompt_tiles, 0), 0))]


def _pair_load(i, prompt_tiles, p_ref, s_ref):
    return jnp.where(i < prompt_tiles, p_ref[...], s_ref[...])


def _w_slice(ref, sl):
    return None if ref is None else ref[:, sl]


def _inproj_kernel(*refs, tail, prompt_tiles):
    if tail is None:
        xp_ref, xs_ref, g_ref, lb_ref, w_ref, qs_ref, lf_ref, kf_ref, v_ref, sog_ref, u_ref = refs
        wl_ref = None
    else:
        xp_ref, xs_ref, g_ref, lb_ref, w_ref, wl_ref, qs_ref, lf_ref, kf_ref, v_ref, sog_ref, u_ref = refs
    d = D_MODEL
    i = pl.program_id(0)

    def body(precise):
        h = _rmsnorm(_pair_load(i, prompt_tiles, xp_ref, xs_ref), g_ref[...])

        def seg(j):
            sl = slice(j * d, (j + 1) * d)
            return _mmw(h, w_ref[:, sl], _w_slice(wl_ref, sl), precise)

        q = seg(0)
        qs_ref[...] = q * jax.nn.sigmoid(q)
        fp = seg(1)
        lb = lb_ref[...]
        log_sig = jnp.minimum(fp, 0.0) - jnp.log1p(jnp.exp(-jnp.abs(fp)))
        a = jnp.log(lb)
        b = jnp.log1p(-lb) + log_sig
        lf_ref[...] = jnp.maximum(a, b) + jnp.log1p(jnp.exp(-jnp.abs(a - b)))
        kf_ref[...] = (1.0 - lb) * jax.nn.sigmoid(-fp)
        v_ref[...] = seg(2).astype(v_ref.dtype)
        og = seg(3)
        sog_ref[...] = (og * jax.nn.sigmoid(og)).astype(sog_ref.dtype)
        u_ref[...] = seg(4) * seg(5)

    _either(None if tail is None else _tail_tile(i, *tail), body)


def _inproj(xp, xs, g, lb, w, w_lo, tail, act_dtype):
    d = D_MODEL
    tm = ROW_TILE
    assert xp.shape[0] % tm == 0 and xs.shape[0] % tm == 0 and xp.shape[1] == d
    pt = xp.shape[0] // tm
    t = xp.shape[0] + xs.shape[0]
    row = pl.BlockSpec((tm, d), lambda i: (i, 0))
    ws = [w] if tail is None else [w, w_lo]
    return pl.pallas_call(
        functools.partial(_inproj_kernel, tail=tail, prompt_tiles=pt),
        grid=(t // tm,),
        in_specs=_pair_specs(tm, d, pt) + [_resident((1, d)), _resident((1, d))] + [_resident(a.shape) for a in ws],
        out_specs=[row] * 6,
        out_shape=[jax.ShapeDtypeStruct((t, d), dt) for dt in (F32, F32, F32, act_dtype, act_dtype, F32)],
        compiler_params=_params(("parallel",)),
        name="inproj",
    )(xp, xs, g, lb, *ws)


def _score_masks(n_rows, block):
    xr = (lax.broadcasted_iota(I32, (n_rows, n_rows), 0)
          ^ lax.broadcasted_iota(I32, (n_rows, n_rows), 1))
    masks = {0: (xr == 0).astype(F32)}
    b = 1
    while 2 * b < n_rows and b < block:
        masks[b] = (xr < 2 * b).astype(F32)
        b *= 2
    return masks


def _intra_scores(q, k, g, cum, n_rows, block, precise, masks):
    ri = lax.broadcasted_iota(I32, (n_rows, LANES), 0)
    s = masks[0] * jnp.sum(q * k, axis=1, keepdims=True)
    b = 1
    while b < block:
        if b >= SUBLANES:
            shape3 = (n_rows // (2 * b), 2 * b, LANES)
            c3, q3, k3 = cum.reshape(shape3), q.reshape(shape3), k.reshape(shape3)
            mid = c3[:, b - 1:b, :]
            zero = jnp.zeros((shape3[0], b, LANES), F32)
            qt = jnp.concatenate([zero, q3[:, b:, :] * jnp.exp(c3[:, b:, :] - mid)], axis=1).reshape(n_rows, LANES)
            kt = jnp.concatenate([k3[:, :b, :] * jnp.exp(mid - c3[:, :b, :]), zero], axis=1).reshape(n_rows, LANES)
        else:
            upper = (ri & b) != 0
            if b == 1:
                z = jnp.where(upper, g, 0.0)
            elif b == 2:
                g_prev = pltpu.roll(g, 1, 0)
                g_next = pltpu.roll(g, n_rows - 1, 0)
                m4 = ri & 3
                z = jnp.where(m4 == 2, g, jnp.where(m4 == 3, g + g_prev, jnp.where(m4 == 0, g_next, 0.0)))
            else:
                c3 = cum.reshape(n_rows // (2 * b), 2 * b, LANES)
                mid = jnp.broadcast_to(c3[:, b - 1:b, :], c3.shape).reshape(n_rows, LANES)
                z = jnp.where(upper, cum - mid, mid - cum)
            e = jnp.exp(z)
            qt = jnp.where(upper, q * e, 0.0)
            kt = jnp.where(upper, 0.0, k * e)
        prod = _mm(qt, kt, precise, _NT)
        s = s + (prod * masks[b] if b in masks else prod)
        b *= 2
    return s


def _decay_columns(e_row):
    hi = e_row.astype(BF16).astype(F32)
    lo = e_row - hi
    r = lax.broadcasted_iota(I32, (BF16_ROWS, LANES), 0)
    stacked = jnp.where(r == 0, hi, jnp.where(r == 1, lo, 0.0)).astype(BF16)
    return _dg(stacked, jnp.ones((BF16_ROWS, LANES), BF16), _TN)


def _head_out(o, gh, sog, dtype):
    on = o * lax.rsqrt(jnp.mean(o * o, axis=-1, keepdims=True) + EPS) * gh
    return (on * sog.astype(F32)).astype(dtype)


def _scan_prompt_kernel(qs_ref, lf_ref, kf_ref, v_ref, sog_ref, gh_ref, og_ref, sfin_ref, s_scr, *, tail_steps):
    c = pl.program_id(1)
    nc = pl.num_programs(1)
    n = SCAN_CHUNK

    @pl.when(c == 0)
    def _():
        s_scr[...] = jnp.zeros_like(s_scr)

    def body(precise):
        tri = (lax.broadcasted_iota(I32, (n, n), 0) >= lax.broadcasted_iota(I32, (n, n), 1)).astype(BF16)
        masks = _score_masks(n, n)

        def head(h, carry):
            hs = pl.ds(pl.multiple_of(h * HEAD_DIM, HEAD_DIM), HEAD_DIM)
            s_old = s_scr[h]
            for j in range(SCAN_STEP_CHUNKS):
                rs = slice(j * n, (j + 1) * n)
                q, g, k, v = qs_ref[rs, hs], lf_ref[rs, hs], kf_ref[rs, hs], v_ref[rs, hs]
                cum = _dot_exact_lhs(tri, g)
                tot = cum[n - 1:n, :]
                o = _mm(q * jnp.exp(cum), s_old, precise)
                o = o + _mm(_intra_scores(q, k, g, cum, n, n, precise, masks), v, precise)
                ke = k * jnp.exp(tot - cum)
                s_old = _decay_columns(jnp.exp(tot)) * s_old + _mm(ke, v, precise, _TN)
                og_ref[rs, hs] = _head_out(o, gh_ref[:, hs], sog_ref[rs, hs], og_ref.dtype)
            s_scr[h] = s_old
            return carry

        lax.fori_loop(0, N_HEADS, head, 0, unroll=1 if precise else SCAN_HEAD_UNROLL)

    _either(None if tail_steps is None else c >= nc - tail_steps, body)

    @pl.when(c == nc - 1)
    def _():
        sfin_ref[0] = s_scr[...]


def _scan_prompt(qs, lf, kf, v, sog, gh, batch, seq, tail_rows):
    t, d = qs.shape
    n = SCAN_CHUNK * SCAN_STEP_CHUNKS
    assert seq % n == 0 and (tail_rows is None or tail_rows % n == 0)
    nc = seq // n
    row = pl.BlockSpec((n, d), lambda b, c: (b * nc + c, 0))
    return pl.pallas_call(
        functools.partial(_scan_prompt_kernel, tail_steps=None if tail_rows is None else tail_rows // n),
        grid=(batch, nc),
        in_specs=[row, row, row, row, row, _resident((1, d))],
        out_specs=[row, pl.BlockSpec((1, N_HEADS, HEAD_DIM, HEAD_DIM), lambda b, c: (b, 0, 0, 0))],
        out_shape=[jax.ShapeDtypeStruct((t, d), v.dtype),
                   jax.ShapeDtypeStruct((batch, N_HEADS, HEAD_DIM, HEAD_DIM), F32)],
        scratch_shapes=[pltpu.VMEM((N_HEADS, HEAD_DIM, HEAD_DIM), F32)],
        compiler_params=_params(("parallel", "arbitrary")),
        name="scan_prompt",
    )(qs, lf, kf, v, sog, gh)


def _scan_sample_kernel(*refs, steps, precise, chained):
    if chained:
        qs_ref, lf_ref, kf_ref, v_ref, sog_ref, gh_ref, s_ref, _, _, og_ref, snew_ref = refs
    else:
        qs_ref, lf_ref, kf_ref, v_ref, sog_ref, gh_ref, s_ref, _, og_ref, snew_ref = refs
    n = SAMPLE_SEQS * steps
    shift = steps.bit_length() - 1
    ri = lax.broadcasted_iota(I32, (n, n), 0)
    ci = lax.broadcasted_iota(I32, (n, n), 1)
    same = (ri >> shift) == (ci >> shift)
    tri = (same & (ri >= ci)).astype(BF16)
    tri_after = (same & (ci > ri)).astype(BF16)
    seq_of_row = lax.broadcasted_iota(I32, (n, LANES), 0) >> shift
    masks = _score_masks(n, steps)

    def head(h, carry):
        hs = pl.ds(pl.multiple_of(h * HEAD_DIM, HEAD_DIM), HEAD_DIM)
        q, g, k, v = qs_ref[:, hs], lf_ref[:, hs], kf_ref[:, hs], v_ref[:, hs]
        cum = _dot_exact_lhs(tri, g)
        after = _dot_exact_lhs(tri_after, g)
        o = _mm(_intra_scores(q, k, g, cum, n, steps, precise, masks), v, precise)
        qe = q * jnp.exp(cum)
        ke = k * jnp.exp(after)
        for s in range(SAMPLE_SEQS):
            mine = seq_of_row == s
            s_old = s_ref[s, h]
            o = o + jnp.where(mine, _mm(qe, s_old, precise), 0.0)
            kv = _mm(jnp.where(mine, ke, 0.0), v, precise, _TN)
            last = (s + 1) * steps - 1
            snew_ref[s, h] = _decay_columns(jnp.exp(cum[last:last + 1, :])) * s_old + kv
        og_ref[:, hs] = _head_out(o, gh_ref[:, hs], sog_ref[:, hs], og_ref.dtype)
        return carry

    lax.fori_loop(0, N_HEADS, head, 0)


def _scan_sample(qs, lf, kf, v, sog, gh, states, layer, new_states, og_all, row0, steps, precise):
    t, d = qs.shape
    nseq = states.shape[1]
    assert steps & (steps - 1) == 0 and steps >= 4, "sample block must be a power of two >= 4"
    n = SAMPLE_SEQS * steps
    assert nseq % SAMPLE_SEQS == 0 and row0 % n == 0 and n % BF16_ROWS == 0
    b0 = row0 // n
    chained = new_states is not None
    row = pl.BlockSpec((n, d), lambda i: (b0 + i, 0))
    st = pl.BlockSpec((None, SAMPLE_SEQS, N_HEADS, HEAD_DIM, HEAD_DIM), lambda i: (layer, i, 0, 0, 0))
    anywhere = pl.BlockSpec(memory_space=pl.ANY)
    extra = [og_all, new_states] if chained else [og_all]
    return pl.pallas_call(
        functools.partial(_scan_sample_kernel, steps=steps, precise=precise, chained=chained),
        grid=(nseq // SAMPLE_SEQS,),
        in_specs=[row, row, row, row, row, _resident((1, d)), st] + [anywhere] * len(extra),
        out_specs=[row, st],
        out_shape=[jax.ShapeDtypeStruct((t, d), og_all.dtype), jax.ShapeDtypeStruct(states.shape, F32)],
        input_output_aliases={7: 0, 8: 1} if chained else {7: 0},
        compiler_params=_params(("parallel",)),
        name="scan_sample",
    )(qs, lf, kf, v, sog, gh, states, *extra)


def _mix_kernel(*refs, tiles_per_seq, prompt_tiles, steps, tail):
    if tail is None:
        (xp_ref, xs_ref, og_ref, u_ref, halo_ref, e1_ref, e2_ref, g_ref, wc_ref,
         w_cbg_ref, wa_ref, wb_ref, wo_ref, x1_ref) = refs
        l_cbg_ref = la_ref = lb_ref = lo_ref = None
    else:
        (xp_ref, xs_ref, og_ref, u_ref, halo_ref, e1_ref, e2_ref, g_ref, wc_ref,
         w_cbg_ref, wa_ref, wb_ref, wo_ref, l_cbg_ref, la_ref, lb_ref, lo_ref, x1_ref) = refs
    d = D_MODEL
    tm = ROW_TILE
    i = pl.program_id(0)

    def full(ref):
        return None if ref is None else ref[...]

    def body(precise):
        x = _pair_load(i, prompt_tiles, xp_ref, xs_ref)
        h = _rmsnorm(x, g_ref[...])
        u = u_ref[...]
        ri = lax.broadcasted_iota(I32, (tm, d), 0)
        r1 = pltpu.roll(u, 1, 0)
        r2 = pltpu.roll(u, 2, 0)
        keep = jnp.where(i % tiles_per_seq == 0, 0.0, 1.0)
        halo = halo_ref[...]
        h_last = halo[7:8, :] * keep
        h_prev = halo[6:7, :] * keep
        is_sample = i >= prompt_tiles
        t_in_seq = ri & (steps - 1)
        prev1 = jnp.where(is_sample,
                          jnp.where(t_in_seq == 0, e1_ref[...], r1),
                          jnp.where(ri == 0, h_last, r1))
        prev2 = jnp.where(is_sample,
                          jnp.where(t_in_seq < 2, e2_ref[...], r2),
                          jnp.where(ri == 0, h_prev, jnp.where(ri == 1, h_last, r2)))
        wc = wc_ref[...]
        conv = prev2 * wc[0:1, :] + prev1 * wc[1:2, :] + u * wc[2:3, :]

        def seg(j):
            sl = slice(j * d, (j + 1) * d)
            return _mmw(h, w_cbg_ref[:, sl], _w_slice(l_cbg_ref, sl), precise)

        y_b = _mmw(seg(0) * conv, wb_ref[...], full(lb_ref), precise)
        y_a = _mmw(og_ref[...], wa_ref[...], full(la_ref), precise)
        m = jax.nn.sigmoid(seg(1)) * y_a + jax.nn.sigmoid(seg(2)) * y_b
        x1_ref[...] = x + _mmw(m, wo_ref[...], full(lo_ref), precise)

    _either(None if tail is None else _tail_tile(i, *tail), body)


def _mix(xp, xs, og, u, e1, e2, g, wc, ws, ws_lo, seq, steps, tail):
    t, d = og.shape
    tm = ROW_TILE
    prompt_rows = xp.shape[0]
    assert seq % tm == 0 and prompt_rows % tm == 0 and (t - prompt_rows) % tm == 0 and tm % steps == 0
    pt = prompt_rows // tm
    row = pl.BlockSpec((tm, d), lambda i: (i, 0))
    halo = pl.BlockSpec((SUBLANES, d), lambda i: (jnp.maximum(i * (tm // SUBLANES) - 1, 0), 0))
    erow = pl.BlockSpec((tm, d), lambda i: (jnp.maximum(i - pt, 0), 0))
    weights = list(ws) + ([] if tail is None else list(ws_lo))
    return pl.pallas_call(
        functools.partial(_mix_kernel, tiles_per_seq=seq // tm, prompt_tiles=pt, steps=steps, tail=tail),
        grid=(t // tm,),
        in_specs=_pair_specs(tm, d, pt) + [row, row, halo, erow, erow, _resident((1, d)), _resident(wc.shape)]
        + [_resident(a.shape) for a in weights],
        out_specs=row,
        out_shape=jax.ShapeDtypeStruct((t, d), F32),
        compiler_params=_params(("parallel",)),
        name="mix",
    )(xp, xs, og, u, u, e1, e2, g, wc, *weights)


def _route_kernel(x_ref, g_ref, wr_ref, br_ref, xs_ref, rinfo_ref, cinfo_ref):
    tm = MOE_TILE
    rows = MOE_SORTED_ROWS
    hn = _rmsnorm(x_ref[...], g_ref[...])
    hn_hi, hn_lo = _split2(hn)
    wr_hi, wr_lo = _split2(wr_ref[...])
    logits = _dg(hn_hi, wr_hi) + _dg(hn_lo, wr_hi) + _dg(hn_hi, wr_lo) + br_ref[...]

    lane = lax.broadcasted_iota(I32, (tm, LANES), 1)
    neg = -jnp.inf
    big = jnp.int32(LANES)

    def first_lane(mask):
        return jnp.min(jnp.where(mask, lane, big), axis=1, keepdims=True)

    is_group = (lane >= GROUP_LANE0) & (lane < GROUP_LANE0 + N_GROUPS)
    gl = jnp.where(is_group, logits, neg)
    gmax = jnp.max(gl, axis=1, keepdims=True)
    g_idx = first_lane(gl == gmax) - GROUP_LANE0
    g_w = 1.0 / jnp.sum(jnp.exp(gl - gmax), axis=1, keepdims=True)

    in_group = (lane < N_EXPERTS) & ((lane >> (EXPERTS_PER_GROUP.bit_length() - 1)) == g_idx)
    el = jnp.where(in_group, logits, neg)
    emax = jnp.max(el, axis=1, keepdims=True)
    ee = jnp.exp(el - emax)
    prob = ee / jnp.sum(ee, axis=1, keepdims=True)
    prob = jnp.where(in_group, prob, -1.0)
    p1 = jnp.max(prob, axis=1, keepdims=True)
    i1 = first_lane(prob == p1)
    prob2 = jnp.where(lane == i1, -1.0, prob)
    p2 = jnp.max(prob2, axis=1, keepdims=True)
    i2 = first_lane(prob2 == p2)
    psum = p1 + p2
    w1 = p1 / psum * g_w
    w2 = p2 / psum * g_w

    oh1 = (lane == i1).astype(BF16)
    oh2 = (lane == i2).astype(BF16)
    before = (lax.broadcasted_iota(I32, (tm, tm), 1) < lax.broadcasted_iota(I32, (tm, tm), 0)).astype(BF16)
    c1 = _dg(before, oh1)
    c2 = _dg(before, oh2)
    oh1f = oh1.astype(F32)
    oh2f = oh2.astype(F32)
    cnt1 = jnp.sum(oh1f, axis=0, keepdims=True)
    cnt2 = jnp.sum(oh2f, axis=0, keepdims=True)
    chunks = jnp.floor((cnt1 + cnt2 + (BF16_ROWS - 1)) * (1.0 / BF16_ROWS))
    excl = (lax.broadcasted_iota(I32, (LANES, LANES), 0) < lax.broadcasted_iota(I32, (LANES, LANES), 1)).astype(BF16)
    chunk0 = _dg(jnp.broadcast_to(chunks, (BF16_ROWS, LANES)).astype(BF16), excl)[0:1, :]
    base = chunk0 * BF16_ROWS
    pos1 = jnp.sum(oh1f * (base + c1), axis=1, keepdims=True)
    pos2 = jnp.sum(oh2f * (base + cnt1 + c2), axis=1, keepdims=True)

    slot = lax.broadcasted_iota(I32, (tm, rows), 1)
    place = ((slot == pos1.astype(I32)) | (slot == pos2.astype(I32))).astype(BF16)
    xs_ref[0] = _dg(place, hn_hi, _TN).astype(BF16)

    rinfo_ref[...] = jnp.where(lane == 0, pos1, jnp.where(lane == 1, pos2, jnp.where(lane == 2, w1, jnp.where(lane == 3, w2, 0.0))))
    r8 = lax.broadcasted_iota(I32, (SUBLANES, LANES), 0)
    cinfo_ref[0] = jnp.where(r8 == 0, chunk0, jnp.where(r8 == 1, chunks, 0.0)).astype(I32)


def _route(x1, g, wr, br):
    t, d = x1.shape
    tm = MOE_TILE
    assert t % tm == 0
    nt = t // tm
    ns = -(-nt // MOE_SUPER)
    return pl.pallas_call(
        _route_kernel,
        grid=(nt,),
        in_specs=[pl.BlockSpec((tm, d), lambda i: (i, 0)), _resident((1, d)), _resident(wr.shape), _resident(br.shape)],
        out_specs=[pl.BlockSpec((1, MOE_SORTED_ROWS, d), lambda i: (i, 0, 0)),
                   pl.BlockSpec((tm, LANES), lambda i: (i, 0)),
                   pl.BlockSpec((1, SUBLANES, LANES), lambda i: (i, 0, 0))],
        out_shape=[jax.ShapeDtypeStruct((ns * MOE_SUPER, MOE_SORTED_ROWS, d), BF16),
                   jax.ShapeDtypeStruct((t, LANES), F32),
                   jax.ShapeDtypeStruct((nt, SUBLANES, LANES), I32)],
        compiler_params=_params(("parallel",)),
        name="route",
    )(x1, g, wr, br)


def _ffn_kernel(src_ref, start_ref, count_ref, xs_ref, wgu_ref, wd_ref, out_ref, lhs_scr, res_scr):
    s = pl.program_id(0)
    e = pl.program_id(1)
    rows = BF16_ROWS
    nsrc = MOE_SUPER * MOE_CHUNKS

    @pl.when(e == 0)
    def _():
        out_ref[...] = jnp.zeros_like(out_ref)

    n = count_ref[s * N_EXPERTS + e]
    k0 = start_ref[s * N_EXPERTS + e]

    def block(b, carry):
        srcs = []
        for i in range(MOE_BLOCK_CHUNKS):
            k = b * MOE_BLOCK_CHUNKS + i
            valid = k < n
            src = src_ref[s * nsrc + k0 + jnp.where(valid, k, 0)]
            off = pl.multiple_of(src * rows, rows)
            lhs_scr[i * rows:(i + 1) * rows, :] = xs_ref[0, pl.ds(off, rows), :]
            srcs.append((valid, off))
        lhs = lhs_scr[...]
        gu = _dg(lhs, wgu_ref[0])
        gate = gu[:, :D_EXPERT]
        hid = gate * jax.nn.sigmoid(gate) * gu[:, D_EXPERT:]
        res_scr[...] = _dg(hid.astype(BF16), wd_ref[0]).astype(BF16)
        for i, (valid, off) in enumerate(srcs):
            @pl.when(valid)
            def _():
                out_ref[0, pl.ds(off, rows), :] = res_scr[i * rows:(i + 1) * rows, :]
        return carry

    lax.fori_loop(0, (n + MOE_BLOCK_CHUNKS - 1) // MOE_BLOCK_CHUNKS, block, 0)


def _ffn(xs, chunk_src, e_start, e_count, wgu, wd):
    ns, srows, d = xs.shape
    assert srows == MOE_SUPER * MOE_SORTED_ROWS
    blk = pl.BlockSpec((1, srows, d), lambda s, e, *_: (s, 0, 0), pipeline_mode=pl.Buffered(1))
    brows = MOE_BLOCK_CHUNKS * BF16_ROWS
    return pl.pallas_call(
        _ffn_kernel,
        grid_spec=pltpu.PrefetchScalarGridSpec(
            num_scalar_prefetch=3,
            grid=(ns, N_EXPERTS),
            in_specs=[blk,
                      pl.BlockSpec((1, d, 2 * D_EXPERT), lambda s, e, *_: (e, 0, 0)),
                      pl.BlockSpec((1, D_EXPERT, d), lambda s, e, *_: (e, 0, 0))],
            out_specs=blk,
            scratch_shapes=[pltpu.VMEM((brows, d), BF16), pltpu.VMEM((brows, d), BF16)],
        ),
        out_shape=jax.ShapeDtypeStruct(xs.shape, BF16),
        compiler_params=_params(("parallel", "arbitrary")),
        name="expert_ffn",
    )(chunk_src, e_start, e_count, xs, wgu, wd)


def _chunk_tables(cinfo):
    nt = cinfo.shape[0]
    ns = -(-nt // MOE_SUPER)
    pad = ns * MOE_SUPER - nt
    chunk0 = jnp.pad(cinfo[:, 0, :N_EXPERTS], ((0, pad), (0, 0)))
    chunks = jnp.pad(cinfo[:, 1, :N_EXPERTS], ((0, pad), (0, 0)))
    ends = chunk0 + chunks
    c = jnp.arange(MOE_CHUNKS, dtype=I32)
    label = jnp.sum((c[None, :, None] >= ends[:, None, :]).astype(I32), axis=-1)
    label = label.reshape(ns, MOE_SUPER * MOE_CHUNKS)
    chunk_src = jnp.argsort(label, axis=-1, stable=True).astype(I32)
    e_count = chunks.reshape(ns, MOE_SUPER, N_EXPERTS).sum(axis=1).astype(I32)
    e_start = (jnp.cumsum(e_count, axis=-1) - e_count).astype(I32)
    return chunk_src.reshape(-1), e_start.reshape(-1), e_count.reshape(-1)


def _ple_kernel(x1_ref, ys_ref, rinfo_ref, pp_ref, ps_ref, g_ref, wg_ref, wp_ref, gfin_ref, outp_ref, outs_ref, *,
                final, prompt_tiles):
    tm = MOE_TILE
    rows = MOE_SORTED_ROWS
    i = pl.program_id(0)
    rinfo = rinfo_ref[...]
    pos1 = rinfo[:, 0:1].astype(I32)
    pos2 = rinfo[:, 1:2].astype(I32)
    w1 = rinfo[:, 2:3]
    w2 = rinfo[:, 3:4]
    slot = lax.broadcasted_iota(I32, (tm, rows), 1)
    ys = ys_ref[0]
    y1 = _dg((slot == pos1).astype(BF16), ys)
    y2 = _dg((slot == pos2).astype(BF16), ys)
    x = x1_ref[...] + (w1 * y1 + w2 * y2)
    gate = jax.nn.sigmoid(_dg(_rmsnorm(x, g_ref[...]).astype(BF16), wg_ref[...]))
    p = _pair_load(i, prompt_tiles, pp_ref, ps_ref)
    x = x + gate * _dg(p.astype(BF16), wp_ref[...])
    if final:
        x = _rmsnorm(x, gfin_ref[...])

    @pl.when(i < prompt_tiles)
    def _():
        outp_ref[...] = x

    @pl.when(i >= prompt_tiles)
    def _():
        outs_ref[...] = x


def _ple(x1, ys, rinfo, p_prompt, p_sample, layer, g, wg, wp, gfin, final):
    t, d = x1.shape
    tm = MOE_TILE
    nt = t // tm
    tp, ts = p_prompt.shape[1], p_sample.shape[1]
    assert tp % tm == 0 and ts % tm == 0 and tp + ts == t
    pt = tp // tm
    pd = p_prompt.shape[2]
    ys = ys.reshape(-1, MOE_SORTED_ROWS, d)
    return pl.pallas_call(
        functools.partial(_ple_kernel, final=final, prompt_tiles=pt),
        grid=(nt,),
        in_specs=[pl.BlockSpec((tm, d), lambda i: (i, 0)),
                  pl.BlockSpec((1, MOE_SORTED_ROWS, d), lambda i: (i, 0, 0)),
                  pl.BlockSpec((tm, LANES), lambda i: (i, 0)),
                  pl.BlockSpec((None, tm, pd), lambda i: (layer, jnp.minimum(i, pt - 1), 0)),
                  pl.BlockSpec((None, tm, pd), lambda i: (layer, jnp.maximum(i - pt, 0), 0)),
                  _resident((1, d)), _resident(wg.shape), _resident(wp.shape), _resident((1, d))],
        out_specs=_pair_specs(tm, d, pt),
        out_shape=[jax.ShapeDtypeStruct((tp, d), F32), jax.ShapeDtypeStruct((ts, d), F32)],
        compiler_params=_params(("arbitrary",)),
        name="ple",
    )(x1, ys, rinfo, p_prompt, p_sample, g, wg, wp, gfin)


def _lower_bounds(lb_param):
    s = jax.nn.softmax(lb_param.astype(F32), axis=0)
    c = jnp.cumsum(s, axis=0)
    return c - c[0:1]


def _split_kernel(w_ref, hi_ref, lo_ref):
    hi_ref[...], lo_ref[...] = _split2(w_ref[...])


def _hi_lo(w):
    k, n = w.shape
    tk = ROW_TILE
    assert k % tk == 0
    spec = pl.BlockSpec((tk, n), lambda i: (i, 0))
    return pl.pallas_call(
        _split_kernel,
        grid=(k // tk,),
        in_specs=[spec],
        out_specs=[spec, spec],
        out_shape=[jax.ShapeDtypeStruct((k, n), BF16)] * 2,
        compiler_params=_params(("parallel",)),
        name="split_weight",
    )(w)


def kernel(x_prompt, x_sample, state_hgrn, state_conv, p_prompt, p_sample, g_mix, w_in, hg_lower, g_hg_out, w_br_a, w_conv, w_br_b, w_out, g_ffn, w_router_group, b_router_group, w_router_expert, b_router_expert, w_gate, w_up, w_down, g_ple, w_ple_gate, w_ple_proj, g_final):
    depth = w_in.shape[0]
    bp, seq, d = x_prompt.shape
    bs, steps, _ = x_sample.shape
    assert d == D_MODEL and w_conv.shape[1] == CONV_W
    assert TAIL_ROWS % ROW_TILE == 0 and seq >= TAIL_ROWS
    tp = bp * seq
    ts = bs * steps
    hw = N_HEADS * HEAD_DIM

    xp, xs_rows = x_prompt.reshape(tp, d), x_sample.reshape(ts, d)
    pp = p_prompt.reshape(depth, tp, -1)
    ps = p_sample.reshape(depth, ts, -1)
    lbs = _lower_bounds(hg_lower)
    row = lambda a: a.reshape(1, -1)

    hg_p, cv_p, cv_s = [], [], []
    hg_s = None
    for li in range(depth):
        careful = li < depth - 1
        tail = (seq // ROW_TILE, TAIL_ROWS // ROW_TILE, tp // ROW_TILE) if careful else None
        act = F32 if careful else BF16
        wi = w_in[li]
        w_scan = jnp.concatenate([wi[:, :4 * hw], wi[:, 4 * hw + d:4 * hw + 3 * d]], axis=1)
        w_cbg = jnp.concatenate([wi[:, 4 * hw:4 * hw + d], wi[:, 4 * hw + 3 * d:]], axis=1)
        mix_w = (w_cbg, w_br_a[li], w_br_b[li], w_out[li])
        if careful:
            w_scan, w_scan_lo = _hi_lo(w_scan)
            mix_w, mix_w_lo = zip(*[_hi_lo(w) for w in mix_w])
        else:
            w_scan, w_scan_lo = w_scan.astype(BF16), None
            mix_w, mix_w_lo = [w.astype(BF16) for w in mix_w], None

        qs, lf, kf, v, sog, u = _inproj(xp, xs_rows, row(g_mix[li]), row(lbs[li]), w_scan, w_scan_lo, tail, act)

        gh = row(g_hg_out[li])
        og, s_p = _scan_prompt(qs, lf, kf, v, sog, gh, bp, seq, TAIL_ROWS if careful else None)
        og, hg_s = _scan_sample(qs, lf, kf, v, sog, gh, state_hgrn, li, hg_s, og, tp, steps, careful)
        hg_p.append(s_p)
        cv_p.append(u[:tp].reshape(bp, seq, d)[:, seq - (CONV_W - 1):])
        cv_s.append(jnp.concatenate([state_conv[li], u[tp:].reshape(bs, steps, d)], axis=1)[:, steps:])

        buf = state_conv[li]
        zero = jnp.zeros((bs, steps - 1, d), F32)
        e1 = jnp.concatenate([buf[:, 1:2], zero], axis=1).reshape(ts, d)
        e2 = jnp.concatenate([buf, zero[:, 1:]], axis=1).reshape(ts, d)
        x1 = _mix(xp, xs_rows, og, u, e1, e2, row(g_mix[li]), w_conv[li], mix_w, mix_w_lo, seq, steps, tail)

        wr = jnp.zeros((d, LANES), F32).at[:, :N_EXPERTS].set(w_router_expert[li])
        wr = wr.at[:, GROUP_LANE0:GROUP_LANE0 + N_GROUPS].set(w_router_group[li])
        br = jnp.zeros((1, LANES), F32).at[0, :N_EXPERTS].set(b_router_expert[li])
        br = br.at[0, GROUP_LANE0:GROUP_LANE0 + N_GROUPS].set(b_router_group[li])
        xsort, rinfo, cinfo = _route(x1, row(g_ffn[li]), wr, br)
        chunk_src, e_start, e_count = _chunk_tables(cinfo)
        xsort = xsort.reshape(-1, MOE_SUPER * MOE_SORTED_ROWS, d)
        wgu = jnp.concatenate([w_gate[li], w_up[li]], axis=-1).astype(BF16)
        ys = _ffn(xsort, chunk_src, e_start, e_count, wgu, w_down[li].astype(BF16))

        xp, xs_rows = _ple(x1, ys, rinfo, pp, ps, li, row(g_ple[li]), w_ple_gate[li].astype(BF16),
                           w_ple_proj[li].astype(BF16), row(g_final), li == depth - 1)

    return (xp.reshape(bp, seq, d), xs_rows.reshape(bs, steps, d), jnp.stack(hg_p), jnp.stack(cv_p), hg_s,
            jnp.stack(cv_s))
```

```python
import functools

import jax
import jax.numpy as jnp
from jax import lax
from jax.experimental import pallas as pl
from jax.experimental.pallas import tpu as pltpu

F32 = jnp.float32
BF16 = jnp.bfloat16
I32 = jnp.int32

D_MODEL = 1024
N_HEADS = 8
HEAD_DIM = 128
CONV_W = 3
N_GROUPS = 4
EXPERTS_PER_GROUP = 8
N_EXPERTS = N_GROUPS * EXPERTS_PER_GROUP
D_EXPERT = 256
EPS = 1e-6

LANES = 128
SUBLANES = 8
BF16_ROWS = 16
VMEM_LIMIT = 56 * 1024 * 1024

ROW_TILE = 256
SCAN_CHUNK = 128
SCAN_STEP_CHUNKS = 2
SCAN_HEAD_UNROLL = 2
TAIL_ROWS = 256
SAMPLE_SEQS = 8
MOE_TILE = 256
MOE_SORTED_ROWS = 1024
MOE_CHUNKS = MOE_SORTED_ROWS // BF16_ROWS
MOE_BLOCK_CHUNKS = 16
GROUP_LANE0 = N_EXPERTS

_NN = (((1,), (0,)), ((), ()))
_NT = (((1,), (1,)), ((), ()))
_TN = (((0,), (0,)), ((), ()))


def _params(sem):
    return pltpu.CompilerParams(dimension_semantics=sem, vmem_limit_bytes=VMEM_LIMIT)


def _resident(shape):
    nd = len(shape)
    return pl.BlockSpec(shape, lambda *_: (0,) * nd, pipeline_mode=pl.Buffered(1))


def _rmsnorm(x, g):
    r = lax.rsqrt(jnp.mean(x * x, axis=-1, keepdims=True) + EPS)
    return (x * r) * g


def _dg(a, b, dims=_NN):
    return lax.dot_general(a, b, dims, preferred_element_type=F32)


def _split2(x):
    hi = x.astype(BF16)
    return hi, (x - hi.astype(F32)).astype(BF16)


def _mm(a, b, precise, dims=_NN):
    if not precise:
        return _dg(a.astype(BF16), b.astype(BF16), dims)
    ah, al = _split2(a.astype(F32))
    bh, bl = _split2(b.astype(F32))
    return _dg(ah, bh, dims) + (_dg(al, bh, dims) + _dg(ah, bl, dims))


def _mmw(a, w_hi, w_lo, precise):
    if not precise:
        return _dg(a.astype(BF16), w_hi)
    ah, al = _split2(a.astype(F32))
    return _dg(ah, w_hi) + (_dg(al, w_hi) + _dg(ah, w_lo))


def _split3(x):
    hi = x.astype(BF16)
    r1 = x - hi.astype(F32)
    mid = r1.astype(BF16)
    lo = (r1 - mid.astype(F32)).astype(BF16)
    return hi, mid, lo


def _dot_exact_lhs(m01, x):
    hi, mid, lo = _split3(x)
    return _dg(m01, hi) + _dg(m01, mid) + _dg(m01, lo)


def _either(precise_pred, body):
    if precise_pred is None:
        body(False)
    else:
        pl.when(precise_pred)(lambda: body(True))
        pl.when(jnp.logical_not(precise_pred))(lambda: body(False))


def _tail_tile(i, tiles_per_seq, tail_tiles, prompt_tiles):
    return (i >= prompt_tiles) | ((i % tiles_per_seq) >= tiles_per_seq - tail_tiles)


def _pair_specs(tm, cols, prompt_tiles):
    return [pl.BlockSpec((tm, cols), lambda i: (jnp.minimum(i, prompt_tiles - 1), 0)),
            pl.BlockSpec((tm, cols), lambda i: (jnp.maximum(i - prompt_tiles, 0), 0))]


def _pair_load(i, prompt_tiles, p_ref, s_ref):
    return jnp.where(i < prompt_tiles, p_ref[...], s_ref[...])


def _w_slice(ref, sl):
    return None if ref is None else ref[:, sl]


def _inproj_kernel(*refs, tail, prompt_tiles):
    if tail is None:
        xp_ref, xs_ref, g_ref, lb_ref, w_ref, qs_ref, lf_ref, kf_ref, v_ref, sog_ref, u_ref = refs
        wl_ref = None
    else:
        xp_ref, xs_ref, g_ref, lb_ref, w_ref, wl_ref, qs_ref, lf_ref, kf_ref, v_ref, sog_ref, u_ref = refs
    d = D_MODEL
    i = pl.program_id(0)

    def body(precise):
        h = _rmsnorm(_pair_load(i, prompt_tiles, xp_ref, xs_ref), g_ref[...])

        def seg(j):
            sl = slice(j * d, (j + 1) * d)
            return _mmw(h, w_ref[:, sl], _w_slice(wl_ref, sl), precise)

        q = seg(0)
        qs_ref[...] = q * jax.nn.sigmoid(q)
        fp = seg(1)
        lb = lb_ref[...]
        log_sig = jnp.minimum(fp, 0.0) - jnp.log1p(jnp.exp(-jnp.abs(fp)))
        a = jnp.log(lb)
        b = jnp.log1p(-lb) + log_sig
        lf_ref[...] = jnp.maximum(a, b) + jnp.log1p(jnp.exp(-jnp.abs(a - b)))
        kf_ref[...] = (1.0 - lb) * jax.nn.sigmoid(-fp)
        v_ref[...] = seg(2).astype(v_ref.dtype)
        og = seg(3)
        sog_ref[...] = (og * jax.nn.sigmoid(og)).astype(sog_ref.dtype)
        u_ref[...] = seg(4) * seg(5)

    _either(None if tail is None else _tail_tile(i, *tail), body)


def _inproj(xp, xs, g, lb, w, w_lo, tail, act_dtype):
    d = D_MODEL
    tm = ROW_TILE
    assert xp.shape[0] % tm == 0 and xs.shape[0] % tm == 0 and xp.shape[1] == d
    pt = xp.shape[0] // tm
    t = xp.shape[0] + xs.shape[0]
    row = pl.BlockSpec((tm, d), lambda i: (i, 0))
    ws = [w] if tail is None else [w, w_lo]
    return pl.pallas_call(
        functools.partial(_inproj_kernel, tail=tail, prompt_tiles=pt),
        grid=(t // tm,),
        in_specs=_pair_specs(tm, d, pt) + [_resident((1, d)), _resident((1, d))] + [_resident(a.shape) for a in ws],
        out_specs=[row] * 6,
        out_shape=[jax.ShapeDtypeStruct((t, d), dt) for dt in (F32, F32, F32, act_dtype, act_dtype, F32)],
        compiler_params=_params(("parallel",)),
        name="inproj",
    )(xp, xs, g, lb, *ws)


def _score_masks(n_rows, block):
    xr = (lax.broadcasted_iota(I32, (n_rows, n_rows), 0)
          ^ lax.broadcasted_iota(I32, (n_rows, n_rows), 1))
    masks = {0: (xr == 0).astype(F32)}
    b = 1
    while 2 * b < n_rows and b < block:
        masks[b] = (xr < 2 * b).astype(F32)
        b *= 2
    return masks


def _intra_scores(q, k, g, cum, n_rows, block, precise, masks):
    ri = lax.broadcasted_iota(I32, (n_rows, LANES), 0)
    s = masks[0] * jnp.sum(q * k, axis=1, keepdims=True)
    b = 1
    while b < block:
        if b >= SUBLANES:
            shape3 = (n_rows // (2 * b), 2 * b, LANES)
            c3, q3, k3 = cum.reshape(shape3), q.reshape(shape3), k.reshape(shape3)
            mid = c3[:, b - 1:b, :]
            zero = jnp.zeros((shape3[0], b, LANES), F32)
            qt = jnp.concatenate([zero, q3[:, b:, :] * jnp.exp(c3[:, b:, :] - mid)], axis=1).reshape(n_rows, LANES)
            kt = jnp.concatenate([k3[:, :b, :] * jnp.exp(mid - c3[:, :b, :]), zero], axis=1).reshape(n_rows, LANES)
        else:
            upper = (ri & b) != 0
            if b == 1:
                z = jnp.where(upper, g, 0.0)
            elif b == 2:
                g_prev = pltpu.roll(g, 1, 0)
                g_next = pltpu.roll(g, n_rows - 1, 0)
                m4 = ri & 3
                z = jnp.where(m4 == 2, g, jnp.where(m4 == 3, g + g_prev, jnp.where(m4 == 0, g_next, 0.0)))
            else:
                c3 = cum.reshape(n_rows // (2 * b), 2 * b, LANES)
                mid = jnp.broadcast_to(c3[:, b - 1:b, :], c3.shape).reshape(n_rows, LANES)
                z = jnp.where(upper, cum - mid, mid - cum)
            e = jnp.exp(z)
            qt = jnp.where(upper, q * e, 0.0)
            kt = jnp.where(upper, 0.0, k * e)
        prod = _mm(qt, kt, precise, _NT)
        s = s + (prod * masks[b] if b in masks else prod)
        b *= 2
    return s


def _decay_columns(e_row):
    hi = e_row.astype(BF16).astype(F32)
    lo = e_row - hi
    r = lax.broadcasted_iota(I32, (BF16_ROWS, LANES), 0)
    stacked = jnp.where(r == 0, hi, jnp.where(r == 1, lo, 0.0)).astype(BF16)
    return _dg(stacked, jnp.ones((BF16_ROWS, LANES), BF16), _TN)


def _head_out(o, gh, sog, dtype):
    on = o * lax.rsqrt(jnp.mean(o * o, axis=-1, keepdims=True) + EPS) * gh
    return (on * sog.astype(F32)).astype(dtype)


def _scan_prompt_kernel(qs_ref, lf_ref, kf_ref, v_ref, sog_ref, gh_ref, og_ref, sfin_ref, s_scr, *, tail_steps):
    c = pl.program_id(1)
    nc = pl.num_programs(1)
    n = SCAN_CHUNK

    @pl.when(c == 0)
    def _():
        s_scr[...] = jnp.zeros_like(s_scr)

    def body(precise):
        tri = (lax.broadcasted_iota(I32, (n, n), 0) >= lax.broadcasted_iota(I32, (n, n), 1)).astype(BF16)
        masks = _score_masks(n, n)

        def head(h, carry):
            hs = pl.ds(pl.multiple_of(h * HEAD_DIM, HEAD_DIM), HEAD_DIM)
            s_old = s_scr[h]
            for j in range(SCAN_STEP_CHUNKS):
                rs = slice(j * n, (j + 1) * n)
                q, g, k, v = qs_ref[rs, hs], lf_ref[rs, hs], kf_ref[rs, hs], v_ref[rs, hs]
                cum = _dot_exact_lhs(tri, g)
                tot = cum[n - 1:n, :]
                o = _mm(q * jnp.exp(cum), s_old, precise)
                o = o + _mm(_intra_scores(q, k, g, cum, n, n, precise, masks), v, precise)
                ke = k * jnp.exp(tot - cum)
                s_old = _decay_columns(jnp.exp(tot)) * s_old + _mm(ke, v, precise, _TN)
                og_ref[rs, hs] = _head_out(o, gh_ref[:, hs], sog_ref[rs, hs], og_ref.dtype)
            s_scr[h] = s_old
            return carry

        lax.fori_loop(0, N_HEADS, head, 0, unroll=1 if precise else SCAN_HEAD_UNROLL)

    _either(None if tail_steps is None else c >= nc - tail_steps, body)

    @pl.when(c == nc - 1)
    def _():
        sfin_ref[0] = s_scr[...]


def _scan_prompt(qs, lf, kf, v, sog, gh, batch, seq, tail_rows):
    t, d = qs.shape
    n = SCAN_CHUNK * SCAN_STEP_CHUNKS
    assert seq % n == 0 and (tail_rows is None or tail_rows % n == 0)
    nc = seq // n
    row = pl.BlockSpec((n, d), lambda b, c: (b * nc + c, 0))
    return pl.pallas_call(
        functools.partial(_scan_prompt_kernel, tail_steps=None if tail_rows is None else tail_rows // n),
        grid=(batch, nc),
        in_specs=[row, row, row, row, row, _resident((1, d))],
        out_specs=[row, pl.BlockSpec((1, N_HEADS, HEAD_DIM, HEAD_DIM), lambda b, c: (b, 0, 0, 0))],
        out_shape=[jax.ShapeDtypeStruct((batch * seq, d), v.dtype),
                   jax.ShapeDtypeStruct((batch, N_HEADS, HEAD_DIM, HEAD_DIM), F32)],
        scratch_shapes=[pltpu.VMEM((N_HEADS, HEAD_DIM, HEAD_DIM), F32)],
        compiler_params=_params(("parallel", "arbitrary")),
        name="scan_prompt",
    )(qs, lf, kf, v, sog, gh)


def _scan_sample_kernel(*refs, steps, precise, layer, chained):
    if chained:
        qs_ref, lf_ref, kf_ref, v_ref, sog_ref, gh_ref, s_ref, _, og_ref, snew_ref = refs
    else:
        qs_ref, lf_ref, kf_ref, v_ref, sog_ref, gh_ref, s_ref, og_ref, all_ref = refs
        for other in range(all_ref.shape[0]):
            if other != layer:
                all_ref[other] = jnp.zeros(all_ref.shape[1:], F32)
        snew_ref = all_ref.at[layer]
    n = SAMPLE_SEQS * steps
    shift = steps.bit_length() - 1
    ri = lax.broadcasted_iota(I32, (n, n), 0)
    ci = lax.broadcasted_iota(I32, (n, n), 1)
    same = (ri >> shift) == (ci >> shift)
    tri = (same & (ri >= ci)).astype(BF16)
    tri_after = (same & (ci > ri)).astype(BF16)
    seq_of_row = lax.broadcasted_iota(I32, (n, LANES), 0) >> shift
    masks = _score_masks(n, steps)

    def head(h, carry):
        hs = pl.ds(pl.multiple_of(h * HEAD_DIM, HEAD_DIM), HEAD_DIM)
        q, g, k, v = qs_ref[:, hs], lf_ref[:, hs], kf_ref[:, hs], v_ref[:, hs]
        cum = _dot_exact_lhs(tri, g)
        after = _dot_exact_lhs(tri_after, g)
        o = _mm(_intra_scores(q, k, g, cum, n, steps, precise, masks), v, precise)
        qe = q * jnp.exp(cum)
        ke = k * jnp.exp(after)
        for s in range(SAMPLE_SEQS):
            mine = seq_of_row == s
            s_old = s_ref[s, h]
            o = o + jnp.where(mine, _mm(qe, s_old, precise), 0.0)
            kv = _mm(jnp.where(mine, ke, 0.0), v, precise, _TN)
            last = (s + 1) * steps - 1
            snew_ref[s, h] = _decay_columns(jnp.exp(cum[last:last + 1, :])) * s_old + kv
        og_ref[:, hs] = _head_out(o, gh_ref[:, hs], sog_ref[:, hs], og_ref.dtype)
        return carry

    lax.fori_loop(0, N_HEADS, head, 0)


def _scan_sample(qs, lf, kf, v, sog, gh, states, layer, new_states, row0, steps, precise):
    t, d = qs.shape
    depth, nseq = states.shape[:2]
    assert steps & (steps - 1) == 0 and steps >= 4, "sample block must be a power of two >= 4"
    n = SAMPLE_SEQS * steps
    assert nseq % SAMPLE_SEQS == 0 and row0 % n == 0 and n % BF16_ROWS == 0
    b0 = row0 // n
    chained = new_states is not None
    row = pl.BlockSpec((n, d), lambda i: (b0 + i, 0))
    tail = (SAMPLE_SEQS, N_HEADS, HEAD_DIM, HEAD_DIM)
    st = pl.BlockSpec((None,) + tail, lambda i: (layer, i, 0, 0, 0))
    st_all = pl.BlockSpec((depth,) + tail, lambda i: (0, i, 0, 0, 0))
    extra = [new_states] if chained else []
    return pl.pallas_call(
        functools.partial(_scan_sample_kernel, steps=steps, precise=precise, layer=layer, chained=chained),
        grid=(nseq // SAMPLE_SEQS,),
        in_specs=[row, row, row, row, row, _resident((1, d)), st] + [pl.BlockSpec(memory_space=pl.ANY)] * len(extra),
        out_specs=[pl.BlockSpec((n, d), lambda i: (i, 0)), st if chained else st_all],
        out_shape=[jax.ShapeDtypeStruct((nseq * steps, d), v.dtype), jax.ShapeDtypeStruct(states.shape, F32)],
        input_output_aliases={7: 1} if chained else {},
        compiler_params=_params(("parallel",)),
        name="scan_sample",
    )(qs, lf, kf, v, sog, gh, states, *extra)


def _mix_kernel(*refs, tiles_per_seq, prompt_tiles, steps, tail):
    if tail is None:
        (xp_ref, xs_ref, ogp_ref, ogs_ref, u_ref, halo_ref, e1_ref, e2_ref, g_ref, wc_ref,
         w_cbg_ref, wa_ref, wb_ref, wo_ref, x1_ref) = refs
        l_cbg_ref = la_ref = lb_ref = lo_ref = None
    else:
        (xp_ref, xs_ref, ogp_ref, ogs_ref, u_ref, halo_ref, e1_ref, e2_ref, g_ref, wc_ref,
         w_cbg_ref, wa_ref, wb_ref, wo_ref, l_cbg_ref, la_ref, lb_ref, lo_ref, x1_ref) = refs
    d = D_MODEL
    tm = ROW_TILE
    i = pl.program_id(0)

    def full(ref):
        return None if ref is None else ref[...]

    def body(precise):
        x = _pair_load(i, prompt_tiles, xp_ref, xs_ref)
        h = _rmsnorm(x, g_ref[...])
        u = u_ref[...]
        ri = lax.broadcasted_iota(I32, (tm, d), 0)
        r1 = pltpu.roll(u, 1, 0)
        r2 = pltpu.roll(u, 2, 0)
        keep = jnp.where(i % tiles_per_seq == 0, 0.0, 1.0)
        halo = halo_ref[...]
        h_last = halo[7:8, :] * keep
        h_prev = halo[6:7, :] * keep
        is_sample = i >= prompt_tiles
        t_in_seq = ri & (steps - 1)
        prev1 = jnp.where(is_sample,
                          jnp.where(t_in_seq == 0, e1_ref[...], r1),
                          jnp.where(ri == 0, h_last, r1))
        prev2 = jnp.where(is_sample,
                          jnp.where(t_in_seq < 2, e2_ref[...], r2),
                          jnp.where(ri == 0, h_prev, jnp.where(ri == 1, h_last, r2)))
        wc = wc_ref[...]
        conv = prev2 * wc[0:1, :] + prev1 * wc[1:2, :] + u * wc[2:3, :]

        def seg(j):
            sl = slice(j * d, (j + 1) * d)
            return _mmw(h, w_cbg_ref[:, sl], _w_slice(l_cbg_ref, sl), precise)

        y_b = _mmw(seg(0) * conv, wb_ref[...], full(lb_ref), precise)
        y_a = _mmw(_pair_load(i, prompt_tiles, ogp_ref, ogs_ref), wa_ref[...], full(la_ref), precise)
        m = jax.nn.sigmoid(seg(1)) * y_a + jax.nn.sigmoid(seg(2)) * y_b
        x1_ref[...] = x + _mmw(m, wo_ref[...], full(lo_ref), precise)

    _either(None if tail is None else _tail_tile(i, *tail), body)


def _mix(xp, xs, ogp, ogs, u, e1, e2, g, wc, ws, ws_lo, seq, steps, tail):
    t, d = u.shape
    tm = ROW_TILE
    prompt_rows = xp.shape[0]
    assert seq % tm == 0 and prompt_rows % tm == 0 and (t - prompt_rows) % tm == 0 and tm % steps == 0
    pt = prompt_rows // tm
    row = pl.BlockSpec((tm, d), lambda i: (i, 0))
    halo = pl.BlockSpec((SUBLANES, d), lambda i: (jnp.maximum(i * (tm // SUBLANES) - 1, 0), 0))
    erow = pl.BlockSpec((tm, d), lambda i: (jnp.maximum(i - pt, 0), 0))
    weights = list(ws) + ([] if tail is None else list(ws_lo))
    return pl.pallas_call(
        functools.partial(_mix_kernel, tiles_per_seq=seq // tm, prompt_tiles=pt, steps=steps, tail=tail),
        grid=(t // tm,),
        in_specs=_pair_specs(tm, d, pt) + _pair_specs(tm, d, pt)
        + [row, halo, erow, erow, _resident((1, d)), _resident(wc.shape)]
        + [_resident(a.shape) for a in weights],
        out_specs=row,
        out_shape=jax.ShapeDtypeStruct((t, d), F32),
        compiler_params=_params(("parallel",)),
        name="mix",
    )(xp, xs, ogp, ogs, u, u, e1, e2, g, wc, *weights)


def _route_kernel(x_ref, g_ref, wr_ref, br_ref, xs_ref, zeros_ref, rinfo_ref, cinfo_ref):
    tm = MOE_TILE
    rows = MOE_SORTED_ROWS
    zeros_ref[...] = jnp.zeros_like(zeros_ref)
    hn = _rmsnorm(x_ref[...], g_ref[...])
    hn_hi, hn_lo = _split2(hn)
    wr_hi, wr_lo = _split2(wr_ref[...])
    logits = _dg(hn_hi, wr_hi) + _dg(hn_lo, wr_hi) + _dg(hn_hi, wr_lo) + br_ref[...]

    lane = lax.broadcasted_iota(I32, (tm, LANES), 1)
    neg = -jnp.inf
    big = jnp.int32(LANES)

    def first_lane(mask):
        return jnp.min(jnp.where(mask, lane, big), axis=1, keepdims=True)

    is_group = (lane >= GROUP_LANE0) & (lane < GROUP_LANE0 + N_GROUPS)
    gl = jnp.where(is_group, logits, neg)
    gmax = jnp.max(gl, axis=1, keepdims=True)
    g_idx = first_lane(gl == gmax) - GROUP_LANE0
    g_w = 1.0 / jnp.sum(jnp.exp(gl - gmax), axis=1, keepdims=True)

    in_group = (lane < N_EXPERTS) & ((lane >> (EXPERTS_PER_GROUP.bit_length() - 1)) == g_idx)
    el = jnp.where(in_group, logits, neg)
    emax = jnp.max(el, axis=1, keepdims=True)
    ee = jnp.exp(el - emax)
    prob = ee / jnp.sum(ee, axis=1, keepdims=True)
    prob = jnp.where(in_group, prob, -1.0)
    p1 = jnp.max(prob, axis=1, keepdims=True)
    i1 = first_lane(prob == p1)
    prob2 = jnp.where(lane == i1, -1.0, prob)
    p2 = jnp.max(prob2, axis=1, keepdims=True)
    i2 = first_lane(prob2 == p2)
    psum = p1 + p2
    w1 = p1 / psum * g_w
    w2 = p2 / psum * g_w

    oh1 = (lane == i1).astype(BF16)
    oh2 = (lane == i2).astype(BF16)
    before = (lax.broadcasted_iota(I32, (tm, tm), 1) < lax.broadcasted_iota(I32, (tm, tm), 0)).astype(BF16)
    c1 = _dg(before, oh1)
    c2 = _dg(before, oh2)
    oh1f = oh1.astype(F32)
    oh2f = oh2.astype(F32)
    cnt1 = jnp.sum(oh1f, axis=0, keepdims=True)
    cnt2 = jnp.sum(oh2f, axis=0, keepdims=True)
    chunks = jnp.floor((cnt1 + cnt2 + (BF16_ROWS - 1)) * (1.0 / BF16_ROWS))
    excl = (lax.broadcasted_iota(I32, (LANES, LANES), 0) < lax.broadcasted_iota(I32, (LANES, LANES), 1)).astype(BF16)
    chunk0 = _dg(jnp.broadcast_to(chunks, (BF16_ROWS, LANES)).astype(BF16), excl)[0:1, :]
    base = chunk0 * BF16_ROWS
    pos1 = jnp.sum(oh1f * (base + c1), axis=1, keepdims=True)
    pos2 = jnp.sum(oh2f * (base + cnt1 + c2), axis=1, keepdims=True)

    slot = lax.broadcasted_iota(I32, (tm, rows), 1)
    place = ((slot == pos1.astype(I32)) | (slot == pos2.astype(I32))).astype(BF16)
    xs_ref[0] = _dg(place, hn_hi, _TN).astype(BF16)

    rinfo_ref[...] = jnp.where(lane == 0, pos1, jnp.where(lane == 1, pos2, jnp.where(lane == 2, w1, jnp.where(lane == 3, w2, 0.0))))
    r8 = lax.broadcasted_iota(I32, (SUBLANES, LANES), 0)
    cinfo_ref[0] = jnp.where(r8 == 0, chunk0, jnp.where(r8 == 1, chunks, 0.0)).astype(I32)


def _route(x1, g, wr, br):
    t, d = x1.shape
    tm = MOE_TILE
    assert t % tm == 0
    nt = t // tm
    sorted_spec = pl.BlockSpec((1, MOE_SORTED_ROWS, d), lambda i: (i, 0, 0))
    sorted_shape = jax.ShapeDtypeStruct((nt, MOE_SORTED_ROWS, d), BF16)
    return pl.pallas_call(
        _route_kernel,
        grid=(nt,),
        in_specs=[pl.BlockSpec((tm, d), lambda i: (i, 0)), _resident((1, d)), _resident(wr.shape), _resident(br.shape)],
        out_specs=[sorted_spec, sorted_spec,
                   pl.BlockSpec((tm, LANES), lambda i: (i, 0)),
                   pl.BlockSpec((1, SUBLANES, LANES), lambda i: (i, 0, 0))],
        out_shape=[sorted_shape, sorted_shape,
                   jax.ShapeDtypeStruct((t, LANES), F32),
                   jax.ShapeDtypeStruct((nt, SUBLANES, LANES), I32)],
        compiler_params=_params(("parallel",)),
        name="route",
    )(x1, g, wr, br)


def _ffn_kernel(ids_ref, nvalid_ref, blk0_ref, xs_hbm, wg_hbm, wu_hbm, wd_hbm, ys_in_hbm, ys_hbm,
                lhs_buf, res_buf, wg_stage, wu_stage, wd_stage, wgu_bf, wd_bf, in_sem, out_sem, w_sem, *, layer):
    del ys_in_hbm
    nb = MOE_BLOCK_CHUNKS
    rows = BF16_ROWS
    nblocks = blk0_ref[N_EXPERTS]

    def chunk_copy_in(blk, slot, i):
        return pltpu.make_async_copy(xs_hbm.at[ids_ref[blk * nb + i]],
                                     lhs_buf.at[slot, pl.ds(i * rows, rows)], in_sem.at[slot])

    def chunk_copy_out(blk, slot, i):
        return pltpu.make_async_copy(res_buf.at[slot, pl.ds(i * rows, rows)],
                                     ys_hbm.at[ids_ref[blk * nb + i]], out_sem.at[slot])

    def gather_start(blk, slot):
        for i in range(nb):
            chunk_copy_in(blk, slot, i).start()

    def gather_wait(blk, slot):
        for i in range(nb):
            chunk_copy_in(blk, slot, i).wait()

    def scatter(blk, slot, start):
        n = nvalid_ref[blk]
        for i in range(nb):
            @pl.when(i < n)
            def _():
                cp = chunk_copy_out(blk, slot, i)
                cp.start() if start else cp.wait()

    def weight_copies(e, slot):
        return [pltpu.make_async_copy(src.at[layer, e], dst.at[slot], w_sem.at[slot])
                for src, dst in ((wg_hbm, wg_stage), (wu_hbm, wu_stage), (wd_hbm, wd_stage))]

    for cp in weight_copies(0, 0):
        cp.start()

    @pl.when(nblocks > 0)
    def _():
        gather_start(0, 0)

    def expert(e, carry):
        ws = e % 2
        for cp in weight_copies(e, ws):
            cp.wait()
        wgu_bf[:, :D_EXPERT] = wg_stage[ws].astype(BF16)
        wgu_bf[:, D_EXPERT:] = wu_stage[ws].astype(BF16)
        wd_bf[...] = wd_stage[ws].astype(BF16)

        @pl.when(e + 1 < N_EXPERTS)
        def _():
            for cp in weight_copies(e + 1, 1 - ws):
                cp.start()

        def block(blk, c):
            slot = blk % 2

            @pl.when(blk + 1 < nblocks)
            def _():
                gather_start(blk + 1, 1 - slot)

            gather_wait(blk, slot)

            @pl.when(blk >= 2)
            def _():
                scatter(blk - 2, slot, start=False)

            gu = _dg(lhs_buf[slot], wgu_bf[...])
            gate = gu[:, :D_EXPERT]
            hid = gate * jax.nn.sigmoid(gate) * gu[:, D_EXPERT:]
            res_buf[slot] = _dg(hid.astype(BF16), wd_bf[...]).astype(BF16)
            scatter(blk, slot, start=True)
            return c

        lax.fori_loop(blk0_ref[e], blk0_ref[e + 1], block, 0)
        return carry

    lax.fori_loop(0, N_EXPERTS, expert, 0)

    for back in (2, 1):
        @pl.when(nblocks >= back)
        def _():
            blk = nblocks - back
            scatter(blk, blk % 2, start=False)


def _ffn(xs, zeros, ids, nvalid, blk0, w_gate, w_up, w_down, layer):
    nt, srows, d = xs.shape
    chunked = (nt * srows // BF16_ROWS, BF16_ROWS, d)
    brows = MOE_BLOCK_CHUNKS * BF16_ROWS
    anywhere = pl.BlockSpec(memory_space=pl.ANY)
    f = w_gate.shape[-1]
    ys = pl.pallas_call(
        functools.partial(_ffn_kernel, layer=layer),
        grid_spec=pltpu.PrefetchScalarGridSpec(
            num_scalar_prefetch=3,
            grid=(1,),
            in_specs=[anywhere] * 5,
            out_specs=anywhere,
            scratch_shapes=[pltpu.VMEM((2, brows, d), BF16), pltpu.VMEM((2, brows, d), BF16),
                            pltpu.VMEM((2, d, f), F32), pltpu.VMEM((2, d, f), F32), pltpu.VMEM((2, f, d), F32),
                            pltpu.VMEM((d, 2 * f), BF16), pltpu.VMEM((f, d), BF16),
                            pltpu.SemaphoreType.DMA((2,)), pltpu.SemaphoreType.DMA((2,)),
                            pltpu.SemaphoreType.DMA((2,))],
        ),
        out_shape=jax.ShapeDtypeStruct(chunked, BF16),
        input_output_aliases={7: 0},
        compiler_params=_params(("arbitrary",)),
        name="expert_ffn",
    )(ids, nvalid, blk0, xs.reshape(chunked), w_gate, w_up, w_down, zeros.reshape(chunked))
    return ys.reshape(nt, srows, d)


def _block_tables(cinfo):
    nt = cinfo.shape[0]
    nb = MOE_BLOCK_CHUNKS
    nchunks = nt * MOE_CHUNKS
    chunk0 = cinfo[:, 0, :N_EXPERTS]
    chunks = cinfo[:, 1, :N_EXPERTS]
    ends = chunk0 + chunks
    c = jnp.arange(MOE_CHUNKS, dtype=I32)
    label = jnp.sum((c[None, :, None] >= ends[:, None, :]).astype(I32), axis=-1)
    order = jnp.argsort(label.reshape(-1), stable=True).astype(I32)
    cnt = chunks.sum(axis=0).astype(I32)
    cstart = jnp.cumsum(cnt) - cnt
    nblk = (cnt + nb - 1) // nb
    blk0 = jnp.concatenate([jnp.zeros((1,), I32), jnp.cumsum(nblk).astype(I32)])
    max_blocks = nchunks // nb + N_EXPERTS
    j = jnp.arange(max_blocks, dtype=I32)
    ej = jnp.minimum(jnp.sum((j[:, None] >= blk0[None, 1:]).astype(I32), axis=1), N_EXPERTS - 1)
    first = cstart[ej] + (j - blk0[ej]) * nb
    left = cnt[ej] - (j - blk0[ej]) * nb
    i = jnp.arange(nb, dtype=I32)
    pos = jnp.where(i[None, :] < left[:, None], first[:, None] + i[None, :], first[:, None])
    ids = order[jnp.clip(pos, 0, nchunks - 1)]
    return ids.reshape(-1), jnp.clip(left, 0, nb).astype(I32), blk0


def _ple_kernel(x1_ref, ys_ref, rinfo_ref, pp_ref, ps_ref, g_ref, wg_ref, wp_ref, gfin_ref, outp_ref, outs_ref, *,
                final, prompt_tiles):
    tm = MOE_TILE
    rows = MOE_SORTED_ROWS
    i = pl.program_id(0)
    rinfo = rinfo_ref[...]
    pos1 = rinfo[:, 0:1].astype(I32)
    pos2 = rinfo[:, 1:2].astype(I32)
    w1 = rinfo[:, 2:3]
    w2 = rinfo[:, 3:4]
    slot = lax.broadcasted_iota(I32, (tm, rows), 1)
    ys = ys_ref[0]
    y1 = _dg((slot == pos1).astype(BF16), ys)
    y2 = _dg((slot == pos2).astype(BF16), ys)
    x = x1_ref[...] + (w1 * y1 + w2 * y2)
    gate = jax.nn.sigmoid(_dg(_rmsnorm(x, g_ref[...]).astype(BF16), wg_ref[...]))
    p = _pair_load(i, prompt_tiles, pp_ref, ps_ref)
    x = x + gate * _dg(p.astype(BF16), wp_ref[...])
    if final:
        x = _rmsnorm(x, gfin_ref[...])

    @pl.when(i < prompt_tiles)
    def _():
        outp_ref[...] = x

    @pl.when(i >= prompt_tiles)
    def _():
        outs_ref[...] = x


def _ple(x1, ys, rinfo, p_prompt, p_sample, layer, g, wg, wp, gfin, final):
    t, d = x1.shape
    tm = MOE_TILE
    nt = t // tm
    tp, ts = p_prompt.shape[1], p_sample.shape[1]
    assert tp % tm == 0 and ts % tm == 0 and tp + ts == t
    pt = tp // tm
    pd = p_prompt.shape[2]
    ys = ys.reshape(-1, MOE_SORTED_ROWS, d)
    return pl.pallas_call(
        functools.partial(_ple_kernel, final=final, prompt_tiles=pt),
        grid=(nt,),
        in_specs=[pl.BlockSpec((tm, d), lambda i: (i, 0)),
                  pl.BlockSpec((1, MOE_SORTED_ROWS, d), lambda i: (i, 0, 0)),
                  pl.BlockSpec((tm, LANES), lambda i: (i, 0)),
                  pl.BlockSpec((None, tm, pd), lambda i: (layer, jnp.minimum(i, pt - 1), 0)),
                  pl.BlockSpec((None, tm, pd), lambda i: (layer, jnp.maximum(i - pt, 0), 0)),
                  _resident((1, d)), _resident(wg.shape), _resident(wp.shape), _resident((1, d))],
        out_specs=_pair_specs(tm, d, pt),
        out_shape=[jax.ShapeDtypeStruct((tp, d), F32), jax.ShapeDtypeStruct((ts, d), F32)],
        compiler_params=_params(("arbitrary",)),
        name="ple",
    )(x1, ys, rinfo, p_prompt, p_sample, g, wg, wp, gfin)


def _lower_bounds(lb_param):
    s = jax.nn.softmax(lb_param.astype(F32), axis=0)
    c = jnp.cumsum(s, axis=0)
    return c - c[0:1]


def _split_kernel(w_ref, hi_ref, lo_ref):
    hi_ref[...], lo_ref[...] = _split2(w_ref[...])


def _hi_lo(w):
    k, n = w.shape
    tk = ROW_TILE
    assert k % tk == 0
    spec = pl.BlockSpec((tk, n), lambda i: (i, 0))
    return pl.pallas_call(
        _split_kernel,
        grid=(k // tk,),
        in_specs=[spec],
        out_specs=[spec, spec],
        out_shape=[jax.ShapeDtypeStruct((k, n), BF16)] * 2,
        compiler_params=_params(("parallel",)),
        name="split_weight",
    )(w)


def kernel(x_prompt, x_sample, state_hgrn, state_conv, p_prompt, p_sample, g_mix, w_in, hg_lower, g_hg_out, w_br_a, w_conv, w_br_b, w_out, g_ffn, w_router_group, b_router_group, w_router_expert, b_router_expert, w_gate, w_up, w_down, g_ple, w_ple_gate, w_ple_proj, g_final):
    depth = w_in.shape[0]
    bp, seq, d = x_prompt.shape
    bs, steps, _ = x_sample.shape
    assert d == D_MODEL and w_conv.shape[1] == CONV_W
    assert TAIL_ROWS % ROW_TILE == 0 and seq >= TAIL_ROWS
    tp = bp * seq
    ts = bs * steps
    hw = N_HEADS * HEAD_DIM

    xp, xs_rows = x_prompt.reshape(tp, d), x_sample.reshape(ts, d)
    pp = p_prompt.reshape(depth, tp, -1)
    ps = p_sample.reshape(depth, ts, -1)
    lbs = _lower_bounds(hg_lower)
    row = lambda a: a.reshape(1, -1)

    hg_p, cv_p, cv_s = [], [], []
    hg_s = None
    for li in range(depth):
        careful = li < depth - 1
        tail = (seq // ROW_TILE, TAIL_ROWS // ROW_TILE, tp // ROW_TILE) if careful else None
        act = F32 if careful else BF16
        wi = w_in[li]
        w_scan = jnp.concatenate([wi[:, :4 * hw], wi[:, 4 * hw + d:4 * hw + 3 * d]], axis=1)
        w_cbg = jnp.concatenate([wi[:, 4 * hw:4 * hw + d], wi[:, 4 * hw + 3 * d:]], axis=1)
        mix_w = (w_cbg, w_br_a[li], w_br_b[li], w_out[li])
        if careful:
            w_scan, w_scan_lo = _hi_lo(w_scan)
            mix_w, mix_w_lo = zip(*[_hi_lo(w) for w in mix_w])
        else:
            w_scan, w_scan_lo = w_scan.astype(BF16), None
            mix_w, mix_w_lo = [w.astype(BF16) for w in mix_w], None

        qs, lf, kf, v, sog, u = _inproj(xp, xs_rows, row(g_mix[li]), row(lbs[li]), w_scan, w_scan_lo, tail, act)

        gh = row(g_hg_out[li])
        ogp, s_p = _scan_prompt(qs, lf, kf, v, sog, gh, bp, seq, TAIL_ROWS if careful else None)
        ogs, hg_s = _scan_sample(qs, lf, kf, v, sog, gh, state_hgrn, li, hg_s, tp, steps, careful)
        hg_p.append(s_p)
        keep = CONV_W - 1
        cv_p.append(jnp.stack([lax.slice_in_dim(u, (b + 1) * seq - keep, (b + 1) * seq) for b in range(bp)]))
        cv_s.append(jnp.concatenate([state_conv[li], lax.slice_in_dim(u, tp, tp + ts).reshape(bs, steps, d)],
                                    axis=1)[:, steps:])

        buf = state_conv[li]
        zero = jnp.zeros((bs, steps - 1, d), F32)
        e1 = jnp.concatenate([buf[:, 1:2], zero], axis=1).reshape(ts, d)
        e2 = jnp.concatenate([buf, zero[:, 1:]], axis=1).reshape(ts, d)
        x1 = _mix(xp, xs_rows, ogp, ogs, u, e1, e2, row(g_mix[li]), w_conv[li], mix_w, mix_w_lo, seq, steps, tail)

        wr = jnp.zeros((d, LANES), F32).at[:, :N_EXPERTS].set(w_router_expert[li])
        wr = wr.at[:, GROUP_LANE0:GROUP_LANE0 + N_GROUPS].set(w_router_group[li])
        br = jnp.zeros((1, LANES), F32).at[0, :N_EXPERTS].set(b_router_expert[li])
        br = br.at[0, GROUP_LANE0:GROUP_LANE0 + N_GROUPS].set(b_router_group[li])
        xsort, zeros, rinfo, cinfo = _route(x1, row(g_ffn[li]), wr, br)
        ys = _ffn(xsort, zeros, *_block_tables(cinfo), w_gate, w_up, w_down, li)

        xp, xs_rows = _ple(x1, ys, rinfo, pp, ps, li, row(g_ple[li]), w_ple_gate[li].astype(BF16),
                           w_ple_proj[li].astype(BF16), row(g_final), li == depth - 1)

    return (xp.reshape(bp, seq, d), xs_rows.reshape(bs, steps, d), jnp.stack(hg_p), jnp.stack(cv_p), hg_s,
            jnp.stack(cv_s))
```

```python
import functools

import jax
import jax.numpy as jnp
from jax import lax
from jax.experimental import pallas as pl
from jax.experimental.pallas import tpu as pltpu

F32 = jnp.float32
BF16 = jnp.bfloat16
I32 = jnp.int32

D_MODEL = 1024
N_HEADS = 8
HEAD_DIM = 128
CONV_W = 3
N_GROUPS = 4
EXPERTS_PER_GROUP = 8
N_EXPERTS = N_GROUPS * EXPERTS_PER_GROUP
D_EXPERT = 256
EPS = 1e-6

LANES = 128
SUBLANES = 8
BF16_ROWS = 16
VMEM_LIMIT = 56 * 1024 * 1024

ROW_TILE = 256
FAST_ROW_TILE = 512
SCAN_CHUNK = 128
SCAN_STEP_CHUNKS = 2
SCAN_HEAD_UNROLL = 2
TAIL_ROWS = 256
SAMPLE_SEQS = 8
MOE_TILE = 256
MOE_SORTED_ROWS = 1024
MOE_CHUNKS = MOE_SORTED_ROWS // BF16_ROWS
MOE_BLOCK_CHUNKS = 32
GROUP_LANE0 = N_EXPERTS

_NN = (((1,), (0,)), ((), ()))
_NT = (((1,), (1,)), ((), ()))
_TN = (((0,), (0,)), ((), ()))


def _params(sem):
    return pltpu.CompilerParams(dimension_semantics=sem, vmem_limit_bytes=VMEM_LIMIT)


def _resident(shape):
    nd = len(shape)
    return pl.BlockSpec(shape, lambda *_: (0,) * nd, pipeline_mode=pl.Buffered(1))


def _rmsnorm(x, g):
    r = lax.rsqrt(jnp.mean(x * x, axis=-1, keepdims=True) + EPS)
    return (x * r) * g


def _dg(a, b, dims=_NN):
    return lax.dot_general(a, b, dims, preferred_element_type=F32)


def _split2(x):
    hi = x.astype(BF16)
    return hi, (x - hi.astype(F32)).astype(BF16)


def _mm(a, b, precise, dims=_NN):
    if not precise:
        return _dg(a.astype(BF16), b.astype(BF16), dims)
    ah, al = _split2(a.astype(F32))
    bh, bl = _split2(b.astype(F32))
    return _dg(ah, bh, dims) + (_dg(al, bh, dims) + _dg(ah, bl, dims))


def _mmw(a, w_hi, w_lo, precise):
    if not precise:
        return _dg(a.astype(BF16), w_hi)
    ah, al = _split2(a.astype(F32))
    return _dg(ah, w_hi) + (_dg(al, w_hi) + _dg(ah, w_lo))


def _split3(x):
    hi = x.astype(BF16)
    r1 = x - hi.astype(F32)
    mid = r1.astype(BF16)
    lo = (r1 - mid.astype(F32)).astype(BF16)
    return hi, mid, lo


def _dot_exact_lhs(m01, x):
    hi, mid, lo = _split3(x)
    return _dg(m01, hi) + _dg(m01, mid) + _dg(m01, lo)


def _either(precise_pred, body):
    if precise_pred is None:
        body(False)
    else:
        pl.when(precise_pred)(lambda: body(True))
        pl.when(jnp.logical_not(precise_pred))(lambda: body(False))


def _tail_tile(i, tiles_per_seq, tail_tiles, prompt_tiles):
    return (i >= prompt_tiles) | ((i % tiles_per_seq) >= tiles_per_seq - tail_tiles)


def _pair_specs(tm, cols, prompt_tiles):
    return [pl.BlockSpec((tm, cols), lambda i: (jnp.minimum(i, prompt_tiles - 1), 0)),
            pl.BlockSpec((tm, cols), lambda i: (jnp.maximum(i - prompt_tiles, 0), 0))]


def _pair_load(i, prompt_tiles, p_ref, s_ref):
    return jnp.where(i < prompt_tiles, p_ref[...], s_ref[...])


def _w_slice(ref, sl):
    return None if ref is None else ref[:, sl]


def _inproj_kernel(*refs, tail, prompt_tiles):
    if tail is None:
        xp_ref, xs_ref, g_ref, lb_ref, w_ref, qs_ref, lf_ref, kf_ref, v_ref, sog_ref, u_ref = refs
        wl_ref = None
    else:
        xp_ref, xs_ref, g_ref, lb_ref, w_ref, wl_ref, qs_ref, lf_ref, kf_ref, v_ref, sog_ref, u_ref = refs
    d = D_MODEL
    i = pl.program_id(0)

    def body(precise):
        h = _rmsnorm(_pair_load(i, prompt_tiles, xp_ref, xs_ref), g_ref[...])

        def seg(j):
            sl = slice(j * d, (j + 1) * d)
            return _mmw(h, w_ref[:, sl], _w_slice(wl_ref, sl), precise)

        q = seg(0)
        qs_ref[...] = q * jax.nn.sigmoid(q)
        fp = seg(1)
        lb = lb_ref[...]
        log_sig = jnp.minimum(fp, 0.0) - jnp.log1p(jnp.exp(-jnp.abs(fp)))
        a = jnp.log(lb)
        b = jnp.log1p(-lb) + log_sig
        lf_ref[...] = jnp.maximum(a, b) + jnp.log1p(jnp.exp(-jnp.abs(a - b)))
        kf_ref[...] = (1.0 - lb) * jax.nn.sigmoid(-fp)
        v_ref[...] = seg(2).astype(v_ref.dtype)
        og = seg(3)
        sog_ref[...] = (og * jax.nn.sigmoid(og)).astype(sog_ref.dtype)
        u_ref[...] = seg(4) * seg(5)

    _either(None if tail is None else _tail_tile(i, *tail), body)


def _inproj(xp, xs, g, lb, w, w_lo, tail, act_dtype, tm):
    d = D_MODEL
    assert xp.shape[0] % tm == 0 and xs.shape[0] % tm == 0 and xp.shape[1] == d
    pt = xp.shape[0] // tm
    t = xp.shape[0] + xs.shape[0]
    row = pl.BlockSpec((tm, d), lambda i: (i, 0))
    ws = [w] if tail is None else [w, w_lo]
    return pl.pallas_call(
        functools.partial(_inproj_kernel, tail=tail, prompt_tiles=pt),
        grid=(t // tm,),
        in_specs=_pair_specs(tm, d, pt) + [_resident((1, d)), _resident((1, d))] + [_resident(a.shape) for a in ws],
        out_specs=[row] * 6,
        out_shape=[jax.ShapeDtypeStruct((t, d), dt) for dt in (F32, F32, F32, act_dtype, act_dtype, F32)],
        compiler_params=_params(("parallel",)),
        name="inproj",
    )(xp, xs, g, lb, *ws)


def _score_masks(n_rows, block):
    xr = (lax.broadcasted_iota(I32, (n_rows, n_rows), 0)
          ^ lax.broadcasted_iota(I32, (n_rows, n_rows), 1))
    masks = {0: (xr == 0).astype(F32)}
    b = 1
    while 2 * b < n_rows and b < block:
        masks[b] = (xr < 2 * b).astype(F32)
        b *= 2
    return masks


def _intra_scores(q, k, g, cum, n_rows, block, precise, masks):
    ri = lax.broadcasted_iota(I32, (n_rows, LANES), 0)
    s = masks[0] * jnp.sum(q * k, axis=1, keepdims=True)
    b = 1
    while b < block:
        if b >= SUBLANES:
            shape3 = (n_rows // (2 * b), 2 * b, LANES)
            c3, q3, k3 = cum.reshape(shape3), q.reshape(shape3), k.reshape(shape3)
            mid = c3[:, b - 1:b, :]
            zero = jnp.zeros((shape3[0], b, LANES), F32)
            qt = jnp.concatenate([zero, q3[:, b:, :] * jnp.exp(c3[:, b:, :] - mid)], axis=1).reshape(n_rows, LANES)
            kt = jnp.concatenate([k3[:, :b, :] * jnp.exp(mid - c3[:, :b, :]), zero], axis=1).reshape(n_rows, LANES)
        else:
            upper = (ri & b) != 0
            if b == 1:
                z = jnp.where(upper, g, 0.0)
            elif b == 2:
                g_prev = pltpu.roll(g, 1, 0)
                g_next = pltpu.roll(g, n_rows - 1, 0)
                m4 = ri & 3
                z = jnp.where(m4 == 2, g, jnp.where(m4 == 3, g + g_prev, jnp.where(m4 == 0, g_next, 0.0)))
            else:
                c3 = cum.reshape(n_rows // (2 * b), 2 * b, LANES)
                mid = jnp.broadcast_to(c3[:, b - 1:b, :], c3.shape).reshape(n_rows, LANES)
                z = jnp.where(upper, cum - mid, mid - cum)
            e = jnp.exp(z)
            qt = jnp.where(upper, q * e, 0.0)
            kt = jnp.where(upper, 0.0, k * e)
        prod = _mm(qt, kt, precise, _NT)
        s = s + (prod * masks[b] if b in masks else prod)
        b *= 2
    return s


def _decay_columns(e_row):
    hi = e_row.astype(BF16).astype(F32)
    lo = e_row - hi
    r = lax.broadcasted_iota(I32, (BF16_ROWS, LANES), 0)
    stacked = jnp.where(r == 0, hi, jnp.where(r == 1, lo, 0.0)).astype(BF16)
    return _dg(stacked, jnp.ones((BF16_ROWS, LANES), BF16), _TN)


def _head_out(o, gh, sog, dtype):
    on = o * lax.rsqrt(jnp.mean(o * o, axis=-1, keepdims=True) + EPS) * gh
    return (on * sog.astype(F32)).astype(dtype)


def _scan_prompt_kernel(qs_ref, lf_ref, kf_ref, v_ref, sog_ref, gh_ref, og_ref, sfin_ref, s_scr, *, tail_steps):
    c = pl.program_id(1)
    nc = pl.num_programs(1)
    n = SCAN_CHUNK

    @pl.when(c == 0)
    def _():
        s_scr[...] = jnp.zeros_like(s_scr)

    def body(precise):
        tri = (lax.broadcasted_iota(I32, (n, n), 0) >= lax.broadcasted_iota(I32, (n, n), 1)).astype(BF16)
        masks = _score_masks(n, n)

        def head(h, carry):
            hs = pl.ds(pl.multiple_of(h * HEAD_DIM, HEAD_DIM), HEAD_DIM)
            s_old = s_scr[h]
            for j in range(SCAN_STEP_CHUNKS):
                rs = slice(j * n, (j + 1) * n)
                q, g, k, v = qs_ref[rs, hs], lf_ref[rs, hs], kf_ref[rs, hs], v_ref[rs, hs]
                cum = _dot_exact_lhs(tri, g)
                tot = cum[n - 1:n, :]
                o = _mm(q * jnp.exp(cum), s_old, precise)
                o = o + _mm(_intra_scores(q, k, g, cum, n, n, precise, masks), v, precise)
                ke = k * jnp.exp(tot - cum)
                s_old = _decay_columns(jnp.exp(tot)) * s_old + _mm(ke, v, precise, _TN)
                og_ref[rs, hs] = _head_out(o, gh_ref[:, hs], sog_ref[rs, hs], og_ref.dtype)
            s_scr[h] = s_old
            return carry

        lax.fori_loop(0, N_HEADS, head, 0, unroll=1 if precise else SCAN_HEAD_UNROLL)

    _either(None if tail_steps is None else c >= nc - tail_steps, body)

    @pl.when(c == nc - 1)
    def _():
        sfin_ref[0] = s_scr[...]


def _scan_prompt(qs, lf, kf, v, sog, gh, batch, seq, tail_rows):
    t, d = qs.shape
    n = SCAN_CHUNK * SCAN_STEP_CHUNKS
    assert seq % n == 0 and (tail_rows is None or tail_rows % n == 0)
    nc = seq // n
    row = pl.BlockSpec((n, d), lambda b, c: (b * nc + c, 0))
    return pl.pallas_call(
        functools.partial(_scan_prompt_kernel, tail_steps=None if tail_rows is None else tail_rows // n),
        grid=(batch, nc),
        in_specs=[row, row, row, row, row, _resident((1, d))],
        out_specs=[row, pl.BlockSpec((1, N_HEADS, HEAD_DIM, HEAD_DIM), lambda b, c: (b, 0, 0, 0))],
        out_shape=[jax.ShapeDtypeStruct((batch * seq, d), v.dtype),
                   jax.ShapeDtypeStruct((batch, N_HEADS, HEAD_DIM, HEAD_DIM), F32)],
        scratch_shapes=[pltpu.VMEM((N_HEADS, HEAD_DIM, HEAD_DIM), F32)],
        compiler_params=_params(("parallel", "arbitrary")),
        name="scan_prompt",
    )(qs, lf, kf, v, sog, gh)


def _scan_sample_kernel(*refs, steps, precise, layer, chained):
    if chained:
        qs_ref, lf_ref, kf_ref, v_ref, sog_ref, gh_ref, s_ref, _, og_ref, snew_ref = refs
    else:
        qs_ref, lf_ref, kf_ref, v_ref, sog_ref, gh_ref, s_ref, og_ref, all_ref = refs
        for other in range(all_ref.shape[0]):
            if other != layer:
                all_ref[other] = jnp.zeros(all_ref.shape[1:], F32)
        snew_ref = all_ref.at[layer]
    n = SAMPLE_SEQS * steps
    shift = steps.bit_length() - 1
    ri = lax.broadcasted_iota(I32, (n, n), 0)
    ci = lax.broadcasted_iota(I32, (n, n), 1)
    same = (ri >> shift) == (ci >> shift)
    tri = (same & (ri >= ci)).astype(BF16)
    tri_after = (same & (ci > ri)).astype(BF16)
    seq_of_row = lax.broadcasted_iota(I32, (n, LANES), 0) >> shift
    masks = _score_masks(n, steps)

    def head(h, carry):
        hs = pl.ds(pl.multiple_of(h * HEAD_DIM, HEAD_DIM), HEAD_DIM)
        q, g, k, v = qs_ref[:, hs], lf_ref[:, hs], kf_ref[:, hs], v_ref[:, hs]
        cum = _dot_exact_lhs(tri, g)
        after = _dot_exact_lhs(tri_after, g)
        o = _mm(_intra_scores(q, k, g, cum, n, steps, precise, masks), v, precise)
        qe = q * jnp.exp(cum)
        ke = k * jnp.exp(after)
        for s in range(SAMPLE_SEQS):
            mine = seq_of_row == s
            s_old = s_ref[s, h]
            o = o + jnp.where(mine, _mm(qe, s_old, precise), 0.0)
            kv = _mm(jnp.where(mine, ke, 0.0), v, precise, _TN)
            last = (s + 1) * steps - 1
            snew_ref[s, h] = _decay_columns(jnp.exp(cum[last:last + 1, :])) * s_old + kv
        og_ref[:, hs] = _head_out(o, gh_ref[:, hs], sog_ref[:, hs], og_ref.dtype)
        return carry

    lax.fori_loop(0, N_HEADS, head, 0)


def _scan_sample(qs, lf, kf, v, sog, gh, states, layer, new_states, row0, steps, precise):
    t, d = qs.shape
    depth, nseq = states.shape[:2]
    assert steps & (steps - 1) == 0 and steps >= 4, "sample block must be a power of two >= 4"
    n = SAMPLE_SEQS * steps
    assert nseq % SAMPLE_SEQS == 0 and row0 % n == 0 and n % BF16_ROWS == 0
    b0 = row0 // n
    chained = new_states is not None
    row = pl.BlockSpec((n, d), lambda i: (b0 + i, 0))
    tail = (SAMPLE_SEQS, N_HEADS, HEAD_DIM, HEAD_DIM)
    st = pl.BlockSpec((None,) + tail, lambda i: (layer, i, 0, 0, 0))
    st_all = pl.BlockSpec((depth,) + tail, lambda i: (0, i, 0, 0, 0))
    extra = [new_states] if chained else []
    return pl.pallas_call(
        functools.partial(_scan_sample_kernel, steps=steps, precise=precise, layer=layer, chained=chained),
        grid=(nseq // SAMPLE_SEQS,),
        in_specs=[row, row, row, row, row, _resident((1, d)), st] + [pl.BlockSpec(memory_space=pl.ANY)] * len(extra),
        out_specs=[pl.BlockSpec((n, d), lambda i: (i, 0)), st if chained else st_all],
        out_shape=[jax.ShapeDtypeStruct((nseq * steps, d), v.dtype), jax.ShapeDtypeStruct(states.shape, F32)],
        input_output_aliases={7: 1} if chained else {},
        compiler_params=_params(("parallel",)),
        name="scan_sample",
    )(qs, lf, kf, v, sog, gh, states, *extra)


def _mix_kernel(*refs, tm, tiles_per_seq, prompt_tiles, steps, tail):
    if tail is None:
        (xp_ref, xs_ref, ogp_ref, ogs_ref, u_ref, halo_ref, e1_ref, e2_ref, g_ref, wc_ref,
         w_cbg_ref, wa_ref, wb_ref, wo_ref, x1_ref) = refs
        l_cbg_ref = la_ref = lb_ref = lo_ref = None
    else:
        (xp_ref, xs_ref, ogp_ref, ogs_ref, u_ref, halo_ref, e1_ref, e2_ref, g_ref, wc_ref,
         w_cbg_ref, wa_ref, wb_ref, wo_ref, l_cbg_ref, la_ref, lb_ref, lo_ref, x1_ref) = refs
    d = D_MODEL
    i = pl.program_id(0)

    def full(ref):
        return None if ref is None else ref[...]

    def body(precise):
        x = _pair_load(i, prompt_tiles, xp_ref, xs_ref)
        h = _rmsnorm(x, g_ref[...])
        u = u_ref[...]
        ri = lax.broadcasted_iota(I32, (tm, d), 0)
        r1 = pltpu.roll(u, 1, 0)
        r2 = pltpu.roll(u, 2, 0)
        keep = jnp.where(i % tiles_per_seq == 0, 0.0, 1.0)
        halo = halo_ref[...]
        h_last = halo[7:8, :] * keep
        h_prev = halo[6:7, :] * keep
        is_sample = i >= prompt_tiles
        t_in_seq = ri & (steps - 1)
        prev1 = jnp.where(is_sample,
                          jnp.where(t_in_seq == 0, e1_ref[...], r1),
                          jnp.where(ri == 0, h_last, r1))
        prev2 = jnp.where(is_sample,
                          jnp.where(t_in_seq < 2, e2_ref[...], r2),
                          jnp.where(ri == 0, h_prev, jnp.where(ri == 1, h_last, r2)))
        wc = wc_ref[...]
        conv = prev2 * wc[0:1, :] + prev1 * wc[1:2, :] + u * wc[2:3, :]

        def seg(j):
            sl = slice(j * d, (j + 1) * d)
            return _mmw(h, w_cbg_ref[:, sl], _w_slice(l_cbg_ref, sl), precise)

        y_b = _mmw(seg(0) * conv, wb_ref[...], full(lb_ref), precise)
        y_a = _mmw(_pair_load(i, prompt_tiles, ogp_ref, ogs_ref), wa_ref[...], full(la_ref), precise)
        m = jax.nn.sigmoid(seg(1)) * y_a + jax.nn.sigmoid(seg(2)) * y_b
        x1_ref[...] = x + _mmw(m, wo_ref[...], full(lo_ref), precise)

    _either(None if tail is None else _tail_tile(i, *tail), body)


def _mix(xp, xs, ogp, ogs, u, e1, e2, g, wc, ws, ws_lo, seq, steps, tail, tm):
    t, d = u.shape
    prompt_rows = xp.shape[0]
    assert seq % tm == 0 and prompt_rows % tm == 0 and (t - prompt_rows) % tm == 0 and tm % steps == 0
    pt = prompt_rows // tm
    row = pl.BlockSpec((tm, d), lambda i: (i, 0))
    halo = pl.BlockSpec((SUBLANES, d), lambda i: (jnp.maximum(i * (tm // SUBLANES) - 1, 0), 0))
    erow = pl.BlockSpec((tm, d), lambda i: (jnp.maximum(i - pt, 0), 0))
    weights = list(ws) + ([] if tail is None else list(ws_lo))
    return pl.pallas_call(
        functools.partial(_mix_kernel, tm=tm, tiles_per_seq=seq // tm, prompt_tiles=pt, steps=steps, tail=tail),
        grid=(t // tm,),
        in_specs=_pair_specs(tm, d, pt) + _pair_specs(tm, d, pt)
        + [row, halo, erow, erow, _resident((1, d)), _resident(wc.shape)]
        + [_resident(a.shape) for a in weights],
        out_specs=row,
        out_shape=jax.ShapeDtypeStruct((t, d), F32),
        compiler_params=_params(("parallel",)),
        name="mix",
    )(xp, xs, ogp, ogs, u, u, e1, e2, g, wc, *weights)


def _route_kernel(x_ref, g_ref, wr_ref, br_ref, xs_ref, zeros_ref, rinfo_ref, cinfo_ref):
    tm = MOE_TILE
    rows = MOE_SORTED_ROWS
    zeros_ref[...] = jnp.zeros_like(zeros_ref)
    hn = _rmsnorm(x_ref[...], g_ref[...])
    hn_hi, hn_lo = _split2(hn)
    wr_hi, wr_lo = _split2(wr_ref[...])
    logits = _dg(hn_hi, wr_hi) + _dg(hn_lo, wr_hi) + _dg(hn_hi, wr_lo) + br_ref[...]

    lane = lax.broadcasted_iota(I32, (tm, LANES), 1)
    neg = -jnp.inf
    big = jnp.int32(LANES)

    def first_lane(mask):
        return jnp.min(jnp.where(mask, lane, big), axis=1, keepdims=True)

    is_group = (lane >= GROUP_LANE0) & (lane < GROUP_LANE0 + N_GROUPS)
    gl = jnp.where(is_group, logits, neg)
    gmax = jnp.max(gl, axis=1, keepdims=True)
    g_idx = first_lane(gl == gmax) - GROUP_LANE0
    g_w = 1.0 / jnp.sum(jnp.exp(gl - gmax), axis=1, keepdims=True)

    in_group = (lane < N_EXPERTS) & ((lane >> (EXPERTS_PER_GROUP.bit_length() - 1)) == g_idx)
    el = jnp.where(in_group, logits, neg)
    emax = jnp.max(el, axis=1, keepdims=True)
    ee = jnp.exp(el - emax)
    prob = ee / jnp.sum(ee, axis=1, keepdims=True)
    prob = jnp.where(in_group, prob, -1.0)
    p1 = jnp.max(prob, axis=1, keepdims=True)
    i1 = first_lane(prob == p1)
    prob2 = jnp.where(lane == i1, -1.0, prob)
    p2 = jnp.max(prob2, axis=1, keepdims=True)
    i2 = first_lane(prob2 == p2)
    psum = p1 + p2
    w1 = p1 / psum * g_w
    w2 = p2 / psum * g_w

    oh1 = (lane == i1).astype(BF16)
    oh2 = (lane == i2).astype(BF16)
    before = (lax.broadcasted_iota(I32, (tm, tm), 1) < lax.broadcasted_iota(I32, (tm, tm), 0)).astype(BF16)
    c1 = _dg(before, oh1)
    c2 = _dg(before, oh2)
    oh1f = oh1.astype(F32)
    oh2f = oh2.astype(F32)
    cnt1 = jnp.sum(oh1f, axis=0, keepdims=True)
    cnt2 = jnp.sum(oh2f, axis=0, keepdims=True)
    chunks = jnp.floor((cnt1 + cnt2 + (BF16_ROWS - 1)) * (1.0 / BF16_ROWS))
    excl = (lax.broadcasted_iota(I32, (LANES, LANES), 0) < lax.broadcasted_iota(I32, (LANES, LANES), 1)).astype(BF16)
    chunk0 = _dg(jnp.broadcast_to(chunks, (BF16_ROWS, LANES)).astype(BF16), excl)[0:1, :]
    base = chunk0 * BF16_ROWS
    pos1 = jnp.sum(oh1f * (base + c1), axis=1, keepdims=True)
    pos2 = jnp.sum(oh2f * (base + cnt1 + c2), axis=1, keepdims=True)

    slot = lax.broadcasted_iota(I32, (tm, rows), 1)
    place = ((slot == pos1.astype(I32)) | (slot == pos2.astype(I32))).astype(BF16)
    xs_ref[0] = _dg(place, hn_hi, _TN).astype(BF16)

    rinfo_ref[...] = jnp.where(lane == 0, pos1, jnp.where(lane == 1, pos2, jnp.where(lane == 2, w1, jnp.where(lane == 3, w2, 0.0))))
    r8 = lax.broadcasted_iota(I32, (SUBLANES, LANES), 0)
    cinfo_ref[0] = jnp.where(r8 == 0, chunk0, jnp.where(r8 == 1, chunks, 0.0)).astype(I32)


def _route(x1, g, wr, br):
    t, d = x1.shape
    tm = MOE_TILE
    assert t % tm == 0
    nt = t // tm
    sorted_spec = pl.BlockSpec((1, MOE_SORTED_ROWS, d), lambda i: (i, 0, 0))
    sorted_shape = jax.ShapeDtypeStruct((nt, MOE_SORTED_ROWS, d), BF16)
    return pl.pallas_call(
        _route_kernel,
        grid=(nt,),
        in_specs=[pl.BlockSpec((tm, d), lambda i: (i, 0)), _resident((1, d)), _resident(wr.shape), _resident(br.shape)],
        out_specs=[sorted_spec, sorted_spec,
                   pl.BlockSpec((tm, LANES), lambda i: (i, 0)),
                   pl.BlockSpec((1, SUBLANES, LANES), lambda i: (i, 0, 0))],
        out_shape=[sorted_shape, sorted_shape,
                   jax.ShapeDtypeStruct((t, LANES), F32),
                   jax.ShapeDtypeStruct((nt, SUBLANES, LANES), I32)],
        compiler_params=_params(("parallel",)),
        name="route",
    )(x1, g, wr, br)


def _ffn_kernel(order_ref, first_ref, left_ref, blk0_ref, xs_hbm, wg_hbm, wu_hbm, wd_hbm, ys_in_hbm, ys_hbm,
                lhs_buf, res_buf, wg_stage, wu_stage, wd_stage, wgu_bf, wd_bf, in_sem, out_sem, w_sem, *, layer):
    del ys_in_hbm
    nb = MOE_BLOCK_CHUNKS
    rows = BF16_ROWS
    nblocks = blk0_ref[N_EXPERTS]

    def chunk_id(blk, i):
        return order_ref[first_ref[blk] + jnp.where(i < left_ref[blk], i, 0)]

    def chunk_copy_in(blk, slot, i):
        return pltpu.make_async_copy(xs_hbm.at[chunk_id(blk, i)],
                                     lhs_buf.at[slot, pl.ds(i * rows, rows)], in_sem.at[slot])

    def chunk_copy_out(blk, slot, i):
        return pltpu.make_async_copy(res_buf.at[slot, pl.ds(i * rows, rows)],
                                     ys_hbm.at[chunk_id(blk, i)], out_sem.at[slot])

    def gather_start(blk, slot):
        for i in range(nb):
            chunk_copy_in(blk, slot, i).start()

    def gather_wait(blk, slot):
        for i in range(nb):
            chunk_copy_in(blk, slot, i).wait()

    def scatter(blk, slot, start):
        n = left_ref[blk]
        for i in range(nb):
            @pl.when(i < n)
            def _():
                cp = chunk_copy_out(blk, slot, i)
                cp.start() if start else cp.wait()

    def weight_copies(e, slot):
        return [pltpu.make_async_copy(src.at[layer, e], dst.at[slot], w_sem.at[slot])
                for src, dst in ((wg_hbm, wg_stage), (wu_hbm, wu_stage), (wd_hbm, wd_stage))]

    for cp in weight_copies(0, 0):
        cp.start()

    @pl.when(nblocks > 0)
    def _():
        gather_start(0, 0)

    def expert(e, carry):
        ws = e % 2
        for cp in weight_copies(e, ws):
            cp.wait()
        wgu_bf[:, :D_EXPERT] = wg_stage[ws].astype(BF16)
        wgu_bf[:, D_EXPERT:] = wu_stage[ws].astype(BF16)
        wd_bf[...] = wd_stage[ws].astype(BF16)

        @pl.when(e + 1 < N_EXPERTS)
        def _():
            for cp in weight_copies(e + 1, 1 - ws):
                cp.start()

        def block(blk, c):
            slot = blk % 2

            @pl.when(blk + 1 < nblocks)
            def _():
                gather_start(blk + 1, 1 - slot)

            gather_wait(blk, slot)

            @pl.when(blk >= 2)
            def _():
                scatter(blk - 2, slot, start=False)

            gu = _dg(lhs_buf[slot], wgu_bf[...])
            gate = gu[:, :D_EXPERT]
            hid = gate * jax.nn.sigmoid(gate) * gu[:, D_EXPERT:]
            res_buf[slot] = _dg(hid.astype(BF16), wd_bf[...]).astype(BF16)
            scatter(blk, slot, start=True)
            return c

        lax.fori_loop(blk0_ref[e], blk0_ref[e + 1], block, 0)
        return carry

    lax.fori_loop(0, N_EXPERTS, expert, 0)

    for back in (2, 1):
        @pl.when(nblocks >= back)
        def _():
            blk = nblocks - back
            scatter(blk, blk % 2, start=False)


def _ffn(xs, zeros, order, first, left, blk0, w_gate, w_up, w_down, layer):
    nt, srows, d = xs.shape
    chunked = (nt * srows // BF16_ROWS, BF16_ROWS, d)
    brows = MOE_BLOCK_CHUNKS * BF16_ROWS
    anywhere = pl.BlockSpec(memory_space=pl.ANY)
    f = w_gate.shape[-1]
    ys = pl.pallas_call(
        functools.partial(_ffn_kernel, layer=layer),
        grid_spec=pltpu.PrefetchScalarGridSpec(
            num_scalar_prefetch=4,
            grid=(1,),
            in_specs=[anywhere] * 5,
            out_specs=anywhere,
            scratch_shapes=[pltpu.VMEM((2, brows, d), BF16), pltpu.VMEM((2, brows, d), BF16),
                            pltpu.VMEM((2, d, f), F32), pltpu.VMEM((2, d, f), F32), pltpu.VMEM((2, f, d), F32),
                            pltpu.VMEM((d, 2 * f), BF16), pltpu.VMEM((f, d), BF16),
                            pltpu.SemaphoreType.DMA((2,)), pltpu.SemaphoreType.DMA((2,)),
                            pltpu.SemaphoreType.DMA((2,))],
        ),
        out_shape=jax.ShapeDtypeStruct(chunked, BF16),
        input_output_aliases={8: 0},
        compiler_params=_params(("arbitrary",)),
        name="expert_ffn",
    )(order, first, left, blk0, xs.reshape(chunked), w_gate, w_up, w_down, zeros.reshape(chunked))
    return ys.reshape(nt, srows, d)


def _block_tables(cinfo):
    nt = cinfo.shape[0]
    nb = MOE_BLOCK_CHUNKS
    nchunks = nt * MOE_CHUNKS
    chunk0 = cinfo[:, 0, :N_EXPERTS]
    chunks = cinfo[:, 1, :N_EXPERTS]
    ends = chunk0 + chunks
    c = jnp.arange(MOE_CHUNKS, dtype=I32)
    label = jnp.sum((c[None, :, None] >= ends[:, None, :]).astype(I32), axis=-1)
    order = jnp.argsort(label.reshape(-1), stable=True).astype(I32)
    cnt = chunks.sum(axis=0).astype(I32)
    cstart = jnp.cumsum(cnt) - cnt
    nblk = (cnt + nb - 1) // nb
    blk0 = jnp.concatenate([jnp.zeros((1,), I32), jnp.cumsum(nblk).astype(I32)])
    max_blocks = nchunks // nb + N_EXPERTS
    j = jnp.arange(max_blocks, dtype=I32)
    ej = jnp.minimum(jnp.sum((j[:, None] >= blk0[None, 1:]).astype(I32), axis=1), N_EXPERTS - 1)
    mine = (ej[:, None] == jnp.arange(N_EXPERTS, dtype=I32)[None, :]).astype(I32)
    local = j - jnp.sum(mine * blk0[None, :N_EXPERTS], axis=1)
    first = jnp.sum(mine * cstart[None, :], axis=1) + local * nb
    left = jnp.sum(mine * cnt[None, :], axis=1) - local * nb
    return order, jnp.clip(first, 0, nchunks - 1).astype(I32), jnp.maximum(left, 0).astype(I32), blk0


def _ple_kernel(x1_ref, ys_ref, rinfo_ref, pp_ref, ps_ref, g_ref, wg_ref, wp_ref, gfin_ref, outp_ref, outs_ref, *,
                final, prompt_tiles):
    tm = MOE_TILE
    rows = MOE_SORTED_ROWS
    i = pl.program_id(0)
    rinfo = rinfo_ref[...]
    pos1 = rinfo[:, 0:1].astype(I32)
    pos2 = rinfo[:, 1:2].astype(I32)
    w1 = rinfo[:, 2:3]
    w2 = rinfo[:, 3:4]
    slot = lax.broadcasted_iota(I32, (tm, rows), 1)
    ys = ys_ref[0]
    y1 = _dg((slot == pos1).astype(BF16), ys)
    y2 = _dg((slot == pos2).astype(BF16), ys)
    x = x1_ref[...] + (w1 * y1 + w2 * y2)
    gate = jax.nn.sigmoid(_dg(_rmsnorm(x, g_ref[...]).astype(BF16), wg_ref[...]))
    p = _pair_load(i, prompt_tiles, pp_ref, ps_ref)
    x = x + gate * _dg(p.astype(BF16), wp_ref[...])
    if final:
        x = _rmsnorm(x, gfin_ref[...])

    @pl.when(i < prompt_tiles)
    def _():
        outp_ref[...] = x

    @pl.when(i >= prompt_tiles)
    def _():
        outs_ref[...] = x


def _ple(x1, ys, rinfo, p_prompt, p_sample, layer, g, wg, wp, gfin, final):
    t, d = x1.shape
    tm = MOE_TILE
    nt = t // tm
    tp, ts = p_prompt.shape[1], p_sample.shape[1]
    assert tp % tm == 0 and ts % tm == 0 and tp + ts == t
    pt = tp // tm
    pd = p_prompt.shape[2]
    ys = ys.reshape(-1, MOE_SORTED_ROWS, d)
    return pl.pallas_call(
        functools.partial(_ple_kernel, final=final, prompt_tiles=pt),
        grid=(nt,),
        in_specs=[pl.BlockSpec((tm, d), lambda i: (i, 0)),
                  pl.BlockSpec((1, MOE_SORTED_ROWS, d), lambda i: (i, 0, 0)),
                  pl.BlockSpec((tm, LANES), lambda i: (i, 0)),
                  pl.BlockSpec((None, tm, pd), lambda i: (layer, jnp.minimum(i, pt - 1), 0)),
                  pl.BlockSpec((None, tm, pd), lambda i: (layer, jnp.maximum(i - pt, 0), 0)),
                  _resident((1, d)), _resident(wg.shape), _resident(wp.shape), _resident((1, d))],
        out_specs=_pair_specs(tm, d, pt),
        out_shape=[jax.ShapeDtypeStruct((tp, d), F32), jax.ShapeDtypeStruct((ts, d), F32)],
        compiler_params=_params(("arbitrary",)),
        name="ple",
    )(x1, ys, rinfo, p_prompt, p_sample, g, wg, wp, gfin)


def _lower_bounds(lb_param):
    s = jax.nn.softmax(lb_param.astype(F32), axis=0)
    c = jnp.cumsum(s, axis=0)
    return c - c[0:1]


def _split_kernel(w_ref, hi_ref, lo_ref):
    hi_ref[...], lo_ref[...] = _split2(w_ref[...])


def _hi_lo(w):
    k, n = w.shape
    tk = ROW_TILE
    assert k % tk == 0
    spec = pl.BlockSpec((tk, n), lambda i: (i, 0))
    return pl.pallas_call(
        _split_kernel,
        grid=(k // tk,),
        in_specs=[spec],
        out_specs=[spec, spec],
        out_shape=[jax.ShapeDtypeStruct((k, n), BF16)] * 2,
        compiler_params=_params(("parallel",)),
        name="split_weight",
    )(w)


def kernel(x_prompt, x_sample, state_hgrn, state_conv, p_prompt, p_sample, g_mix, w_in, hg_lower, g_hg_out, w_br_a, w_conv, w_br_b, w_out, g_ffn, w_router_group, b_router_group, w_router_expert, b_router_expert, w_gate, w_up, w_down, g_ple, w_ple_gate, w_ple_proj, g_final):
    depth = w_in.shape[0]
    bp, seq, d = x_prompt.shape
    bs, steps, _ = x_sample.shape
    assert d == D_MODEL and w_conv.shape[1] == CONV_W
    assert TAIL_ROWS % ROW_TILE == 0 and seq >= TAIL_ROWS
    tp = bp * seq
    ts = bs * steps
    hw = N_HEADS * HEAD_DIM

    xp, xs_rows = x_prompt.reshape(tp, d), x_sample.reshape(ts, d)
    pp = p_prompt.reshape(depth, tp, -1)
    ps = p_sample.reshape(depth, ts, -1)
    lbs = _lower_bounds(hg_lower)
    row = lambda a: a.reshape(1, -1)

    hg_p, cv_p, cv_s = [], [], []
    hg_s = None
    for li in range(depth):
        careful = li < depth - 1
        tm = ROW_TILE if careful else FAST_ROW_TILE
        tail = (seq // tm, TAIL_ROWS // tm, tp // tm) if careful else None
        act = F32 if careful else BF16
        wi = w_in[li]
        w_scan = jnp.concatenate([wi[:, :4 * hw], wi[:, 4 * hw + d:4 * hw + 3 * d]], axis=1)
        w_cbg = jnp.concatenate([wi[:, 4 * hw:4 * hw + d], wi[:, 4 * hw + 3 * d:]], axis=1)
        mix_w = (w_cbg, w_br_a[li], w_br_b[li], w_out[li])
        if careful:
            w_scan, w_scan_lo = _hi_lo(w_scan)
            mix_w, mix_w_lo = zip(*[_hi_lo(w) for w in mix_w])
        else:
            w_scan, w_scan_lo = w_scan.astype(BF16), None
            mix_w, mix_w_lo = [w.astype(BF16) for w in mix_w], None

        qs, lf, kf, v, sog, u = _inproj(xp, xs_rows, row(g_mix[li]), row(lbs[li]), w_scan, w_scan_lo, tail, act, tm)

        gh = row(g_hg_out[li])
        ogp, s_p = _scan_prompt(qs, lf, kf, v, sog, gh, bp, seq, TAIL_ROWS if careful else None)
        ogs, hg_s = _scan_sample(qs, lf, kf, v, sog, gh, state_hgrn, li, hg_s, tp, steps, careful)
        hg_p.append(s_p)
        keep = CONV_W - 1
        cv_p.append(jnp.stack([lax.slice_in_dim(u, (b + 1) * seq - keep, (b + 1) * seq) for b in range(bp)]))
        cv_s.append(jnp.concatenate([state_conv[li], lax.slice_in_dim(u, tp, tp + ts).reshape(bs, steps, d)],
                                    axis=1)[:, steps:])

        buf = state_conv[li]
        zero = jnp.zeros((bs, steps - 1, d), F32)
        e1 = jnp.concatenate([buf[:, 1:2], zero], axis=1).reshape(ts, d)
        e2 = jnp.concatenate([buf, zero[:, 1:]], axis=1).reshape(ts, d)
        x1 = _mix(xp, xs_rows, ogp, ogs, u, e1, e2, row(g_mix[li]), w_conv[li], mix_w, mix_w_lo, seq, steps, tail,
                  tm)

        wr = jnp.zeros((d, LANES), F32).at[:, :N_EXPERTS].set(w_router_expert[li])
        wr = wr.at[:, GROUP_LANE0:GROUP_LANE0 + N_GROUPS].set(w_router_group[li])
        br = jnp.zeros((1, LANES), F32).at[0, :N_EXPERTS].set(b_router_expert[li])
        br = br.at[0, GROUP_LANE0:GROUP_LANE0 + N_GROUPS].set(b_router_group[li])
        xsort, zeros, rinfo, cinfo = _route(x1, row(g_ffn[li]), wr, br)
        ys = _ffn(xsort, zeros, *_block_tables(cinfo), w_gate, w_up, w_down, li)

        xp, xs_rows = _ple(x1, ys, rinfo, pp, ps, li, row(g_ple[li]), w_ple_gate[li].astype(BF16),
                           w_ple_proj[li].astype(BF16), row(g_final), li == depth - 1)

    return (xp.reshape(bp, seq, d), xs_rows.reshape(bs, steps, d), jnp.stack(hg_p), jnp.stack(cv_p), hg_s,
            jnp.stack(cv_s))
```

```python
import functools

import jax
import jax.numpy as jnp
from jax import lax
from jax.experimental import pallas as pl
from jax.experimental.pallas import tpu as pltpu

F32 = jnp.float32
BF16 = jnp.bfloat16
I32 = jnp.int32

D_MODEL = 1024
N_HEADS = 8
HEAD_DIM = 128
CONV_W = 3
N_GROUPS = 4
EXPERTS_PER_GROUP = 8
N_EXPERTS = N_GROUPS * EXPERTS_PER_GROUP
D_EXPERT = 256
EPS = 1e-6

LANES = 128
SUBLANES = 8
BF16_ROWS = 16
VMEM_LIMIT = 56 * 1024 * 1024

ROW_TILE = 256
FAST_ROW_TILE = 512
SCAN_CHUNK = 128
SCAN_STEP_CHUNKS = 2
SCAN_HEAD_UNROLL = 2
TAIL_ROWS = 256
SAMPLE_SEQS = 8
MOE_TILE = 256
MOE_SORTED_ROWS = 1024
MOE_CHUNKS = MOE_SORTED_ROWS // BF16_ROWS
MOE_BLOCK_CHUNKS = 32
GROUP_LANE0 = N_EXPERTS

_NN = (((1,), (0,)), ((), ()))
_NT = (((1,), (1,)), ((), ()))
_TN = (((0,), (0,)), ((), ()))


def _params(sem):
    return pltpu.CompilerParams(dimension_semantics=sem, vmem_limit_bytes=VMEM_LIMIT)


def _resident(shape):
    nd = len(shape)
    return pl.BlockSpec(shape, lambda *_: (0,) * nd, pipeline_mode=pl.Buffered(1))


def _rmsnorm(x, g):
    r = lax.rsqrt(jnp.mean(x * x, axis=-1, keepdims=True) + EPS)
    return (x * r) * g


def _dg(a, b, dims=_NN):
    return lax.dot_general(a, b, dims, preferred_element_type=F32)


def _split2(x):
    hi = x.astype(BF16)
    return hi, (x - hi.astype(F32)).astype(BF16)


def _mm(a, b, precise, dims=_NN):
    if not precise:
        return _dg(a.astype(BF16), b.astype(BF16), dims)
    ah, al = _split2(a.astype(F32))
    bh, bl = _split2(b.astype(F32))
    return _dg(ah, bh, dims) + (_dg(al, bh, dims) + _dg(ah, bl, dims))


def _mmw(a, w_hi, w_lo, precise):
    if not precise:
        return _dg(a.astype(BF16), w_hi)
    ah, al = _split2(a.astype(F32))
    return _dg(ah, w_hi) + (_dg(al, w_hi) + _dg(ah, w_lo))


def _split3(x):
    hi = x.astype(BF16)
    r1 = x - hi.astype(F32)
    mid = r1.astype(BF16)
    lo = (r1 - mid.astype(F32)).astype(BF16)
    return hi, mid, lo


def _dot_exact_lhs(m01, x):
    hi, mid, lo = _split3(x)
    return _dg(m01, hi) + _dg(m01, mid) + _dg(m01, lo)


def _either(precise_pred, body):
    if precise_pred is None:
        body(False)
    else:
        pl.when(precise_pred)(lambda: body(True))
        pl.when(jnp.logical_not(precise_pred))(lambda: body(False))


def _tail_tile(i, tiles_per_seq, tail_tiles, prompt_tiles):
    return (i >= prompt_tiles) | ((i % tiles_per_seq) >= tiles_per_seq - tail_tiles)


def _pair_specs(tm, cols, prompt_tiles):
    return [pl.BlockSpec((tm, cols), lambda i: (jnp.minimum(i, prompt_tiles - 1), 0)),
            pl.BlockSpec((tm, cols), lambda i: (jnp.maximum(i - prompt_tiles, 0), 0))]


def _pair_load(i, prompt_tiles, p_ref, s_ref):
    return jnp.where(i < prompt_tiles, p_ref[...], s_ref[...])


def _w_slice(ref, sl):
    return None if ref is None else ref[:, sl]


def _inproj_kernel(*refs, tail, prompt_tiles):
    if tail is None:
        xp_ref, xs_ref, g_ref, lb_ref, w_ref, qs_ref, lf_ref, kf_ref, v_ref, sog_ref, u_ref = refs
        wl_ref = None
    else:
        xp_ref, xs_ref, g_ref, lb_ref, w_ref, wl_ref, qs_ref, lf_ref, kf_ref, v_ref, sog_ref, u_ref = refs
    d = D_MODEL
    i = pl.program_id(0)

    def body(precise):
        h = _rmsnorm(_pair_load(i, prompt_tiles, xp_ref, xs_ref), g_ref[...])

        def seg(j):
            sl = slice(j * d, (j + 1) * d)
            return _mmw(h, w_ref[:, sl], _w_slice(wl_ref, sl), precise)

        q = seg(0)
        qs_ref[...] = q * jax.nn.sigmoid(q)
        fp = seg(1)
        lb = lb_ref[...]
        log_sig = jnp.minimum(fp, 0.0) - jnp.log1p(jnp.exp(-jnp.abs(fp)))
        a = jnp.log(lb)
        b = jnp.log1p(-lb) + log_sig
        lf_ref[...] = jnp.maximum(a, b) + jnp.log1p(jnp.exp(-jnp.abs(a - b)))
        kf_ref[...] = (1.0 - lb) * jax.nn.sigmoid(-fp)
        v_ref[...] = seg(2).astype(v_ref.dtype)
        og = seg(3)
        sog_ref[...] = (og * jax.nn.sigmoid(og)).astype(sog_ref.dtype)
        u_ref[...] = seg(4) * seg(5)

    _either(None if tail is None else _tail_tile(i, *tail), body)


def _inproj(xp, xs, g, lb, w, w_lo, tail, act_dtype, tm):
    d = D_MODEL
    assert xp.shape[0] % tm == 0 and xs.shape[0] % tm == 0 and xp.shape[1] == d
    pt = xp.shape[0] // tm
    t = xp.shape[0] + xs.shape[0]
    row = pl.BlockSpec((tm, d), lambda i: (i, 0))
    ws = [w] if tail is None else [w, w_lo]
    return pl.pallas_call(
        functools.partial(_inproj_kernel, tail=tail, prompt_tiles=pt),
        grid=(t // tm,),
        in_specs=_pair_specs(tm, d, pt) + [_resident((1, d)), _resident((1, d))] + [_resident(a.shape) for a in ws],
        out_specs=[row] * 6,
        out_shape=[jax.ShapeDtypeStruct((t, d), dt) for dt in (F32, F32, F32, act_dtype, act_dtype, F32)],
        compiler_params=_params(("parallel",)),
        name="inproj",
    )(xp, xs, g, lb, *ws)


def _score_masks(n_rows, block):
    xr = (lax.broadcasted_iota(I32, (n_rows, n_rows), 0)
          ^ lax.broadcasted_iota(I32, (n_rows, n_rows), 1))
    masks = {0: (xr == 0).astype(F32)}
    b = 1
    while 2 * b < n_rows and b < block:
        masks[b] = (xr < 2 * b).astype(F32)
        b *= 2
    return masks


def _level_operands(q, k, g, cum, n_rows, block):
    ri = lax.broadcasted_iota(I32, (n_rows, LANES), 0)
    b = 1
    while b < block:
        if b >= SUBLANES:
            shape3 = (n_rows // (2 * b), 2 * b, LANES)
            c3, q3, k3 = cum.reshape(shape3), q.reshape(shape3), k.reshape(shape3)
            mid = c3[:, b - 1:b, :]
            zero = jnp.zeros((shape3[0], b, LANES), F32)
            qt = jnp.concatenate([zero, q3[:, b:, :] * jnp.exp(c3[:, b:, :] - mid)], axis=1).reshape(n_rows, LANES)
            kt = jnp.concatenate([k3[:, :b, :] * jnp.exp(mid - c3[:, :b, :]), zero], axis=1).reshape(n_rows, LANES)
        else:
            upper = (ri & b) != 0
            if b == 1:
                z = jnp.where(upper, g, 0.0)
            elif b == 2:
                g_prev = pltpu.roll(g, 1, 0)
                g_next = pltpu.roll(g, n_rows - 1, 0)
                m4 = ri & 3
                z = jnp.where(m4 == 2, g, jnp.where(m4 == 3, g + g_prev, jnp.where(m4 == 0, g_next, 0.0)))
            else:
                c3 = cum.reshape(n_rows // (2 * b), 2 * b, LANES)
                mid = jnp.broadcast_to(c3[:, b - 1:b, :], c3.shape).reshape(n_rows, LANES)
                z = jnp.where(upper, cum - mid, mid - cum)
            e = jnp.exp(z)
            qt = jnp.where(upper, q * e, 0.0)
            kt = jnp.where(upper, 0.0, k * e)
        yield b, qt, kt
        b *= 2


def _intra_scores(q, k, g, cum, n_rows, block, precise, masks):
    s = masks[0] * jnp.sum(q * k, axis=1, keepdims=True)
    for b, qt, kt in _level_operands(q, k, g, cum, n_rows, block):
        prod = _mm(qt, kt, precise, _NT)
        s = s + (prod * masks[b] if b in masks else prod)
    return s


def _decay_columns(e_row):
    hi = e_row.astype(BF16).astype(F32)
    lo = e_row - hi
    r = lax.broadcasted_iota(I32, (BF16_ROWS, LANES), 0)
    stacked = jnp.where(r == 0, hi, jnp.where(r == 1, lo, 0.0)).astype(BF16)
    return _dg(stacked, jnp.ones((BF16_ROWS, LANES), BF16), _TN)


def _head_out(o, gh, sog, dtype):
    on = o * lax.rsqrt(jnp.mean(o * o, axis=-1, keepdims=True) + EPS) * gh
    return (on * sog.astype(F32)).astype(dtype)


def _scan_step_staged(qs_ref, lf_ref, kf_ref, v_ref, sog_ref, gh_ref, og_ref, s_scr,
                      cum_scr, qe_scr, ke_scr, qt_scr, kt_scr, sc_scr):
    n = SCAN_CHUNK
    rows = n * SCAN_STEP_CHUNKS
    shift = n.bit_length() - 1
    ri = lax.broadcasted_iota(I32, (rows, rows), 0)
    ci = lax.broadcasted_iota(I32, (rows, rows), 1)
    tri = (((ri >> shift) == (ci >> shift)) & (ri >= ci)).astype(BF16)
    cum_scr[...] = _dot_exact_lhs(tri, lf_ref[...])
    masks = _score_masks(n, n)
    chunks = [slice(j * n, (j + 1) * n) for j in range(SCAN_STEP_CHUNKS)]

    def head_lanes(h):
        return pl.ds(pl.multiple_of(h * HEAD_DIM, HEAD_DIM), HEAD_DIM)

    def operands(h, carry):
        hs = head_lanes(h)
        for rs in chunks:
            q, g, k, cum = qs_ref[rs, hs], lf_ref[rs, hs], kf_ref[rs, hs], cum_scr[rs, hs]
            qe_scr[rs, hs] = (q * jnp.exp(cum)).astype(BF16)
            ke_scr[rs, hs] = (k * jnp.exp(cum[n - 1:n, :] - cum)).astype(BF16)
            for level, (_, qt, kt) in enumerate(_level_operands(q, k, g, cum, n, n)):
                qt_scr[level, rs, hs] = qt.astype(BF16)
                kt_scr[level, rs, hs] = kt.astype(BF16)
        return carry

    def scores(h, carry):
        hs = head_lanes(h)
        for rs in chunks:
            s = masks[0] * jnp.sum(qs_ref[rs, hs] * kf_ref[rs, hs], axis=1, keepdims=True)
            b = 1
            for level in range(qt_scr.shape[0]):
                prod = _dg(qt_scr[level, rs, hs], kt_scr[level, rs, hs], _NT)
                s = s + (prod * masks[b] if b in masks else prod)
                b *= 2
            sc_scr[h, rs, :] = s.astype(BF16)
        return carry

    def outputs(h, carry):
        hs = head_lanes(h)
        s_old = s_scr[h]
        for rs in chunks:
            v = v_ref[rs, hs].astype(BF16)
            o = _dg(qe_scr[rs, hs], s_old.astype(BF16)) + _dg(sc_scr[h, rs, :], v)
            tot = cum_scr[rs.stop - 1:rs.stop, hs]
            s_old = _decay_columns(jnp.exp(tot)) * s_old + _dg(ke_scr[rs, hs], v, _TN)
            og_ref[rs, hs] = _head_out(o, gh_ref[:, hs], sog_ref[rs, hs], og_ref.dtype)
        s_scr[h] = s_old
        return carry

    lax.fori_loop(0, N_HEADS, operands, 0)
    lax.fori_loop(0, N_HEADS, scores, 0, unroll=True)
    lax.fori_loop(0, N_HEADS, outputs, 0, unroll=True)


def _scan_prompt_kernel(qs_ref, lf_ref, kf_ref, v_ref, sog_ref, gh_ref, og_ref, sfin_ref, s_scr, *stage_scr,
                        tail_steps):
    c = pl.program_id(1)
    nc = pl.num_programs(1)
    n = SCAN_CHUNK

    @pl.when(c == 0)
    def _():
        s_scr[...] = jnp.zeros_like(s_scr)

    def body(precise):
        if not precise:
            _scan_step_staged(qs_ref, lf_ref, kf_ref, v_ref, sog_ref, gh_ref, og_ref, s_scr, *stage_scr)
            return
        tri = (lax.broadcasted_iota(I32, (n, n), 0) >= lax.broadcasted_iota(I32, (n, n), 1)).astype(BF16)
        masks = _score_masks(n, n)

        def head(h, carry):
            hs = pl.ds(pl.multiple_of(h * HEAD_DIM, HEAD_DIM), HEAD_DIM)
            s_old = s_scr[h]
            for j in range(SCAN_STEP_CHUNKS):
                rs = slice(j * n, (j + 1) * n)
                q, g, k, v = qs_ref[rs, hs], lf_ref[rs, hs], kf_ref[rs, hs], v_ref[rs, hs]
                cum = _dot_exact_lhs(tri, g)
                tot = cum[n - 1:n, :]
                o = _mm(q * jnp.exp(cum), s_old, precise)
                o = o + _mm(_intra_scores(q, k, g, cum, n, n, precise, masks), v, precise)
                ke = k * jnp.exp(tot - cum)
                s_old = _decay_columns(jnp.exp(tot)) * s_old + _mm(ke, v, precise, _TN)
                og_ref[rs, hs] = _head_out(o, gh_ref[:, hs], sog_ref[rs, hs], og_ref.dtype)
            s_scr[h] = s_old
            return carry

        lax.fori_loop(0, N_HEADS, head, 0)

    _either(None if tail_steps is None else c >= nc - tail_steps, body)

    @pl.when(c == nc - 1)
    def _():
        sfin_ref[0] = s_scr[...]


def _scan_prompt(qs, lf, kf, v, sog, gh, batch, seq, tail_rows):
    t, d = qs.shape
    n = SCAN_CHUNK * SCAN_STEP_CHUNKS
    assert seq % n == 0 and (tail_rows is None or tail_rows % n == 0)
    nc = seq // n
    levels = SCAN_CHUNK.bit_length() - 1
    row = pl.BlockSpec((n, d), lambda b, c: (b * nc + c, 0))
    return pl.pallas_call(
        functools.partial(_scan_prompt_kernel, tail_steps=None if tail_rows is None else tail_rows // n),
        grid=(batch, nc),
        in_specs=[row, row, row, row, row, _resident((1, d))],
        out_specs=[row, pl.BlockSpec((1, N_HEADS, HEAD_DIM, HEAD_DIM), lambda b, c: (b, 0, 0, 0))],
        out_shape=[jax.ShapeDtypeStruct((batch * seq, d), v.dtype),
                   jax.ShapeDtypeStruct((batch, N_HEADS, HEAD_DIM, HEAD_DIM), F32)],
        scratch_shapes=[pltpu.VMEM((N_HEADS, HEAD_DIM, HEAD_DIM), F32),
                        pltpu.VMEM((n, d), F32),
                        pltpu.VMEM((n, d), BF16), pltpu.VMEM((n, d), BF16),
                        pltpu.VMEM((levels, n, d), BF16), pltpu.VMEM((levels, n, d), BF16),
                        pltpu.VMEM((N_HEADS, n, SCAN_CHUNK), BF16)],
        compiler_params=_params(("parallel", "arbitrary")),
        name="scan_prompt",
    )(qs, lf, kf, v, sog, gh)


def _scan_sample_kernel(*refs, steps, precise, layer, chained):
    if chained:
        qs_ref, lf_ref, kf_ref, v_ref, sog_ref, gh_ref, s_ref, _, og_ref, snew_ref = refs
    else:
        qs_ref, lf_ref, kf_ref, v_ref, sog_ref, gh_ref, s_ref, og_ref, all_ref = refs
        for other in range(all_ref.shape[0]):
            if other != layer:
                all_ref[other] = jnp.zeros(all_ref.shape[1:], F32)
        snew_ref = all_ref.at[layer]
    n = SAMPLE_SEQS * steps
    shift = steps.bit_length() - 1
    ri = lax.broadcasted_iota(I32, (n, n), 0)
    ci = lax.broadcasted_iota(I32, (n, n), 1)
    same = (ri >> shift) == (ci >> shift)
    tri = (same & (ri >= ci)).astype(BF16)
    tri_after = (same & (ci > ri)).astype(BF16)
    seq_of_row = lax.broadcasted_iota(I32, (n, LANES), 0) >> shift
    masks = _score_masks(n, steps)

    def head(h, carry):
        hs = pl.ds(pl.multiple_of(h * HEAD_DIM, HEAD_DIM), HEAD_DIM)
        q, g, k, v = qs_ref[:, hs], lf_ref[:, hs], kf_ref[:, hs], v_ref[:, hs]
        cum = _dot_exact_lhs(tri, g)
        after = _dot_exact_lhs(tri_after, g)
        o = _mm(_intra_scores(q, k, g, cum, n, steps, precise, masks), v, precise)
        qe = q * jnp.exp(cum)
        ke = k * jnp.exp(after)
        for s in range(SAMPLE_SEQS):
            mine = seq_of_row == s
            s_old = s_ref[s, h]
            o = o + jnp.where(mine, _mm(qe, s_old, precise), 0.0)
            kv = _mm(jnp.where(mine, ke, 0.0), v, precise, _TN)
            last = (s + 1) * steps - 1
            snew_ref[s, h] = _decay_columns(jnp.exp(cum[last:last + 1, :])) * s_old + kv
        og_ref[:, hs] = _head_out(o, gh_ref[:, hs], sog_ref[:, hs], og_ref.dtype)
        return carry

    lax.fori_loop(0, N_HEADS, head, 0)


def _scan_sample(qs, lf, kf, v, sog, gh, states, layer, new_states, row0, steps, precise):
    t, d = qs.shape
    depth, nseq = states.shape[:2]
    assert steps & (steps - 1) == 0 and steps >= 4, "sample block must be a power of two >= 4"
    n = SAMPLE_SEQS * steps
    assert nseq % SAMPLE_SEQS == 0 and row0 % n == 0 and n % BF16_ROWS == 0
    b0 = row0 // n
    chained = new_states is not None
    row = pl.BlockSpec((n, d), lambda i: (b0 + i, 0))
    tail = (SAMPLE_SEQS, N_HEADS, HEAD_DIM, HEAD_DIM)
    st = pl.BlockSpec((None,) + tail, lambda i: (layer, i, 0, 0, 0))
    st_all = pl.BlockSpec((depth,) + tail, lambda i: (0, i, 0, 0, 0))
    extra = [new_states] if chained else []
    return pl.pallas_call(
        functools.partial(_scan_sample_kernel, steps=steps, precise=precise, layer=layer, chained=chained),
        grid=(nseq // SAMPLE_SEQS,),
        in_specs=[row, row, row, row, row, _resident((1, d)), st] + [pl.BlockSpec(memory_space=pl.ANY)] * len(extra),
        out_specs=[pl.BlockSpec((n, d), lambda i: (i, 0)), st if chained else st_all],
        out_shape=[jax.ShapeDtypeStruct((nseq * steps, d), v.dtype), jax.ShapeDtypeStruct(states.shape, F32)],
        input_output_aliases={7: 1} if chained else {},
        compiler_params=_params(("parallel",)),
        name="scan_sample",
    )(qs, lf, kf, v, sog, gh, states, *extra)


def _mix_kernel(*refs, tm, tiles_per_seq, prompt_tiles, steps, tail):
    if tail is None:
        (xp_ref, xs_ref, ogp_ref, ogs_ref, u_ref, halo_ref, e1_ref, e2_ref, g_ref, wc_ref,
         w_cbg_ref, wa_ref, wb_ref, wo_ref, x1_ref) = refs
        l_cbg_ref = la_ref = lb_ref = lo_ref = None
    else:
        (xp_ref, xs_ref, ogp_ref, ogs_ref, u_ref, halo_ref, e1_ref, e2_ref, g_ref, wc_ref,
         w_cbg_ref, wa_ref, wb_ref, wo_ref, l_cbg_ref, la_ref, lb_ref, lo_ref, x1_ref) = refs
    d = D_MODEL
    i = pl.program_id(0)

    def full(ref):
        return None if ref is None else ref[...]

    def body(precise):
        x = _pair_load(i, prompt_tiles, xp_ref, xs_ref)
        h = _rmsnorm(x, g_ref[...])
        u = u_ref[...]
        ri = lax.broadcasted_iota(I32, (tm, d), 0)
        r1 = pltpu.roll(u, 1, 0)
        r2 = pltpu.roll(u, 2, 0)
        keep = jnp.where(i % tiles_per_seq == 0, 0.0, 1.0)
        halo = halo_ref[...]
        h_last = halo[7:8, :] * keep
        h_prev = halo[6:7, :] * keep
        is_sample = i >= prompt_tiles
        t_in_seq = ri & (steps - 1)
        prev1 = jnp.where(is_sample,
                          jnp.where(t_in_seq == 0, e1_ref[...], r1),
                          jnp.where(ri == 0, h_last, r1))
        prev2 = jnp.where(is_sample,
                          jnp.where(t_in_seq < 2, e2_ref[...], r2),
                          jnp.where(ri == 0, h_prev, jnp.where(ri == 1, h_last, r2)))
        wc = wc_ref[...]
        conv = prev2 * wc[0:1, :] + prev1 * wc[1:2, :] + u * wc[2:3, :]

        def seg(j):
            sl = slice(j * d, (j + 1) * d)
            return _mmw(h, w_cbg_ref[:, sl], _w_slice(l_cbg_ref, sl), precise)

        y_b = _mmw(seg(0) * conv, wb_ref[...], full(lb_ref), precise)
        y_a = _mmw(_pair_load(i, prompt_tiles, ogp_ref, ogs_ref), wa_ref[...], full(la_ref), precise)
        m = jax.nn.sigmoid(seg(1)) * y_a + jax.nn.sigmoid(seg(2)) * y_b
        x1_ref[...] = x + _mmw(m, wo_ref[...], full(lo_ref), precise)

    _either(None if tail is None else _tail_tile(i, *tail), body)


def _mix(xp, xs, ogp, ogs, u, e1, e2, g, wc, ws, ws_lo, seq, steps, tail, tm):
    t, d = u.shape
    prompt_rows = xp.shape[0]
    assert seq % tm == 0 and prompt_rows % tm == 0 and (t - prompt_rows) % tm == 0 and tm % steps == 0
    pt = prompt_rows // tm
    row = pl.BlockSpec((tm, d), lambda i: (i, 0))
    halo = pl.BlockSpec((SUBLANES, d), lambda i: (jnp.maximum(i * (tm // SUBLANES) - 1, 0), 0))
    erow = pl.BlockSpec((tm, d), lambda i: (jnp.maximum(i - pt, 0), 0))
    weights = list(ws) + ([] if tail is None else list(ws_lo))
    return pl.pallas_call(
        functools.partial(_mix_kernel, tm=tm, tiles_per_seq=seq // tm, prompt_tiles=pt, steps=steps, tail=tail),
        grid=(t // tm,),
        in_specs=_pair_specs(tm, d, pt) + _pair_specs(tm, d, pt)
        + [row, halo, erow, erow, _resident((1, d)), _resident(wc.shape)]
        + [_resident(a.shape) for a in weights],
        out_specs=row,
        out_shape=jax.ShapeDtypeStruct((t, d), F32),
        compiler_params=_params(("parallel",)),
        name="mix",
    )(xp, xs, ogp, ogs, u, u, e1, e2, g, wc, *weights)


def _route_kernel(x_ref, g_ref, wr_ref, br_ref, xs_ref, zeros_ref, rinfo_ref, cinfo_ref):
    tm = MOE_TILE
    rows = MOE_SORTED_ROWS
    zeros_ref[...] = jnp.zeros_like(zeros_ref)
    hn = _rmsnorm(x_ref[...], g_ref[...])
    hn_hi, hn_lo = _split2(hn)
    wr_hi, wr_lo = _split2(wr_ref[...])
    logits = _dg(hn_hi, wr_hi) + _dg(hn_lo, wr_hi) + _dg(hn_hi, wr_lo) + br_ref[...]

    lane = lax.broadcasted_iota(I32, (tm, LANES), 1)
    neg = -jnp.inf
    big = jnp.int32(LANES)

    def first_lane(mask):
        return jnp.min(jnp.where(mask, lane, big), axis=1, keepdims=True)

    is_group = (lane >= GROUP_LANE0) & (lane < GROUP_LANE0 + N_GROUPS)
    gl = jnp.where(is_group, logits, neg)
    gmax = jnp.max(gl, axis=1, keepdims=True)
    g_idx = first_lane(gl == gmax) - GROUP_LANE0
    g_w = 1.0 / jnp.sum(jnp.exp(gl - gmax), axis=1, keepdims=True)

    in_group = (lane < N_EXPERTS) & ((lane >> (EXPERTS_PER_GROUP.bit_length() - 1)) == g_idx)
    el = jnp.where(in_group, logits, neg)
    emax = jnp.max(el, axis=1, keepdims=True)
    ee = jnp.exp(el - emax)
    prob = ee / jnp.sum(ee, axis=1, keepdims=True)
    prob = jnp.where(in_group, prob, -1.0)
    p1 = jnp.max(prob, axis=1, keepdims=True)
    i1 = first_lane(prob == p1)
    prob2 = jnp.where(lane == i1, -1.0, prob)
    p2 = jnp.max(prob2, axis=1, keepdims=True)
    i2 = first_lane(prob2 == p2)
    psum = p1 + p2
    w1 = p1 / psum * g_w
    w2 = p2 / psum * g_w

    oh1 = (lane == i1).astype(BF16)
    oh2 = (lane == i2).astype(BF16)
    before = (lax.broadcasted_iota(I32, (tm, tm), 1) < lax.broadcasted_iota(I32, (tm, tm), 0)).astype(BF16)
    c1 = _dg(before, oh1)
    c2 = _dg(before, oh2)
    oh1f = oh1.astype(F32)
    oh2f = oh2.astype(F32)
    cnt1 = jnp.sum(oh1f, axis=0, keepdims=True)
    cnt2 = jnp.sum(oh2f, axis=0, keepdims=True)
    chunks = jnp.floor((cnt1 + cnt2 + (BF16_ROWS - 1)) * (1.0 / BF16_ROWS))
    excl = (lax.broadcasted_iota(I32, (LANES, LANES), 0) < lax.broadcasted_iota(I32, (LANES, LANES), 1)).astype(BF16)
    chunk0 = _dg(jnp.broadcast_to(chunks, (BF16_ROWS, LANES)).astype(BF16), excl)[0:1, :]
    base = chunk0 * BF16_ROWS
    pos1 = jnp.sum(oh1f * (base + c1), axis=1, keepdims=True)
    pos2 = jnp.sum(oh2f * (base + cnt1 + c2), axis=1, keepdims=True)

    slot = lax.broadcasted_iota(I32, (tm, rows), 1)
    place = ((slot == pos1.astype(I32)) | (slot == pos2.astype(I32))).astype(BF16)
    xs_ref[0] = _dg(place, hn_hi, _TN).astype(BF16)

    rinfo_ref[...] = jnp.where(lane == 0, pos1, jnp.where(lane == 1, pos2, jnp.where(lane == 2, w1, jnp.where(lane == 3, w2, 0.0))))
    r8 = lax.broadcasted_iota(I32, (SUBLANES, LANES), 0)
    cinfo_ref[0] = jnp.where(r8 == 0, chunk0, jnp.where(r8 == 1, chunks, 0.0)).astype(I32)


def _route(x1, g, wr, br):
    t, d = x1.shape
    tm = MOE_TILE
    assert t % tm == 0
    nt = t // tm
    sorted_spec = pl.BlockSpec((1, MOE_SORTED_ROWS, d), lambda i: (i, 0, 0))
    sorted_shape = jax.ShapeDtypeStruct((nt, MOE_SORTED_ROWS, d), BF16)
    return pl.pallas_call(
        _route_kernel,
        grid=(nt,),
        in_specs=[pl.BlockSpec((tm, d), lambda i: (i, 0)), _resident((1, d)), _resident(wr.shape), _resident(br.shape)],
        out_specs=[sorted_spec, sorted_spec,
                   pl.BlockSpec((tm, LANES), lambda i: (i, 0)),
                   pl.BlockSpec((1, SUBLANES, LANES), lambda i: (i, 0, 0))],
        out_shape=[sorted_shape, sorted_shape,
                   jax.ShapeDtypeStruct((t, LANES), F32),
                   jax.ShapeDtypeStruct((nt, SUBLANES, LANES), I32)],
        compiler_params=_params(("parallel",)),
        name="route",
    )(x1, g, wr, br)


def _ffn_kernel(order_ref, first_ref, left_ref, blk0_ref, xs_hbm, wg_hbm, wu_hbm, wd_hbm, ys_in_hbm, ys_hbm,
                lhs_buf, res_buf, wg_stage, wu_stage, wd_stage, wgu_bf, wd_bf, in_sem, out_sem, w_sem, *, layer):
    del ys_in_hbm
    nb = MOE_BLOCK_CHUNKS
    rows = BF16_ROWS
    nblocks = blk0_ref[N_EXPERTS]

    def chunk_id(blk, i):
        return order_ref[first_ref[blk] + jnp.where(i < left_ref[blk], i, 0)]

    def chunk_copy_in(blk, slot, i):
        return pltpu.make_async_copy(xs_hbm.at[chunk_id(blk, i)],
                                     lhs_buf.at[slot, pl.ds(i * rows, rows)], in_sem.at[slot])

    def chunk_copy_out(blk, slot, i):
        return pltpu.make_async_copy(res_buf.at[slot, pl.ds(i * rows, rows)],
                                     ys_hbm.at[chunk_id(blk, i)], out_sem.at[slot])

    def gather_start(blk, slot):
        for i in range(nb):
            chunk_copy_in(blk, slot, i).start()

    def gather_wait(blk, slot):
        for i in range(nb):
            chunk_copy_in(blk, slot, i).wait()

    def scatter(blk, slot, start):
        n = left_ref[blk]
        for i in range(nb):
            @pl.when(i < n)
            def _():
                cp = chunk_copy_out(blk, slot, i)
                cp.start() if start else cp.wait()

    def weight_copies(e, slot):
        return [pltpu.make_async_copy(src.at[layer, e], dst.at[slot], w_sem.at[slot])
                for src, dst in ((wg_hbm, wg_stage), (wu_hbm, wu_stage), (wd_hbm, wd_stage))]

    for cp in weight_copies(0, 0):
        cp.start()

    @pl.when(nblocks > 0)
    def _():
        gather_start(0, 0)

    def expert(e, carry):
        ws = e % 2
        for cp in weight_copies(e, ws):
            cp.wait()
        wgu_bf[:, :D_EXPERT] = wg_stage[ws].astype(BF16)
        wgu_bf[:, D_EXPERT:] = wu_stage[ws].astype(BF16)
        wd_bf[...] = wd_stage[ws].astype(BF16)

        @pl.when(e + 1 < N_EXPERTS)
        def _():
            for cp in weight_copies(e + 1, 1 - ws):
                cp.start()

        def block(blk, c):
            slot = blk % 2

            @pl.when(blk + 1 < nblocks)
            def _():
                gather_start(blk + 1, 1 - slot)

            gather_wait(blk, slot)

            @pl.when(blk >= 2)
            def _():
                scatter(blk - 2, slot, start=False)

            gu = _dg(lhs_buf[slot], wgu_bf[...])
            gate = gu[:, :D_EXPERT]
            hid = gate * jax.nn.sigmoid(gate) * gu[:, D_EXPERT:]
            res_buf[slot] = _dg(hid.astype(BF16), wd_bf[...]).astype(BF16)
            scatter(blk, slot, start=True)
            return c

        lax.fori_loop(blk0_ref[e], blk0_ref[e + 1], block, 0)
        return carry

    lax.fori_loop(0, N_EXPERTS, expert, 0)

    for back in (2, 1):
        @pl.when(nblocks >= back)
        def _():
            blk = nblocks - back
            scatter(blk, blk % 2, start=False)


def _ffn(xs, zeros, order, first, left, blk0, w_gate, w_up, w_down, layer):
    nt, srows, d = xs.shape
    chunked = (nt * srows // BF16_ROWS, BF16_ROWS, d)
    brows = MOE_BLOCK_CHUNKS * BF16_ROWS
    anywhere = pl.BlockSpec(memory_space=pl.ANY)
    f = w_gate.shape[-1]
    ys = pl.pallas_call(
        functools.partial(_ffn_kernel, layer=layer),
        grid_spec=pltpu.PrefetchScalarGridSpec(
            num_scalar_prefetch=4,
            grid=(1,),
            in_specs=[anywhere] * 5,
            out_specs=anywhere,
            scratch_shapes=[pltpu.VMEM((2, brows, d), BF16), pltpu.VMEM((2, brows, d), BF16),
                            pltpu.VMEM((2, d, f), F32), pltpu.VMEM((2, d, f), F32), pltpu.VMEM((2, f, d), F32),
                            pltpu.VMEM((d, 2 * f), BF16), pltpu.VMEM((f, d), BF16),
                            pltpu.SemaphoreType.DMA((2,)), pltpu.SemaphoreType.DMA((2,)),
                            pltpu.SemaphoreType.DMA((2,))],
        ),
        out_shape=jax.ShapeDtypeStruct(chunked, BF16),
        input_output_aliases={8: 0},
        compiler_params=_params(("arbitrary",)),
        name="expert_ffn",
    )(order, first, left, blk0, xs.reshape(chunked), w_gate, w_up, w_down, zeros.reshape(chunked))
    return ys.reshape(nt, srows, d)


def _block_tables(cinfo):
    nt = cinfo.shape[0]
    nb = MOE_BLOCK_CHUNKS
    nchunks = nt * MOE_CHUNKS
    chunk0 = cinfo[:, 0, :N_EXPERTS]
    chunks = cinfo[:, 1, :N_EXPERTS]
    ends = chunk0 + chunks
    c = jnp.arange(MOE_CHUNKS, dtype=I32)
    label = jnp.sum((c[None, :, None] >= ends[:, None, :]).astype(I32), axis=-1)
    order = jnp.argsort(label.reshape(-1), stable=True).astype(I32)
    cnt = chunks.sum(axis=0).astype(I32)
    cstart = jnp.cumsum(cnt) - cnt
    nblk = (cnt + nb - 1) // nb
    blk0 = jnp.concatenate([jnp.zeros((1,), I32), jnp.cumsum(nblk).astype(I32)])
    max_blocks = nchunks // nb + N_EXPERTS
    j = jnp.arange(max_blocks, dtype=I32)
    ej = jnp.minimum(jnp.sum((j[:, None] >= blk0[None, 1:]).astype(I32), axis=1), N_EXPERTS - 1)
    mine = (ej[:, None] == jnp.arange(N_EXPERTS, dtype=I32)[None, :]).astype(I32)
    local = j - jnp.sum(mine * blk0[None, :N_EXPERTS], axis=1)
    first = jnp.sum(mine * cstart[None, :], axis=1) + local * nb
    left = jnp.sum(mine * cnt[None, :], axis=1) - local * nb
    return order, jnp.clip(first, 0, nchunks - 1).astype(I32), jnp.maximum(left, 0).astype(I32), blk0


def _ple_kernel(x1_ref, ys_ref, rinfo_ref, pp_ref, ps_ref, g_ref, wg_ref, wp_ref, gfin_ref, outp_ref, outs_ref, *,
                final, prompt_tiles):
    tm = MOE_TILE
    rows = MOE_SORTED_ROWS
    i = pl.program_id(0)
    rinfo = rinfo_ref[...]
    pos1 = rinfo[:, 0:1].astype(I32)
    pos2 = rinfo[:, 1:2].astype(I32)
    w1 = rinfo[:, 2:3]
    w2 = rinfo[:, 3:4]
    slot = lax.broadcasted_iota(I32, (tm, rows), 1)
    ys = ys_ref[0]
    y1 = _dg((slot == pos1).astype(BF16), ys)
    y2 = _dg((slot == pos2).astype(BF16), ys)
    x = x1_ref[...] + (w1 * y1 + w2 * y2)
    gate = jax.nn.sigmoid(_dg(_rmsnorm(x, g_ref[...]).astype(BF16), wg_ref[...]))
    p = _pair_load(i, prompt_tiles, pp_ref, ps_ref)
    x = x + gate * _dg(p.astype(BF16), wp_ref[...])
    if final:
        x = _rmsnorm(x, gfin_ref[...])

    @pl.when(i < prompt_tiles)
    def _():
        outp_ref[...] = x

    @pl.when(i >= prompt_tiles)
    def _():
        outs_ref[...] = x


def _ple(x1, ys, rinfo, p_prompt, p_sample, layer, g, wg, wp, gfin, final):
    t, d = x1.shape
    tm = MOE_TILE
    nt = t // tm
    tp, ts = p_prompt.shape[1], p_sample.shape[1]
    assert tp % tm == 0 and ts % tm == 0 and tp + ts == t
    pt = tp // tm
    pd = p_prompt.shape[2]
    ys = ys.reshape(-1, MOE_SORTED_ROWS, d)
    return pl.pallas_call(
        functools.partial(_ple_kernel, final=final, prompt_tiles=pt),
        grid=(nt,),
        in_specs=[pl.BlockSpec((tm, d), lambda i: (i, 0)),
                  pl.BlockSpec((1, MOE_SORTED_ROWS, d), lambda i: (i, 0, 0)),
                  pl.BlockSpec((tm, LANES), lambda i: (i, 0)),
                  pl.BlockSpec((None, tm, pd), lambda i: (layer, jnp.minimum(i, pt - 1), 0)),
                  pl.BlockSpec((None, tm, pd), lambda i: (layer, jnp.maximum(i - pt, 0), 0)),
                  _resident((1, d)), _resident(wg.shape), _resident(wp.shape), _resident((1, d))],
        out_specs=_pair_specs(tm, d, pt),
        out_shape=[jax.ShapeDtypeStruct((tp, d), F32), jax.ShapeDtypeStruct((ts, d), F32)],
        compiler_params=_params(("arbitrary",)),
        name="ple",
    )(x1, ys, rinfo, p_prompt, p_sample, g, wg, wp, gfin)


def _lower_bounds(lb_param):
    s = jax.nn.softmax(lb_param.astype(F32), axis=0)
    c = jnp.cumsum(s, axis=0)
    return c - c[0:1]


def _split_kernel(w_ref, hi_ref, lo_ref):
    hi_ref[...], lo_ref[...] = _split2(w_ref[...])


def _hi_lo(w):
    k, n = w.shape
    tk = ROW_TILE
    assert k % tk == 0
    spec = pl.BlockSpec((tk, n), lambda i: (i, 0))
    return pl.pallas_call(
        _split_kernel,
        grid=(k // tk,),
        in_specs=[spec],
        out_specs=[spec, spec],
        out_shape=[jax.ShapeDtypeStruct((k, n), BF16)] * 2,
        compiler_params=_params(("parallel",)),
        name="split_weight",
    )(w)


def kernel(x_prompt, x_sample, state_hgrn, state_conv, p_prompt, p_sample, g_mix, w_in, hg_lower, g_hg_out, w_br_a, w_conv, w_br_b, w_out, g_ffn, w_router_group, b_router_group, w_router_expert, b_router_expert, w_gate, w_up, w_down, g_ple, w_ple_gate, w_ple_proj, g_final):
    depth = w_in.shape[0]
    bp, seq, d = x_prompt.shape
    bs, steps, _ = x_sample.shape
    assert d == D_MODEL and w_conv.shape[1] == CONV_W
    assert TAIL_ROWS % ROW_TILE == 0 and seq >= TAIL_ROWS
    tp = bp * seq
    ts = bs * steps
    hw = N_HEADS * HEAD_DIM

    xp, xs_rows = x_prompt.reshape(tp, d), x_sample.reshape(ts, d)
    pp = p_prompt.reshape(depth, tp, -1)
    ps = p_sample.reshape(depth, ts, -1)
    lbs = _lower_bounds(hg_lower)
    row = lambda a: a.reshape(1, -1)

    hg_p, cv_p, cv_s = [], [], []
    hg_s = None
    for li in range(depth):
        careful = li < depth - 1
        tm = ROW_TILE if careful else FAST_ROW_TILE
        tail = (seq // tm, TAIL_ROWS // tm, tp // tm) if careful else None
        act = F32 if careful else BF16
        wi = w_in[li]
        w_scan = jnp.concatenate([wi[:, :4 * hw], wi[:, 4 * hw + d:4 * hw + 3 * d]], axis=1)
        w_cbg = jnp.concatenate([wi[:, 4 * hw:4 * hw + d], wi[:, 4 * hw + 3 * d:]], axis=1)
        mix_w = (w_cbg, w_br_a[li], w_br_b[li], w_out[li])
        if careful:
            w_scan, w_scan_lo = _hi_lo(w_scan)
            mix_w, mix_w_lo = zip(*[_hi_lo(w) for w in mix_w])
        else:
            w_scan, w_scan_lo = w_scan.astype(BF16), None
            mix_w, mix_w_lo = [w.astype(BF16) for w in mix_w], None

        qs, lf, kf, v, sog, u = _inproj(xp, xs_rows, row(g_mix[li]), row(lbs[li]), w_scan, w_scan_lo, tail, act, tm)

        gh = row(g_hg_out[li])
        ogp, s_p = _scan_prompt(qs, lf, kf, v, sog, gh, bp, seq, TAIL_ROWS if careful else None)
        ogs, hg_s = _scan_sample(qs, lf, kf, v, sog, gh, state_hgrn, li, hg_s, tp, steps, careful)
        hg_p.append(s_p)
        keep = CONV_W - 1
        cv_p.append(jnp.stack([lax.slice_in_dim(u, (b + 1) * seq - keep, (b + 1) * seq) for b in range(bp)]))
        cv_s.append(jnp.concatenate([state_conv[li], lax.slice_in_dim(u, tp, tp + ts).reshape(bs, steps, d)],
                                    axis=1)[:, steps:])

        buf = state_conv[li]
        zero = jnp.zeros((bs, steps - 1, d), F32)
        e1 = jnp.concatenate([buf[:, 1:2], zero], axis=1).reshape(ts, d)
        e2 = jnp.concatenate([buf, zero[:, 1:]], axis=1).reshape(ts, d)
        x1 = _mix(xp, xs_rows, ogp, ogs, u, e1, e2, row(g_mix[li]), w_conv[li], mix_w, mix_w_lo, seq, steps, tail,
                  tm)

        wr = jnp.zeros((d, LANES), F32).at[:, :N_EXPERTS].set(w_router_expert[li])
        wr = wr.at[:, GROUP_LANE0:GROUP_LANE0 + N_GROUPS].set(w_router_group[li])
        br = jnp.zeros((1, LANES), F32).at[0, :N_EXPERTS].set(b_router_expert[li])
        br = br.at[0, GROUP_LANE0:GROUP_LANE0 + N_GROUPS].set(b_router_group[li])
        xsort, zeros, rinfo, cinfo = _route(x1, row(g_ffn[li]), wr, br)
        ys = _ffn(xsort, zeros, *_block_tables(cinfo), w_gate, w_up, w_down, li)

        xp, xs_rows = _ple(x1, ys, rinfo, pp, ps, li, row(g_ple[li]), w_ple_gate[li].astype(BF16),
                           w_ple_proj[li].astype(BF16), row(g_final), li == depth - 1)

    return (xp.reshape(bp, seq, d), xs_rows.reshape(bs, steps, d), jnp.stack(hg_p), jnp.stack(cv_p), hg_s,
            jnp.stack(cv_s))
```

```python
import functools

import jax
import jax.numpy as jnp
from jax import lax
from jax.experimental import pallas as pl
from jax.experimental.pallas import tpu as pltpu

F32 = jnp.float32
BF16 = jnp.bfloat16
I32 = jnp.int32

D_MODEL = 1024
N_HEADS = 8
HEAD_DIM = 128
CONV_W = 3
N_GROUPS = 4
EXPERTS_PER_GROUP = 8
N_EXPERTS = N_GROUPS * EXPERTS_PER_GROUP
D_EXPERT = 256
EPS = 1e-6

LANES = 128
SUBLANES = 8
BF16_ROWS = 16
VMEM_LIMIT = 56 * 1024 * 1024

ROW_TILE = 256
FAST_ROW_TILE = 512
SCAN_CHUNK = 128
SCAN_STEP_CHUNKS = 2
SCAN_HEAD_UNROLL = 2
TAIL_ROWS = 256
SAMPLE_SEQS = 8
MOE_TILE = 256
MOE_SORTED_ROWS = 1024
MOE_CHUNKS = MOE_SORTED_ROWS // BF16_ROWS
MOE_BLOCK_CHUNKS = 32
GROUP_LANE0 = N_EXPERTS

_NN = (((1,), (0,)), ((), ()))
_NT = (((1,), (1,)), ((), ()))
_TN = (((0,), (0,)), ((), ()))


def _params(sem):
    return pltpu.CompilerParams(dimension_semantics=sem, vmem_limit_bytes=VMEM_LIMIT)


def _resident(shape):
    nd = len(shape)
    return pl.BlockSpec(shape, lambda *_: (0,) * nd, pipeline_mode=pl.Buffered(1))


def _rmsnorm(x, g):
    r = lax.rsqrt(jnp.mean(x * x, axis=-1, keepdims=True) + EPS)
    return (x * r) * g


def _dg(a, b, dims=_NN):
    return lax.dot_general(a, b, dims, preferred_element_type=F32)


def _split2(x):
    hi = x.astype(BF16)
    return hi, (x - hi.astype(F32)).astype(BF16)


def _mm(a, b, precise, dims=_NN):
    if not precise:
        return _dg(a.astype(BF16), b.astype(BF16), dims)
    ah, al = _split2(a.astype(F32))
    bh, bl = _split2(b.astype(F32))
    return _dg(ah, bh, dims) + (_dg(al, bh, dims) + _dg(ah, bl, dims))


def _mmw(a, w_hi, w_lo, precise):
    if not precise:
        return _dg(a.astype(BF16), w_hi)
    ah, al = _split2(a.astype(F32))
    return _dg(ah, w_hi) + (_dg(al, w_hi) + _dg(ah, w_lo))


def _split3(x):
    hi = x.astype(BF16)
    r1 = x - hi.astype(F32)
    mid = r1.astype(BF16)
    lo = (r1 - mid.astype(F32)).astype(BF16)
    return hi, mid, lo


def _dot_exact_lhs(m01, x):
    hi, mid, lo = _split3(x)
    return _dg(m01, hi) + _dg(m01, mid) + _dg(m01, lo)


def _either(precise_pred, body):
    if precise_pred is None:
        body(False)
    else:
        pl.when(precise_pred)(lambda: body(True))
        pl.when(jnp.logical_not(precise_pred))(lambda: body(False))


def _tail_tile(i, tiles_per_seq, tail_tiles, prompt_tiles):
    return (i >= prompt_tiles) | ((i % tiles_per_seq) >= tiles_per_seq - tail_tiles)


def _pair_specs(tm, cols, prompt_tiles):
    return [pl.BlockSpec((tm, cols), lambda i: (jnp.minimum(i, prompt_tiles - 1), 0)),
            pl.BlockSpec((tm, cols), lambda i: (jnp.maximum(i - prompt_tiles, 0), 0))]


def _pair_load(i, prompt_tiles, p_ref, s_ref):
    return jnp.where(i < prompt_tiles, p_ref[...], s_ref[...])


def _w_slice(ref, sl):
    return None if ref is None else ref[:, sl]


def _inproj_kernel(*refs, tail, prompt_tiles):
    if tail is None:
        xp_ref, xs_ref, g_ref, lb_ref, w_ref, qs_ref, lf_ref, kf_ref, v_ref, sog_ref, u_ref = refs
        wl_ref = None
    else:
        xp_ref, xs_ref, g_ref, lb_ref, w_ref, wl_ref, qs_ref, lf_ref, kf_ref, v_ref, sog_ref, u_ref = refs
    d = D_MODEL
    i = pl.program_id(0)

    def body(precise):
        h = _rmsnorm(_pair_load(i, prompt_tiles, xp_ref, xs_ref), g_ref[...])

        def seg(j):
            sl = slice(j * d, (j + 1) * d)
            return _mmw(h, w_ref[:, sl], _w_slice(wl_ref, sl), precise)

        q = seg(0)
        qs_ref[...] = q * jax.nn.sigmoid(q)
        fp = seg(1)
        lb = lb_ref[...]
        log_sig = jnp.minimum(fp, 0.0) - jnp.log1p(jnp.exp(-jnp.abs(fp)))
        a = jnp.log(lb)
        b = jnp.log1p(-lb) + log_sig
        lf_ref[...] = jnp.maximum(a, b) + jnp.log1p(jnp.exp(-jnp.abs(a - b)))
        kf_ref[...] = (1.0 - lb) * jax.nn.sigmoid(-fp)
        v_ref[...] = seg(2).astype(v_ref.dtype)
        og = seg(3)
        sog_ref[...] = (og * jax.nn.sigmoid(og)).astype(sog_ref.dtype)
        u_ref[...] = seg(4) * seg(5)

    _either(None if tail is None else _tail_tile(i, *tail), body)


def _inproj(xp, xs, g, lb, w, w_lo, tail, act_dtype, tm):
    d = D_MODEL
    assert xp.shape[0] % tm == 0 and xs.shape[0] % tm == 0 and xp.shape[1] == d
    pt = xp.shape[0] // tm
    t = xp.shape[0] + xs.shape[0]
    row = pl.BlockSpec((tm, d), lambda i: (i, 0))
    ws = [w] if tail is None else [w, w_lo]
    return pl.pallas_call(
        functools.partial(_inproj_kernel, tail=tail, prompt_tiles=pt),
        grid=(t // tm,),
        in_specs=_pair_specs(tm, d, pt) + [_resident((1, d)), _resident((1, d))] + [_resident(a.shape) for a in ws],
        out_specs=[row] * 6,
        out_shape=[jax.ShapeDtypeStruct((t, d), dt) for dt in (F32, F32, F32, act_dtype, act_dtype, F32)],
        compiler_params=_params(("parallel",)),
        name="inproj",
    )(xp, xs, g, lb, *ws)


def _score_masks(n_rows, block):
    xr = (lax.broadcasted_iota(I32, (n_rows, n_rows), 0)
          ^ lax.broadcasted_iota(I32, (n_rows, n_rows), 1))
    masks = {0: (xr == 0).astype(F32)}
    b = 1
    while 2 * b < n_rows and b < block:
        masks[b] = (xr < 2 * b).astype(F32)
        b *= 2
    return masks


def _level_operands(q, k, g, cum, n_rows, block):
    ri = lax.broadcasted_iota(I32, (n_rows, LANES), 0)
    b = 1
    while b < block:
        if b >= SUBLANES:
            shape3 = (n_rows // (2 * b), 2 * b, LANES)
            c3, q3, k3 = cum.reshape(shape3), q.reshape(shape3), k.reshape(shape3)
            mid = c3[:, b - 1:b, :]
            zero = jnp.zeros((shape3[0], b, LANES), F32)
            qt = jnp.concatenate([zero, q3[:, b:, :] * jnp.exp(c3[:, b:, :] - mid)], axis=1).reshape(n_rows, LANES)
            kt = jnp.concatenate([k3[:, :b, :] * jnp.exp(mid - c3[:, :b, :]), zero], axis=1).reshape(n_rows, LANES)
        else:
            upper = (ri & b) != 0
            if b == 1:
                z = jnp.where(upper, g, 0.0)
            elif b == 2:
                g_prev = pltpu.roll(g, 1, 0)
                g_next = pltpu.roll(g, n_rows - 1, 0)
                m4 = ri & 3
                z = jnp.where(m4 == 2, g, jnp.where(m4 == 3, g + g_prev, jnp.where(m4 == 0, g_next, 0.0)))
            else:
                c3 = cum.reshape(n_rows // (2 * b), 2 * b, LANES)
                mid = jnp.broadcast_to(c3[:, b - 1:b, :], c3.shape).reshape(n_rows, LANES)
                z = jnp.where(upper, cum - mid, mid - cum)
            e = jnp.exp(z)
            qt = jnp.where(upper, q * e, 0.0)
            kt = jnp.where(upper, 0.0, k * e)
        yield b, qt, kt
        b *= 2


def _intra_scores(q, k, g, cum, n_rows, block, precise, masks):
    s = masks[0] * jnp.sum(q * k, axis=1, keepdims=True)
    for b, qt, kt in _level_operands(q, k, g, cum, n_rows, block):
        prod = _mm(qt, kt, precise, _NT)
        s = s + (prod * masks[b] if b in masks else prod)
    return s


def _decay_columns(e_row):
    hi = e_row.astype(BF16).astype(F32)
    lo = e_row - hi
    r = lax.broadcasted_iota(I32, (BF16_ROWS, LANES), 0)
    stacked = jnp.where(r == 0, hi, jnp.where(r == 1, lo, 0.0)).astype(BF16)
    return _dg(stacked, jnp.ones((BF16_ROWS, LANES), BF16), _TN)


def _head_out(o, gh, sog, dtype):
    on = o * lax.rsqrt(jnp.mean(o * o, axis=-1, keepdims=True) + EPS) * gh
    return (on * sog.astype(F32)).astype(dtype)


def _scan_step_staged(qs_ref, lf_ref, kf_ref, v_ref, sog_ref, gh_ref, og_ref, s_scr,
                      cum_scr, qe_scr, ke_scr, qt_scr, kt_scr, sc_scr):
    n = SCAN_CHUNK
    rows = n * SCAN_STEP_CHUNKS
    shift = n.bit_length() - 1
    ri = lax.broadcasted_iota(I32, (rows, rows), 0)
    ci = lax.broadcasted_iota(I32, (rows, rows), 1)
    tri = (((ri >> shift) == (ci >> shift)) & (ri >= ci)).astype(BF16)
    cum_scr[...] = _dot_exact_lhs(tri, lf_ref[...])
    masks = _score_masks(n, n)
    chunks = [slice(j * n, (j + 1) * n) for j in range(SCAN_STEP_CHUNKS)]

    def head_lanes(h):
        return slice(h * HEAD_DIM, (h + 1) * HEAD_DIM)

    def operands(h):
        hs = head_lanes(h)
        for rs in chunks:
            q, g, k, cum = qs_ref[rs, hs], lf_ref[rs, hs], kf_ref[rs, hs], cum_scr[rs, hs]
            qe_scr[rs, hs] = (q * jnp.exp(cum)).astype(BF16)
            ke_scr[rs, hs] = (k * jnp.exp(cum[n - 1:n, :] - cum)).astype(BF16)
            yield
            for level, (_, qt, kt) in enumerate(_level_operands(q, k, g, cum, n, n)):
                qt_scr[level, rs, hs] = qt.astype(BF16)
                kt_scr[level, rs, hs] = kt.astype(BF16)
                yield

    def matmuls(h):
        hs = head_lanes(h)
        for rs in chunks:
            s = masks[0] * jnp.sum(qs_ref[rs, hs] * kf_ref[rs, hs], axis=1, keepdims=True)
            b = 1
            for level in range(qt_scr.shape[0]):
                prod = _dg(qt_scr[level, rs, hs], kt_scr[level, rs, hs], _NT)
                s = s + (prod * masks[b] if b in masks else prod)
                b *= 2
                yield
            sc_scr[h, rs, :] = s.astype(BF16)
        s_old = s_scr[h]
        for rs in chunks:
            v = v_ref[rs, hs].astype(BF16)
            o = _dg(qe_scr[rs, hs], s_old.astype(BF16)) + _dg(sc_scr[h, rs, :], v)
            yield
            tot = cum_scr[rs.stop - 1:rs.stop, hs]
            s_old = _decay_columns(jnp.exp(tot)) * s_old + _dg(ke_scr[rs, hs], v, _TN)
            yield
            og_ref[rs, hs] = _head_out(o, gh_ref[:, hs], sog_ref[rs, hs], og_ref.dtype)
            yield
        s_scr[h] = s_old

    for h in range(N_HEADS + 1):
        phases = ([operands(h)] if h < N_HEADS else []) + ([matmuls(h - 1)] if h > 0 else [])
        while phases:
            for phase in list(phases):
                if next(phase, phases) is phases:
                    phases.remove(phase)


def _scan_prompt_kernel(qs_ref, lf_ref, kf_ref, v_ref, sog_ref, gh_ref, og_ref, sfin_ref, zeros_ref, s_scr,
                        *stage_scr, tail_steps):
    c = pl.program_id(1)
    nc = pl.num_programs(1)
    n = SCAN_CHUNK
    zeros_ref[...] = jnp.zeros_like(zeros_ref)

    @pl.when(c == 0)
    def _():
        s_scr[...] = jnp.zeros_like(s_scr)

    def body(precise):
        if not precise:
            _scan_step_staged(qs_ref, lf_ref, kf_ref, v_ref, sog_ref, gh_ref, og_ref, s_scr, *stage_scr)
            return
        tri = (lax.broadcasted_iota(I32, (n, n), 0) >= lax.broadcasted_iota(I32, (n, n), 1)).astype(BF16)
        masks = _score_masks(n, n)

        def head(h, carry):
            hs = pl.ds(pl.multiple_of(h * HEAD_DIM, HEAD_DIM), HEAD_DIM)
            s_old = s_scr[h]
            for j in range(SCAN_STEP_CHUNKS):
                rs = slice(j * n, (j + 1) * n)
                q, g, k, v = qs_ref[rs, hs], lf_ref[rs, hs], kf_ref[rs, hs], v_ref[rs, hs]
                cum = _dot_exact_lhs(tri, g)
                tot = cum[n - 1:n, :]
                o = _mm(q * jnp.exp(cum), s_old, precise)
                o = o + _mm(_intra_scores(q, k, g, cum, n, n, precise, masks), v, precise)
                ke = k * jnp.exp(tot - cum)
                s_old = _decay_columns(jnp.exp(tot)) * s_old + _mm(ke, v, precise, _TN)
                og_ref[rs, hs] = _head_out(o, gh_ref[:, hs], sog_ref[rs, hs], og_ref.dtype)
            s_scr[h] = s_old
            return carry

        lax.fori_loop(0, N_HEADS, head, 0)

    _either(None if tail_steps is None else c >= nc - tail_steps, body)

    @pl.when(c == nc - 1)
    def _():
        sfin_ref[0] = s_scr[...]


def _scan_prompt(qs, lf, kf, v, sog, gh, batch, seq, tail_rows):
    t, d = qs.shape
    n = SCAN_CHUNK * SCAN_STEP_CHUNKS
    assert seq % n == 0 and (tail_rows is None or tail_rows % n == 0)
    nc = seq // n
    levels = SCAN_CHUNK.bit_length() - 1
    row = pl.BlockSpec((n, d), lambda b, c: (b * nc + c, 0))
    sorted_rows = (t // MOE_TILE) * MOE_SORTED_ROWS
    zrows = sorted_rows // (batch * nc)
    assert zrows * batch * nc == sorted_rows and zrows % BF16_ROWS == 0
    return pl.pallas_call(
        functools.partial(_scan_prompt_kernel, tail_steps=None if tail_rows is None else tail_rows // n),
        grid=(batch, nc),
        in_specs=[row, row, row, row, row, _resident((1, d))],
        out_specs=[row, pl.BlockSpec((1, N_HEADS, HEAD_DIM, HEAD_DIM), lambda b, c: (b, 0, 0, 0)),
                   pl.BlockSpec((zrows, d), lambda b, c: (b * nc + c, 0))],
        out_shape=[jax.ShapeDtypeStruct((batch * seq, d), v.dtype),
                   jax.ShapeDtypeStruct((batch, N_HEADS, HEAD_DIM, HEAD_DIM), F32),
                   jax.ShapeDtypeStruct((sorted_rows, d), BF16)],
        scratch_shapes=[pltpu.VMEM((N_HEADS, HEAD_DIM, HEAD_DIM), F32),
                        pltpu.VMEM((n, d), F32),
                        pltpu.VMEM((n, d), BF16), pltpu.VMEM((n, d), BF16),
                        pltpu.VMEM((levels, n, d), BF16), pltpu.VMEM((levels, n, d), BF16),
                        pltpu.VMEM((N_HEADS, n, SCAN_CHUNK), BF16)],
        compiler_params=_params(("parallel", "arbitrary")),
        name="scan_prompt",
    )(qs, lf, kf, v, sog, gh)


def _scan_sample_kernel(*refs, steps, precise, layer, chained):
    if chained:
        qs_ref, lf_ref, kf_ref, v_ref, sog_ref, gh_ref, s_ref, _, og_ref, snew_ref = refs
    else:
        qs_ref, lf_ref, kf_ref, v_ref, sog_ref, gh_ref, s_ref, og_ref, all_ref = refs
        for other in range(all_ref.shape[0]):
            if other != layer:
                all_ref[other] = jnp.zeros(all_ref.shape[1:], F32)
        snew_ref = all_ref.at[layer]
    n = SAMPLE_SEQS * steps
    shift = steps.bit_length() - 1
    ri = lax.broadcasted_iota(I32, (n, n), 0)
    ci = lax.broadcasted_iota(I32, (n, n), 1)
    same = (ri >> shift) == (ci >> shift)
    tri = (same & (ri >= ci)).astype(BF16)
    tri_after = (same & (ci > ri)).astype(BF16)
    seq_of_row = lax.broadcasted_iota(I32, (n, LANES), 0) >> shift
    masks = _score_masks(n, steps)

    def head(h, carry):
        hs = pl.ds(pl.multiple_of(h * HEAD_DIM, HEAD_DIM), HEAD_DIM)
        q, g, k, v = qs_ref[:, hs], lf_ref[:, hs], kf_ref[:, hs], v_ref[:, hs]
        cum = _dot_exact_lhs(tri, g)
        after = _dot_exact_lhs(tri_after, g)
        o = _mm(_intra_scores(q, k, g, cum, n, steps, precise, masks), v, precise)
        qe = q * jnp.exp(cum)
        ke = k * jnp.exp(after)
        for s in range(SAMPLE_SEQS):
            mine = seq_of_row == s
            s_old = s_ref[s, h]
            o = o + jnp.where(mine, _mm(qe, s_old, precise), 0.0)
            kv = _mm(jnp.where(mine, ke, 0.0), v, precise, _TN)
            last = (s + 1) * steps - 1
            snew_ref[s, h] = _decay_columns(jnp.exp(cum[last:last + 1, :])) * s_old + kv
        og_ref[:, hs] = _head_out(o, gh_ref[:, hs], sog_ref[:, hs], og_ref.dtype)
        return carry

    lax.fori_loop(0, N_HEADS, head, 0)


def _scan_sample(qs, lf, kf, v, sog, gh, states, layer, new_states, row0, steps, precise):
    t, d = qs.shape
    depth, nseq = states.shape[:2]
    assert steps & (steps - 1) == 0 and steps >= 4, "sample block must be a power of two >= 4"
    n = SAMPLE_SEQS * steps
    assert nseq % SAMPLE_SEQS == 0 and row0 % n == 0 and n % BF16_ROWS == 0
    b0 = row0 // n
    chained = new_states is not None
    row = pl.BlockSpec((n, d), lambda i: (b0 + i, 0))
    tail = (SAMPLE_SEQS, N_HEADS, HEAD_DIM, HEAD_DIM)
    st = pl.BlockSpec((None,) + tail, lambda i: (layer, i, 0, 0, 0))
    st_all = pl.BlockSpec((depth,) + tail, lambda i: (0, i, 0, 0, 0))
    extra = [new_states] if chained else []
    return pl.pallas_call(
        functools.partial(_scan_sample_kernel, steps=steps, precise=precise, layer=layer, chained=chained),
        grid=(nseq // SAMPLE_SEQS,),
        in_specs=[row, row, row, row, row, _resident((1, d)), st] + [pl.BlockSpec(memory_space=pl.ANY)] * len(extra),
        out_specs=[pl.BlockSpec((n, d), lambda i: (i, 0)), st if chained else st_all],
        out_shape=[jax.ShapeDtypeStruct((nseq * steps, d), v.dtype), jax.ShapeDtypeStruct(states.shape, F32)],
        input_output_aliases={7: 1} if chained else {},
        compiler_params=_params(("parallel",)),
        name="scan_sample",
    )(qs, lf, kf, v, sog, gh, states, *extra)


def _mix_kernel(*refs, tm, tiles_per_seq, prompt_tiles, steps, tail):
    if tail is None:
        (xp_ref, xs_ref, ogp_ref, ogs_ref, u_ref, halo_ref, e1_ref, e2_ref, g_ref, wc_ref, gf_ref, wr_ref, br_ref,
         w_cbg_ref, wa_ref, wb_ref, wo_ref, x1_ref, sorted_ref, rinfo_ref, cinfo_ref, prev_scr) = refs
        l_cbg_ref = la_ref = lb_ref = lo_ref = None
    else:
        (xp_ref, xs_ref, ogp_ref, ogs_ref, u_ref, halo_ref, e1_ref, e2_ref, g_ref, wc_ref, gf_ref, wr_ref, br_ref,
         w_cbg_ref, wa_ref, wb_ref, wo_ref, l_cbg_ref, la_ref, lb_ref, lo_ref,
         x1_ref, sorted_ref, rinfo_ref, cinfo_ref, prev_scr) = refs
    d = D_MODEL
    i = pl.program_id(0)

    @pl.when(i == 0)
    def _():
        prev_scr[...] = jnp.zeros_like(prev_scr)

    def full(ref):
        return None if ref is None else ref[...]

    def body(precise):
        x = _pair_load(i, prompt_tiles, xp_ref, xs_ref)
        h = _rmsnorm(x, g_ref[...])
        u = u_ref[...]
        ri = lax.broadcasted_iota(I32, (tm, d), 0)
        r1 = pltpu.roll(u, 1, 0)
        r2 = pltpu.roll(u, 2, 0)
        keep = jnp.where(i % tiles_per_seq == 0, 0.0, 1.0)
        halo = halo_ref[...]
        h_last = halo[7:8, :] * keep
        h_prev = halo[6:7, :] * keep
        is_sample = i >= prompt_tiles
        t_in_seq = ri & (steps - 1)
        prev1 = jnp.where(is_sample,
                          jnp.where(t_in_seq == 0, e1_ref[...], r1),
                          jnp.where(ri == 0, h_last, r1))
        prev2 = jnp.where(is_sample,
                          jnp.where(t_in_seq < 2, e2_ref[...], r2),
                          jnp.where(ri == 0, h_prev, jnp.where(ri == 1, h_last, r2)))
        wc = wc_ref[...]
        conv = prev2 * wc[0:1, :] + prev1 * wc[1:2, :] + u * wc[2:3, :]

        def seg(j):
            sl = slice(j * d, (j + 1) * d)
            return _mmw(h, w_cbg_ref[:, sl], _w_slice(l_cbg_ref, sl), precise)

        router = _route_stages(prev_scr[...], gf_ref[...], wr_ref[...], br_ref[...],
                               sorted_ref, rinfo_ref, cinfo_ref)
        cb = seg(0)
        next(router)
        y_b = _mmw(cb * conv, wb_ref[...], full(lb_ref), precise)
        next(router)
        y_a = _mmw(_pair_load(i, prompt_tiles, ogp_ref, ogs_ref), wa_ref[...], full(la_ref), precise)
        next(router)
        ga = seg(1)
        next(router)
        gb = seg(2)
        next(router)
        m = jax.nn.sigmoid(ga) * y_a + jax.nn.sigmoid(gb) * y_b
        x1 = x + _mmw(m, wo_ref[...], full(lo_ref), precise)
        for _ in router:
            pass
        x1_ref[...] = x1
        prev_scr[...] = x1

    _either(None if tail is None else _tail_tile(i, *tail), body)


def _mix(xp, xs, ogp, ogs, u, e1, e2, g, wc, g_ffn, wr, br, ws, ws_lo, seq, steps, tail):
    t, d = u.shape
    tm = MOE_TILE
    prompt_rows = xp.shape[0]
    assert seq % tm == 0 and prompt_rows % tm == 0 and (t - prompt_rows) % tm == 0 and tm % steps == 0
    pt = prompt_rows // tm
    nt = t // tm
    cur = lambda i: jnp.minimum(i, nt - 1)
    prev = lambda i: jnp.maximum(i - 1, 0)
    prompt = pl.BlockSpec((tm, d), lambda i: (jnp.minimum(i, pt - 1), 0))
    sample = pl.BlockSpec((tm, d), lambda i: (jnp.clip(i - pt, 0, nt - pt - 1), 0))
    row = pl.BlockSpec((tm, d), lambda i: (cur(i), 0))
    halo = pl.BlockSpec((SUBLANES, d), lambda i: (jnp.maximum(cur(i) * (tm // SUBLANES) - 1, 0), 0))
    weights = list(ws) + ([] if tail is None else list(ws_lo))
    return pl.pallas_call(
        functools.partial(_mix_kernel, tm=tm, tiles_per_seq=seq // tm, prompt_tiles=pt, steps=steps, tail=tail),
        grid=(nt + 1,),
        in_specs=[prompt, sample, prompt, sample, row, halo, sample, sample, _resident((1, d)), _resident(wc.shape),
                  _resident((1, d)), _resident(wr.shape), _resident(br.shape)]
        + [_resident(a.shape) for a in weights],
        out_specs=[row,
                   pl.BlockSpec((1, MOE_SORTED_ROWS, d), lambda i: (prev(i), 0, 0)),
                   pl.BlockSpec((tm, LANES), lambda i: (prev(i), 0)),
                   pl.BlockSpec((1, SUBLANES, LANES), lambda i: (prev(i), 0, 0))],
        out_shape=[jax.ShapeDtypeStruct((t, d), F32),
                   jax.ShapeDtypeStruct((nt, MOE_SORTED_ROWS, d), BF16),
                   jax.ShapeDtypeStruct((t, LANES), F32),
                   jax.ShapeDtypeStruct((nt, SUBLANES, LANES), I32)],
        scratch_shapes=[pltpu.VMEM((tm, d), F32)],
        compiler_params=_params(("arbitrary",)),
        name="mix_route",
    )(xp, xs, ogp, ogs, u, u, e1, e2, g, wc, g_ffn, wr, br, *weights)


def _route_stages(x, g, wr, br, xs_ref, rinfo_ref, cinfo_ref):
    tm = MOE_TILE
    rows = MOE_SORTED_ROWS
    hn = _rmsnorm(x, g)
    hn_hi, hn_lo = _split2(hn)
    wr_hi, wr_lo = _split2(wr)
    logits = _dg(hn_hi, wr_hi) + _dg(hn_lo, wr_hi) + _dg(hn_hi, wr_lo) + br
    yield

    lane = lax.broadcasted_iota(I32, (tm, LANES), 1)
    neg = -jnp.inf
    big = jnp.int32(LANES)

    def first_lane(mask):
        return jnp.min(jnp.where(mask, lane, big), axis=1, keepdims=True)

    is_group = (lane >= GROUP_LANE0) & (lane < GROUP_LANE0 + N_GROUPS)
    gl = jnp.where(is_group, logits, neg)
    gmax = jnp.max(gl, axis=1, keepdims=True)
    g_idx = first_lane(gl == gmax) - GROUP_LANE0
    g_w = 1.0 / jnp.sum(jnp.exp(gl - gmax), axis=1, keepdims=True)
    yield

    in_group = (lane < N_EXPERTS) & ((lane >> (EXPERTS_PER_GROUP.bit_length() - 1)) == g_idx)
    el = jnp.where(in_group, logits, neg)
    emax = jnp.max(el, axis=1, keepdims=True)
    ee = jnp.exp(el - emax)
    prob = ee / jnp.sum(ee, axis=1, keepdims=True)
    prob = jnp.where(in_group, prob, -1.0)
    yield
    p1 = jnp.max(prob, axis=1, keepdims=True)
    i1 = first_lane(prob == p1)
    prob2 = jnp.where(lane == i1, -1.0, prob)
    p2 = jnp.max(prob2, axis=1, keepdims=True)
    i2 = first_lane(prob2 == p2)
    psum = p1 + p2
    w1 = p1 / psum * g_w
    w2 = p2 / psum * g_w
    yield

    oh1 = (lane == i1).astype(BF16)
    oh2 = (lane == i2).astype(BF16)
    before = (lax.broadcasted_iota(I32, (tm, tm), 1) < lax.broadcasted_iota(I32, (tm, tm), 0)).astype(BF16)
    c1 = _dg(before, oh1)
    c2 = _dg(before, oh2)
    oh1f = oh1.astype(F32)
    oh2f = oh2.astype(F32)
    cnt1 = jnp.sum(oh1f, axis=0, keepdims=True)
    cnt2 = jnp.sum(oh2f, axis=0, keepdims=True)
    chunks = jnp.floor((cnt1 + cnt2 + (BF16_ROWS - 1)) * (1.0 / BF16_ROWS))
    excl = (lax.broadcasted_iota(I32, (LANES, LANES), 0) < lax.broadcasted_iota(I32, (LANES, LANES), 1)).astype(BF16)
    chunk0 = _dg(jnp.broadcast_to(chunks, (BF16_ROWS, LANES)).astype(BF16), excl)[0:1, :]
    yield
    base = chunk0 * BF16_ROWS
    pos1 = jnp.sum(oh1f * (base + c1), axis=1, keepdims=True)
    pos2 = jnp.sum(oh2f * (base + cnt1 + c2), axis=1, keepdims=True)

    slot = lax.broadcasted_iota(I32, (tm, rows), 1)
    place = ((slot == pos1.astype(I32)) | (slot == pos2.astype(I32))).astype(BF16)
    xs_ref[0] = _dg(place, hn_hi, _TN).astype(BF16)

    rinfo_ref[...] = jnp.where(lane == 0, pos1, jnp.where(lane == 1, pos2, jnp.where(lane == 2, w1, jnp.where(lane == 3, w2, 0.0))))
    r8 = lax.broadcasted_iota(I32, (SUBLANES, LANES), 0)
    cinfo_ref[0] = jnp.where(r8 == 0, chunk0, jnp.where(r8 == 1, chunks, 0.0)).astype(I32)


def _ffn_kernel(order_ref, first_ref, left_ref, blk0_ref, xs_hbm, wg_hbm, wu_hbm, wd_hbm, ys_in_hbm, ys_hbm,
                lhs_buf, res_buf, wg_stage, wu_stage, wd_stage, wgu_bf, wd_bf, in_sem, out_sem, w_sem, *, layer):
    del ys_in_hbm
    nb = MOE_BLOCK_CHUNKS
    rows = BF16_ROWS
    nblocks = blk0_ref[N_EXPERTS]

    def chunk_id(blk, i):
        return order_ref[first_ref[blk] + jnp.where(i < left_ref[blk], i, 0)]

    def chunk_copy_in(blk, slot, i):
        return pltpu.make_async_copy(xs_hbm.at[chunk_id(blk, i)],
                                     lhs_buf.at[slot, pl.ds(i * rows, rows)], in_sem.at[slot])

    def chunk_copy_out(blk, slot, i):
        return pltpu.make_async_copy(res_buf.at[slot, pl.ds(i * rows, rows)],
                                     ys_hbm.at[chunk_id(blk, i)], out_sem.at[slot])

    def gather_start(blk, slot):
        for i in range(nb):
            chunk_copy_in(blk, slot, i).start()

    def gather_wait(blk, slot):
        for i in range(nb):
            chunk_copy_in(blk, slot, i).wait()

    def scatter(blk, slot, start):
        n = left_ref[blk]
        for i in range(nb):
            @pl.when(i < n)
            def _():
                cp = chunk_copy_out(blk, slot, i)
                cp.start() if start else cp.wait()

    def weight_copies(e, slot):
        return [pltpu.make_async_copy(src.at[layer, e], dst.at[slot], w_sem.at[slot])
                for src, dst in ((wg_hbm, wg_stage), (wu_hbm, wu_stage), (wd_hbm, wd_stage))]

    for cp in weight_copies(0, 0):
        cp.start()

    @pl.when(nblocks > 0)
    def _():
        gather_start(0, 0)

    def expert(e, carry):
        ws = e % 2
        for cp in weight_copies(e, ws):
            cp.wait()
        wgu_bf[:, :D_EXPERT] = wg_stage[ws].astype(BF16)
        wgu_bf[:, D_EXPERT:] = wu_stage[ws].astype(BF16)
        wd_bf[...] = wd_stage[ws].astype(BF16)

        @pl.when(e + 1 < N_EXPERTS)
        def _():
            for cp in weight_copies(e + 1, 1 - ws):
                cp.start()

        def block(blk, c):
            slot = blk % 2

            @pl.when(blk + 1 < nblocks)
            def _():
                gather_start(blk + 1, 1 - slot)

            gather_wait(blk, slot)

            @pl.when(blk >= 2)
            def _():
                scatter(blk - 2, slot, start=False)

            gu = _dg(lhs_buf[slot], wgu_bf[...])
            gate = gu[:, :D_EXPERT]
            hid = gate * jax.nn.sigmoid(gate) * gu[:, D_EXPERT:]
            res_buf[slot] = _dg(hid.astype(BF16), wd_bf[...]).astype(BF16)
            scatter(blk, slot, start=True)
            return c

        lax.fori_loop(blk0_ref[e], blk0_ref[e + 1], block, 0)
        return carry

    lax.fori_loop(0, N_EXPERTS, expert, 0)

    for back in (2, 1):
        @pl.when(nblocks >= back)
        def _():
            blk = nblocks - back
            scatter(blk, blk % 2, start=False)


def _ffn(xs, zeros, order, first, left, blk0, w_gate, w_up, w_down, layer):
    nt, srows, d = xs.shape
    chunked = (nt * srows // BF16_ROWS, BF16_ROWS, d)
    brows = MOE_BLOCK_CHUNKS * BF16_ROWS
    anywhere = pl.BlockSpec(memory_space=pl.ANY)
    f = w_gate.shape[-1]
    ys = pl.pallas_call(
        functools.partial(_ffn_kernel, layer=layer),
        grid_spec=pltpu.PrefetchScalarGridSpec(
            num_scalar_prefetch=4,
            grid=(1,),
            in_specs=[anywhere] * 5,
            out_specs=anywhere,
            scratch_shapes=[pltpu.VMEM((2, brows, d), BF16), pltpu.VMEM((2, brows, d), BF16),
                            pltpu.VMEM((2, d, f), F32), pltpu.VMEM((2, d, f), F32), pltpu.VMEM((2, f, d), F32),
                            pltpu.VMEM((d, 2 * f), BF16), pltpu.VMEM((f, d), BF16),
                            pltpu.SemaphoreType.DMA((2,)), pltpu.SemaphoreType.DMA((2,)),
                            pltpu.SemaphoreType.DMA((2,))],
        ),
        out_shape=jax.ShapeDtypeStruct(chunked, BF16),
        input_output_aliases={8: 0},
        compiler_params=_params(("arbitrary",)),
        name="expert_ffn",
    )(order, first, left, blk0, xs.reshape(chunked), w_gate, w_up, w_down, zeros.reshape(chunked))
    return ys.reshape(nt, srows, d)


def _block_tables(cinfo):
    nt = cinfo.shape[0]
    nb = MOE_BLOCK_CHUNKS
    nchunks = nt * MOE_CHUNKS
    chunk0 = cinfo[:, 0, :N_EXPERTS]
    chunks = cinfo[:, 1, :N_EXPERTS]
    ends = chunk0 + chunks
    c = jnp.arange(MOE_CHUNKS, dtype=I32)
    label = jnp.sum((c[None, :, None] >= ends[:, None, :]).astype(I32), axis=-1)
    order = jnp.argsort(label.reshape(-1), stable=True).astype(I32)
    cnt = chunks.sum(axis=0).astype(I32)
    cstart = jnp.cumsum(cnt) - cnt
    nblk = (cnt + nb - 1) // nb
    blk0 = jnp.concatenate([jnp.zeros((1,), I32), jnp.cumsum(nblk).astype(I32)])
    max_blocks = nchunks // nb + N_EXPERTS
    j = jnp.arange(max_blocks, dtype=I32)
    ej = jnp.minimum(jnp.sum((j[:, None] >= blk0[None, 1:]).astype(I32), axis=1), N_EXPERTS - 1)
    mine = (ej[:, None] == jnp.arange(N_EXPERTS, dtype=I32)[None, :]).astype(I32)
    local = j - jnp.sum(mine * blk0[None, :N_EXPERTS], axis=1)
    first = jnp.sum(mine * cstart[None, :], axis=1) + local * nb
    left = jnp.sum(mine * cnt[None, :], axis=1) - local * nb
    return order, jnp.clip(first, 0, nchunks - 1).astype(I32), jnp.maximum(left, 0).astype(I32), blk0


def _ple_kernel(x1_ref, ys_ref, rinfo_ref, pp_ref, ps_ref, g_ref, wg_ref, wp_ref, gfin_ref, outp_ref, outs_ref, *,
                final, prompt_tiles):
    tm = MOE_TILE
    rows = MOE_SORTED_ROWS
    i = pl.program_id(0)
    rinfo = rinfo_ref[...]
    pos1 = rinfo[:, 0:1].astype(I32)
    pos2 = rinfo[:, 1:2].astype(I32)
    w1 = rinfo[:, 2:3]
    w2 = rinfo[:, 3:4]
    slot = lax.broadcasted_iota(I32, (tm, rows), 1)
    ys = ys_ref[0]
    y1 = _dg((slot == pos1).astype(BF16), ys)
    y2 = _dg((slot == pos2).astype(BF16), ys)
    x = x1_ref[...] + (w1 * y1 + w2 * y2)
    gate = jax.nn.sigmoid(_dg(_rmsnorm(x, g_ref[...]).astype(BF16), wg_ref[...]))
    p = _pair_load(i, prompt_tiles, pp_ref, ps_ref)
    x = x + gate * _dg(p.astype(BF16), wp_ref[...])
    if final:
        x = _rmsnorm(x, gfin_ref[...])

    @pl.when(i < prompt_tiles)
    def _():
        outp_ref[...] = x

    @pl.when(i >= prompt_tiles)
    def _():
        outs_ref[...] = x


def _ple(x1, ys, rinfo, p_prompt, p_sample, layer, g, wg, wp, gfin, final):
    t, d = x1.shape
    tm = MOE_TILE
    nt = t // tm
    tp, ts = p_prompt.shape[1], p_sample.shape[1]
    assert tp % tm == 0 and ts % tm == 0 and tp + ts == t
    pt = tp // tm
    pd = p_prompt.shape[2]
    ys = ys.reshape(-1, MOE_SORTED_ROWS, d)
    return pl.pallas_call(
        functools.partial(_ple_kernel, final=final, prompt_tiles=pt),
        grid=(nt,),
        in_specs=[pl.BlockSpec((tm, d), lambda i: (i, 0)),
                  pl.BlockSpec((1, MOE_SORTED_ROWS, d), lambda i: (i, 0, 0)),
                  pl.BlockSpec((tm, LANES), lambda i: (i, 0)),
                  pl.BlockSpec((None, tm, pd), lambda i: (layer, jnp.minimum(i, pt - 1), 0)),
                  pl.BlockSpec((None, tm, pd), lambda i: (layer, jnp.maximum(i - pt, 0), 0)),
                  _resident((1, d)), _resident(wg.shape), _resident(wp.shape), _resident((1, d))],
        out_specs=_pair_specs(tm, d, pt),
        out_shape=[jax.ShapeDtypeStruct((tp, d), F32), jax.ShapeDtypeStruct((ts, d), F32)],
        compiler_params=_params(("arbitrary",)),
        name="ple",
    )(x1, ys, rinfo, p_prompt, p_sample, g, wg, wp, gfin)


def _lower_bounds(lb_param):
    s = jax.nn.softmax(lb_param.astype(F32), axis=0)
    c = jnp.cumsum(s, axis=0)
    return c - c[0:1]


def _split_kernel(w_ref, hi_ref, lo_ref):
    hi_ref[...], lo_ref[...] = _split2(w_ref[...])


def _hi_lo(w):
    k, n = w.shape
    tk = ROW_TILE
    assert k % tk == 0
    spec = pl.BlockSpec((tk, n), lambda i: (i, 0))
    return pl.pallas_call(
        _split_kernel,
        grid=(k // tk,),
        in_specs=[spec],
        out_specs=[spec, spec],
        out_shape=[jax.ShapeDtypeStruct((k, n), BF16)] * 2,
        compiler_params=_params(("parallel",)),
        name="split_weight",
    )(w)


def kernel(x_prompt, x_sample, state_hgrn, state_conv, p_prompt, p_sample, g_mix, w_in, hg_lower, g_hg_out, w_br_a, w_conv, w_br_b, w_out, g_ffn, w_router_group, b_router_group, w_router_expert, b_router_expert, w_gate, w_up, w_down, g_ple, w_ple_gate, w_ple_proj, g_final):
    depth = w_in.shape[0]
    bp, seq, d = x_prompt.shape
    bs, steps, _ = x_sample.shape
    assert d == D_MODEL and w_conv.shape[1] == CONV_W
    assert TAIL_ROWS % ROW_TILE == 0 and seq >= TAIL_ROWS
    tp = bp * seq
    ts = bs * steps
    hw = N_HEADS * HEAD_DIM

    xp, xs_rows = x_prompt.reshape(tp, d), x_sample.reshape(ts, d)
    pp = p_prompt.reshape(depth, tp, -1)
    ps = p_sample.reshape(depth, ts, -1)
    lbs = _lower_bounds(hg_lower)
    row = lambda a: a.reshape(1, -1)

    hg_p, cv_p, cv_s = [], [], []
    hg_s = None
    for li in range(depth):
        careful = li < depth - 1
        tm = ROW_TILE if careful else FAST_ROW_TILE
        tail = (seq // tm, TAIL_ROWS // tm, tp // tm) if careful else None
        act = F32 if careful else BF16
        wi = w_in[li]
        w_scan = jnp.concatenate([wi[:, :4 * hw], wi[:, 4 * hw + d:4 * hw + 3 * d]], axis=1)
        w_cbg = jnp.concatenate([wi[:, 4 * hw:4 * hw + d], wi[:, 4 * hw + 3 * d:]], axis=1)
        mix_w = (w_cbg, w_br_a[li], w_br_b[li], w_out[li])
        if careful:
            w_scan, w_scan_lo = _hi_lo(w_scan)
            mix_w, mix_w_lo = zip(*[_hi_lo(w) for w in mix_w])
        else:
            w_scan, w_scan_lo = w_scan.astype(BF16), None
            mix_w, mix_w_lo = [w.astype(BF16) for w in mix_w], None

        qs, lf, kf, v, sog, u = _inproj(xp, xs_rows, row(g_mix[li]), row(lbs[li]), w_scan, w_scan_lo, tail, act, tm)

        gh = row(g_hg_out[li])
        ogp, s_p, zeros = _scan_prompt(qs, lf, kf, v, sog, gh, bp, seq, TAIL_ROWS if careful else None)
        ogs, hg_s = _scan_sample(qs, lf, kf, v, sog, gh, state_hgrn, li, hg_s, tp, steps, careful)
        hg_p.append(s_p)
        keep = CONV_W - 1
        cv_p.append(jnp.stack([lax.slice_in_dim(u, (b + 1) * seq - keep, (b + 1) * seq) for b in range(bp)]))
        cv_s.append(jnp.concatenate([state_conv[li], lax.slice_in_dim(u, tp, tp + ts).reshape(bs, steps, d)],
                                    axis=1)[:, steps:])

        buf = state_conv[li]
        zero = jnp.zeros((bs, steps - 1, d), F32)
        e1 = jnp.concatenate([buf[:, 1:2], zero], axis=1).reshape(ts, d)
        e2 = jnp.concatenate([buf, zero[:, 1:]], axis=1).reshape(ts, d)
        wr = jnp.zeros((d, LANES), F32).at[:, :N_EXPERTS].set(w_router_expert[li])
        wr = wr.at[:, GROUP_LANE0:GROUP_LANE0 + N_GROUPS].set(w_router_group[li])
        br = jnp.zeros((1, LANES), F32).at[0, :N_EXPERTS].set(b_router_expert[li])
        br = br.at[0, GROUP_LANE0:GROUP_LANE0 + N_GROUPS].set(b_router_group[li])
        mix_tail = (seq // MOE_TILE, TAIL_ROWS // MOE_TILE, tp // MOE_TILE) if careful else None
        x1, xsort, rinfo, cinfo = _mix(xp, xs_rows, ogp, ogs, u, e1, e2, row(g_mix[li]), w_conv[li],
                                       row(g_ffn[li]), wr, br, mix_w, mix_w_lo, seq, steps, mix_tail)
        ys = _ffn(xsort, zeros.reshape(xsort.shape), *_block_tables(cinfo), w_gate, w_up, w_down, li)

        xp, xs_rows = _ple(x1, ys, rinfo, pp, ps, li, row(g_ple[li]), w_ple_gate[li].astype(BF16),
                           w_ple_proj[li].astype(BF16), row(g_final), li == depth - 1)

    return (xp.reshape(bp, seq, d), xs_rows.reshape(bs, steps, d), jnp.stack(hg_p), jnp.stack(cv_p), hg_s,
            jnp.stack(cv_s))
```

```python
import functools

import jax
import jax.numpy as jnp
from jax import lax
from jax.experimental import pallas as pl
from jax.experimental.pallas import tpu as pltpu

F32 = jnp.float32
BF16 = jnp.bfloat16
I32 = jnp.int32

D_MODEL = 1024
N_HEADS = 8
HEAD_DIM = 128
CONV_W = 3
N_GROUPS = 4
EXPERTS_PER_GROUP = 8
N_EXPERTS = N_GROUPS * EXPERTS_PER_GROUP
D_EXPERT = 256
EPS = 1e-6

LANES = 128
SUBLANES = 8
BF16_ROWS = 16
VMEM_LIMIT = 56 * 1024 * 1024

ROW_TILE = 256
FAST_ROW_TILE = 512
SCAN_CHUNK = 128
SCAN_STEP_CHUNKS = 2
SCAN_HEAD_UNROLL = 2
TAIL_ROWS = 256
SAMPLE_SEQS = 8
MOE_TILE = 256
MOE_SORTED_ROWS = 1024
MOE_CHUNKS = MOE_SORTED_ROWS // BF16_ROWS
MOE_BLOCK_CHUNKS = 32
FFN_GATHER_DEPTH = 4
GROUP_LANE0 = N_EXPERTS

_NN = (((1,), (0,)), ((), ()))
_NT = (((1,), (1,)), ((), ()))
_TN = (((0,), (0,)), ((), ()))


def _params(sem):
    return pltpu.CompilerParams(dimension_semantics=sem, vmem_limit_bytes=VMEM_LIMIT)


def _resident(shape):
    nd = len(shape)
    return pl.BlockSpec(shape, lambda *_: (0,) * nd, pipeline_mode=pl.Buffered(1))


def _rmsnorm(x, g):
    r = lax.rsqrt(jnp.mean(x * x, axis=-1, keepdims=True) + EPS)
    return (x * r) * g


def _dg(a, b, dims=_NN):
    return lax.dot_general(a, b, dims, preferred_element_type=F32)


def _split2(x):
    hi = x.astype(BF16)
    return hi, (x - hi.astype(F32)).astype(BF16)


def _mm(a, b, precise, dims=_NN):
    if not precise:
        return _dg(a.astype(BF16), b.astype(BF16), dims)
    ah, al = _split2(a.astype(F32))
    bh, bl = _split2(b.astype(F32))
    return _dg(ah, bh, dims) + (_dg(al, bh, dims) + _dg(ah, bl, dims))


def _mmw(a, w_hi, w_lo, precise):
    if not precise:
        return _dg(a.astype(BF16), w_hi)
    ah, al = _split2(a.astype(F32))
    return _dg(ah, w_hi) + (_dg(al, w_hi) + _dg(ah, w_lo))


def _split3(x):
    hi = x.astype(BF16)
    r1 = x - hi.astype(F32)
    mid = r1.astype(BF16)
    lo = (r1 - mid.astype(F32)).astype(BF16)
    return hi, mid, lo


def _dot_exact_lhs(m01, x):
    hi, mid, lo = _split3(x)
    return _dg(m01, hi) + _dg(m01, mid) + _dg(m01, lo)


def _either(precise_pred, body):
    if precise_pred is None:
        body(False)
    else:
        pl.when(precise_pred)(lambda: body(True))
        pl.when(jnp.logical_not(precise_pred))(lambda: body(False))


def _tail_tile(i, tiles_per_seq, tail_tiles, prompt_tiles):
    return (i >= prompt_tiles) | ((i % tiles_per_seq) >= tiles_per_seq - tail_tiles)


def _pair_specs(tm, cols, prompt_tiles):
    return [pl.BlockSpec((tm, cols), lambda i: (jnp.minimum(i, prompt_tiles - 1), 0)),
            pl.BlockSpec((tm, cols), lambda i: (jnp.maximum(i - prompt_tiles, 0), 0))]


def _pair_load(i, prompt_tiles, p_ref, s_ref):
    return jnp.where(i < prompt_tiles, p_ref[...], s_ref[...])


def _w_slice(ref, sl):
    return None if ref is None else ref[:, sl]


def _inproj_kernel(*refs, tail, prompt_tiles):
    if tail is None:
        xp_ref, xs_ref, g_ref, lb_ref, w_ref, qs_ref, lf_ref, kf_ref, v_ref, sog_ref, u_ref = refs
        wl_ref = None
    else:
        xp_ref, xs_ref, g_ref, lb_ref, w_ref, wl_ref, qs_ref, lf_ref, kf_ref, v_ref, sog_ref, u_ref = refs
    d = D_MODEL
    i = pl.program_id(0)

    def body(precise):
        h = _rmsnorm(_pair_load(i, prompt_tiles, xp_ref, xs_ref), g_ref[...])

        def seg(j):
            sl = slice(j * d, (j + 1) * d)
            return _mmw(h, w_ref[:, sl], _w_slice(wl_ref, sl), precise)

        q = seg(0)
        fp = seg(1)
        qs_ref[...] = q * jax.nn.sigmoid(q)
        vv = seg(2)
        lb = lb_ref[...]
        log_sig = jnp.minimum(fp, 0.0) - jnp.log1p(jnp.exp(-jnp.abs(fp)))
        a = jnp.log(lb)
        b = jnp.log1p(-lb) + log_sig
        lf_ref[...] = jnp.maximum(a, b) + jnp.log1p(jnp.exp(-jnp.abs(a - b)))
        og = seg(3)
        kf_ref[...] = (1.0 - lb) * jax.nn.sigmoid(-fp)
        v_ref[...] = vv.astype(v_ref.dtype)
        cc = seg(4)
        sog_ref[...] = (og * jax.nn.sigmoid(og)).astype(sog_ref.dtype)
        u_ref[...] = cc * seg(5)

    _either(None if tail is None else _tail_tile(i, *tail), body)


def _inproj(xp, xs, g, lb, w, w_lo, tail, act_dtype, tm):
    d = D_MODEL
    assert xp.shape[0] % tm == 0 and xs.shape[0] % tm == 0 and xp.shape[1] == d
    pt = xp.shape[0] // tm
    t = xp.shape[0] + xs.shape[0]
    row = pl.BlockSpec((tm, d), lambda i: (i, 0))
    ws = [w] if tail is None else [w, w_lo]
    return pl.pallas_call(
        functools.partial(_inproj_kernel, tail=tail, prompt_tiles=pt),
        grid=(t // tm,),
        in_specs=_pair_specs(tm, d, pt) + [_resident((1, d)), _resident((1, d))] + [_resident(a.shape) for a in ws],
        out_specs=[row] * 6,
        out_shape=[jax.ShapeDtypeStruct((t, d), dt) for dt in (F32, F32, F32, act_dtype, act_dtype, F32)],
        compiler_params=_params(("parallel",)),
        name="inproj",
    )(xp, xs, g, lb, *ws)


def _score_masks(n_rows, block):
    xr = (lax.broadcasted_iota(I32, (n_rows, n_rows), 0)
          ^ lax.broadcasted_iota(I32, (n_rows, n_rows), 1))
    masks = {0: (xr == 0).astype(F32)}
    b = 1
    while 2 * b < n_rows and b < block:
        masks[b] = (xr < 2 * b).astype(F32)
        b *= 2
    return masks


def _level_operands(q, k, g, cum, n_rows, block):
    ri = lax.broadcasted_iota(I32, (n_rows, LANES), 0)
    b = 1
    while b < block:
        if b >= SUBLANES:
            shape3 = (n_rows // (2 * b), 2 * b, LANES)
            c3, q3, k3 = cum.reshape(shape3), q.reshape(shape3), k.reshape(shape3)
            mid = c3[:, b - 1:b, :]
            zero = jnp.zeros((shape3[0], b, LANES), F32)
            qt = jnp.concatenate([zero, q3[:, b:, :] * jnp.exp(c3[:, b:, :] - mid)], axis=1).reshape(n_rows, LANES)
            kt = jnp.concatenate([k3[:, :b, :] * jnp.exp(mid - c3[:, :b, :]), zero], axis=1).reshape(n_rows, LANES)
        else:
            upper = (ri & b) != 0
            if b == 1:
                z = jnp.where(upper, g, 0.0)
            elif b == 2:
                g_prev = pltpu.roll(g, 1, 0)
                g_next = pltpu.roll(g, n_rows - 1, 0)
                m4 = ri & 3
                z = jnp.where(m4 == 2, g, jnp.where(m4 == 3, g + g_prev, jnp.where(m4 == 0, g_next, 0.0)))
            else:
                c3 = cum.reshape(n_rows // (2 * b), 2 * b, LANES)
                mid = jnp.broadcast_to(c3[:, b - 1:b, :], c3.shape).reshape(n_rows, LANES)
                z = jnp.where(upper, cum - mid, mid - cum)
            e = jnp.exp(z)
            qt = jnp.where(upper, q * e, 0.0)
            kt = jnp.where(upper, 0.0, k * e)
        yield b, qt, kt
        b *= 2


def _intra_scores(q, k, g, cum, n_rows, block, precise, masks):
    s = masks[0] * jnp.sum(q * k, axis=1, keepdims=True)
    for b, qt, kt in _level_operands(q, k, g, cum, n_rows, block):
        prod = _mm(qt, kt, precise, _NT)
        s = s + (prod * masks[b] if b in masks else prod)
    return s


def _decay_columns(e_row):
    hi = e_row.astype(BF16).astype(F32)
    lo = e_row - hi
    r = lax.broadcasted_iota(I32, (BF16_ROWS, LANES), 0)
    stacked = jnp.where(r == 0, hi, jnp.where(r == 1, lo, 0.0)).astype(BF16)
    return _dg(stacked, jnp.ones((BF16_ROWS, LANES), BF16), _TN)


def _head_out(o, gh, sog, dtype):
    on = o * lax.rsqrt(jnp.mean(o * o, axis=-1, keepdims=True) + EPS) * gh
    return (on * sog.astype(F32)).astype(dtype)


def _scan_step_staged(qs_ref, lf_ref, kf_ref, v_ref, sog_ref, gh_ref, og_ref, s_scr,
                      cum_scr, qe_scr, ke_scr, qt_scr, kt_scr, sc_scr):
    n = SCAN_CHUNK
    rows = n * SCAN_STEP_CHUNKS
    shift = n.bit_length() - 1
    ri = lax.broadcasted_iota(I32, (rows, rows), 0)
    ci = lax.broadcasted_iota(I32, (rows, rows), 1)
    tri = (((ri >> shift) == (ci >> shift)) & (ri >= ci)).astype(BF16)
    cum_scr[...] = _dot_exact_lhs(tri, lf_ref[...])
    masks = _score_masks(n, n)
    chunks = [slice(j * n, (j + 1) * n) for j in range(SCAN_STEP_CHUNKS)]

    def head_lanes(h):
        if isinstance(h, int):
            return slice(h * HEAD_DIM, (h + 1) * HEAD_DIM)
        return pl.ds(pl.multiple_of(h * HEAD_DIM, HEAD_DIM), HEAD_DIM)

    def operands(h):
        hs = head_lanes(h)
        for rs in chunks:
            q, g, k, cum = qs_ref[rs, hs], lf_ref[rs, hs], kf_ref[rs, hs], cum_scr[rs, hs]
            qe_scr[rs, hs] = (q * jnp.exp(cum)).astype(BF16)
            ke_scr[rs, hs] = (k * jnp.exp(cum[n - 1:n, :] - cum)).astype(BF16)
            yield
            for level, (_, qt, kt) in enumerate(_level_operands(q, k, g, cum, n, n)):
                qt_scr[level, rs, hs] = qt.astype(BF16)
                kt_scr[level, rs, hs] = kt.astype(BF16)
                yield

    def matmuls(h):
        hs = head_lanes(h)
        for rs in chunks:
            s = masks[0] * jnp.sum(qs_ref[rs, hs] * kf_ref[rs, hs], axis=1, keepdims=True)
            b = 1
            for level in range(qt_scr.shape[0]):
                prod = _dg(qt_scr[level, rs, hs], kt_scr[level, rs, hs], _NT)
                s = s + (prod * masks[b] if b in masks else prod)
                b *= 2
                yield
            sc_scr[h, rs, :] = s.astype(BF16)
        s_old = s_scr[h]
        for rs in chunks:
            v = v_ref[rs, hs].astype(BF16)
            o = _dg(qe_scr[rs, hs], s_old.astype(BF16)) + _dg(sc_scr[h, rs, :], v)
            yield
            tot = cum_scr[rs.stop - 1:rs.stop, hs]
            s_old = _decay_columns(jnp.exp(tot)) * s_old + _dg(ke_scr[rs, hs], v, _TN)
            yield
            og_ref[rs, hs] = _head_out(o, gh_ref[:, hs], sog_ref[rs, hs], og_ref.dtype)
            yield
        s_scr[h] = s_old

    def interleave(*phases):
        phases = list(phases)
        while phases:
            for phase in list(phases):
                if next(phase, phases) is phases:
                    phases.remove(phase)

    interleave(operands(0))
    for h in range(N_HEADS - 1):
        interleave(operands(h + 1), matmuls(h))
    interleave(matmuls(N_HEADS - 1))


def _scan_prompt_kernel(qs_ref, lf_ref, kf_ref, v_ref, sog_ref, gh_ref, og_ref, sfin_ref, zeros_ref, s_scr,
                        *stage_scr, tail_steps):
    c = pl.program_id(1)
    nc = pl.num_programs(1)
    n = SCAN_CHUNK
    zeros_ref[...] = jnp.zeros_like(zeros_ref)

    @pl.when(c == 0)
    def _():
        s_scr[...] = jnp.zeros_like(s_scr)

    def body(precise):
        if not precise:
            _scan_step_staged(qs_ref, lf_ref, kf_ref, v_ref, sog_ref, gh_ref, og_ref, s_scr, *stage_scr)
            return
        tri = (lax.broadcasted_iota(I32, (n, n), 0) >= lax.broadcasted_iota(I32, (n, n), 1)).astype(BF16)
        masks = _score_masks(n, n)

        def head(h, carry):
            hs = pl.ds(pl.multiple_of(h * HEAD_DIM, HEAD_DIM), HEAD_DIM)
            s_old = s_scr[h]
            for j in range(SCAN_STEP_CHUNKS):
                rs = slice(j * n, (j + 1) * n)
                q, g, k, v = qs_ref[rs, hs], lf_ref[rs, hs], kf_ref[rs, hs], v_ref[rs, hs]
                cum = _dot_exact_lhs(tri, g)
                tot = cum[n - 1:n, :]
                o = _mm(q * jnp.exp(cum), s_old, precise)
                o = o + _mm(_intra_scores(q, k, g, cum, n, n, precise, masks), v, precise)
                ke = k * jnp.exp(tot - cum)
                s_old = _decay_columns(jnp.exp(tot)) * s_old + _mm(ke, v, precise, _TN)
                og_ref[rs, hs] = _head_out(o, gh_ref[:, hs], sog_ref[rs, hs], og_ref.dtype)
            s_scr[h] = s_old
            return carry

        lax.fori_loop(0, N_HEADS, head, 0)

    _either(None if tail_steps is None else c >= nc - tail_steps, body)

    @pl.when(c == nc - 1)
    def _():
        sfin_ref[0] = s_scr[...]


def _scan_prompt(qs, lf, kf, v, sog, gh, batch, seq, tail_rows):
    t, d = qs.shape
    n = SCAN_CHUNK * SCAN_STEP_CHUNKS
    assert seq % n == 0 and (tail_rows is None or tail_rows % n == 0)
    nc = seq // n
    levels = SCAN_CHUNK.bit_length() - 1
    row = pl.BlockSpec((n, d), lambda b, c: (b * nc + c, 0))
    sorted_rows = (t // MOE_TILE + 1) * MOE_SORTED_ROWS
    zrows = sorted_rows // (batch * nc)
    assert zrows * batch * nc == sorted_rows and zrows % BF16_ROWS == 0
    return pl.pallas_call(
        functools.partial(_scan_prompt_kernel, tail_steps=None if tail_rows is None else tail_rows // n),
        grid=(batch, nc),
        in_specs=[row, row, row, row, row, _resident((1, d))],
        out_specs=[row, pl.BlockSpec((1, N_HEADS, HEAD_DIM, HEAD_DIM), lambda b, c: (b, 0, 0, 0)),
                   pl.BlockSpec((zrows, d), lambda b, c: (b * nc + c, 0))],
        out_shape=[jax.ShapeDtypeStruct((batch * seq, d), v.dtype),
                   jax.ShapeDtypeStruct((batch, N_HEADS, HEAD_DIM, HEAD_DIM), F32),
                   jax.ShapeDtypeStruct((sorted_rows, d), BF16)],
        scratch_shapes=[pltpu.VMEM((N_HEADS, HEAD_DIM, HEAD_DIM), F32),
                        pltpu.VMEM((n, d), F32),
                        pltpu.VMEM((n, d), BF16), pltpu.VMEM((n, d), BF16),
                        pltpu.VMEM((levels, n, d), BF16), pltpu.VMEM((levels, n, d), BF16),
                        pltpu.VMEM((N_HEADS, n, SCAN_CHUNK), BF16)],
        compiler_params=_params(("parallel", "arbitrary")),
        name="scan_prompt",
    )(qs, lf, kf, v, sog, gh)


def _scan_sample_kernel(*refs, steps, precise, layer, chained):
    if chained:
        qs_ref, lf_ref, kf_ref, v_ref, sog_ref, gh_ref, s_ref, _, og_ref, snew_ref = refs
    else:
        qs_ref, lf_ref, kf_ref, v_ref, sog_ref, gh_ref, s_ref, og_ref, all_ref = refs
        for other in range(all_ref.shape[0]):
            if other != layer:
                all_ref[other] = jnp.zeros(all_ref.shape[1:], F32)
        snew_ref = all_ref.at[layer]
    n = SAMPLE_SEQS * steps
    shift = steps.bit_length() - 1
    ri = lax.broadcasted_iota(I32, (n, n), 0)
    ci = lax.broadcasted_iota(I32, (n, n), 1)
    same = (ri >> shift) == (ci >> shift)
    tri = (same & (ri >= ci)).astype(BF16)
    tri_after = (same & (ci > ri)).astype(BF16)
    seq_of_row = lax.broadcasted_iota(I32, (n, LANES), 0) >> shift
    masks = _score_masks(n, steps)

    def head(h, carry):
        hs = pl.ds(pl.multiple_of(h * HEAD_DIM, HEAD_DIM), HEAD_DIM)
        q, g, k, v = qs_ref[:, hs], lf_ref[:, hs], kf_ref[:, hs], v_ref[:, hs]
        cum = _dot_exact_lhs(tri, g)
        after = _dot_exact_lhs(tri_after, g)
        o = _mm(_intra_scores(q, k, g, cum, n, steps, precise, masks), v, precise)
        qe = q * jnp.exp(cum)
        ke = k * jnp.exp(after)
        for s in range(SAMPLE_SEQS):
            mine = seq_of_row == s
            s_old = s_ref[s, h]
            o = o + jnp.where(mine, _mm(qe, s_old, precise), 0.0)
            kv = _mm(jnp.where(mine, ke, 0.0), v, precise, _TN)
            last = (s + 1) * steps - 1
            snew_ref[s, h] = _decay_columns(jnp.exp(cum[last:last + 1, :])) * s_old + kv
        og_ref[:, hs] = _head_out(o, gh_ref[:, hs], sog_ref[:, hs], og_ref.dtype)
        return carry

    lax.fori_loop(0, N_HEADS, head, 0)


def _scan_sample(qs, lf, kf, v, sog, gh, states, layer, new_states, row0, steps, precise):
    t, d = qs.shape
    depth, nseq = states.shape[:2]
    assert steps & (steps - 1) == 0 and steps >= 4, "sample block must be a power of two >= 4"
    n = SAMPLE_SEQS * steps
    assert nseq % SAMPLE_SEQS == 0 and row0 % n == 0 and n % BF16_ROWS == 0
    b0 = row0 // n
    chained = new_states is not None
    row = pl.BlockSpec((n, d), lambda i: (b0 + i, 0))
    tail = (SAMPLE_SEQS, N_HEADS, HEAD_DIM, HEAD_DIM)
    st = pl.BlockSpec((None,) + tail, lambda i: (layer, i, 0, 0, 0))
    st_all = pl.BlockSpec((depth,) + tail, lambda i: (0, i, 0, 0, 0))
    extra = [new_states] if chained else []
    return pl.pallas_call(
        functools.partial(_scan_sample_kernel, steps=steps, precise=precise, layer=layer, chained=chained),
        grid=(nseq // SAMPLE_SEQS,),
        in_specs=[row, row, row, row, row, _resident((1, d)), st] + [pl.BlockSpec(memory_space=pl.ANY)] * len(extra),
        out_specs=[pl.BlockSpec((n, d), lambda i: (i, 0)), st if chained else st_all],
        out_shape=[jax.ShapeDtypeStruct((nseq * steps, d), v.dtype), jax.ShapeDtypeStruct(states.shape, F32)],
        input_output_aliases={7: 1} if chained else {},
        compiler_params=_params(("parallel",)),
        name="scan_sample",
    )(qs, lf, kf, v, sog, gh, states, *extra)


def _mix_kernel(*refs, tm, tiles_per_seq, prompt_tiles, steps, tail):
    if tail is None:
        (xp_ref, xs_ref, ogp_ref, ogs_ref, u_ref, halo_ref, e1_ref, e2_ref, g_ref, wc_ref, gf_ref, wr_ref, br_ref,
         w_cbg_ref, wa_ref, wb_ref, wo_ref, x1_ref, sorted_ref, rinfo_ref, cinfo_ref, prev_scr) = refs
        l_cbg_ref = la_ref = lb_ref = lo_ref = None
    else:
        (xp_ref, xs_ref, ogp_ref, ogs_ref, u_ref, halo_ref, e1_ref, e2_ref, g_ref, wc_ref, gf_ref, wr_ref, br_ref,
         w_cbg_ref, wa_ref, wb_ref, wo_ref, l_cbg_ref, la_ref, lb_ref, lo_ref,
         x1_ref, sorted_ref, rinfo_ref, cinfo_ref, prev_scr) = refs
    d = D_MODEL
    i = pl.program_id(0)

    @pl.when(i == 0)
    def _():
        prev_scr[...] = jnp.zeros_like(prev_scr)

    def full(ref):
        return None if ref is None else ref[...]

    def body(precise):
        x = _pair_load(i, prompt_tiles, xp_ref, xs_ref)
        h = _rmsnorm(x, g_ref[...])
        u = u_ref[...]
        ri = lax.broadcasted_iota(I32, (tm, d), 0)
        r1 = pltpu.roll(u, 1, 0)
        r2 = pltpu.roll(u, 2, 0)
        keep = jnp.where(i % tiles_per_seq == 0, 0.0, 1.0)
        halo = halo_ref[...]
        h_last = halo[7:8, :] * keep
        h_prev = halo[6:7, :] * keep
        is_sample = i >= prompt_tiles
        t_in_seq = ri & (steps - 1)
        prev1 = jnp.where(is_sample,
                          jnp.where(t_in_seq == 0, e1_ref[...], r1),
                          jnp.where(ri == 0, h_last, r1))
        prev2 = jnp.where(is_sample,
                          jnp.where(t_in_seq < 2, e2_ref[...], r2),
                          jnp.where(ri == 0, h_prev, jnp.where(ri == 1, h_last, r2)))
        wc = wc_ref[...]
        conv = prev2 * wc[0:1, :] + prev1 * wc[1:2, :] + u * wc[2:3, :]

        def seg(j):
            sl = slice(j * d, (j + 1) * d)
            return _mmw(h, w_cbg_ref[:, sl], _w_slice(l_cbg_ref, sl), precise)

        router = _route_stages(prev_scr[...], gf_ref[...], wr_ref[...], br_ref[...],
                               sorted_ref, rinfo_ref, cinfo_ref)
        cb = seg(0)
        next(router)
        y_b = _mmw(cb * conv, wb_ref[...], full(lb_ref), precise)
        next(router)
        y_a = _mmw(_pair_load(i, prompt_tiles, ogp_ref, ogs_ref), wa_ref[...], full(la_ref), precise)
        next(router)
        ga = seg(1)
        next(router)
        gb = seg(2)
        next(router)
        m = jax.nn.sigmoid(ga) * y_a + jax.nn.sigmoid(gb) * y_b
        x1 = x + _mmw(m, wo_ref[...], full(lo_ref), precise)
        for _ in router:
            pass
        x1_ref[...] = x1
        prev_scr[...] = x1

    _either(None if tail is None else _tail_tile(i, *tail), body)


def _mix(xp, xs, ogp, ogs, u, e1, e2, g, wc, g_ffn, wr, br, ws, ws_lo, seq, steps, tail):
    t, d = u.shape
    tm = MOE_TILE
    prompt_rows = xp.shape[0]
    assert seq % tm == 0 and prompt_rows % tm == 0 and (t - prompt_rows) % tm == 0 and tm % steps == 0
    pt = prompt_rows // tm
    nt = t // tm
    cur = lambda i: jnp.minimum(i, nt - 1)
    prev = lambda i: jnp.maximum(i - 1, 0)
    prompt = pl.BlockSpec((tm, d), lambda i: (jnp.minimum(i, pt - 1), 0))
    sample = pl.BlockSpec((tm, d), lambda i: (jnp.clip(i - pt, 0, nt - pt - 1), 0))
    row = pl.BlockSpec((tm, d), lambda i: (cur(i), 0))
    halo = pl.BlockSpec((SUBLANES, d), lambda i: (jnp.maximum(cur(i) * (tm // SUBLANES) - 1, 0), 0))
    weights = list(ws) + ([] if tail is None else list(ws_lo))
    return pl.pallas_call(
        functools.partial(_mix_kernel, tm=tm, tiles_per_seq=seq // tm, prompt_tiles=pt, steps=steps, tail=tail),
        grid=(nt + 1,),
        in_specs=[prompt, sample, prompt, sample, row, halo, sample, sample, _resident((1, d)), _resident(wc.shape),
                  _resident((1, d)), _resident(wr.shape), _resident(br.shape)]
        + [_resident(a.shape) for a in weights],
        out_specs=[row,
                   pl.BlockSpec((1, MOE_SORTED_ROWS, d), lambda i: (prev(i), 0, 0)),
                   pl.BlockSpec((tm, LANES), lambda i: (prev(i), 0)),
                   pl.BlockSpec((1, SUBLANES, LANES), lambda i: (prev(i), 0, 0))],
        out_shape=[jax.ShapeDtypeStruct((t, d), F32),
                   jax.ShapeDtypeStruct((nt, MOE_SORTED_ROWS, d), BF16),
                   jax.ShapeDtypeStruct((t, LANES), F32),
                   jax.ShapeDtypeStruct((nt, SUBLANES, LANES), I32)],
        scratch_shapes=[pltpu.VMEM((tm, d), F32)],
        compiler_params=_params(("arbitrary",)),
        name="mix_route",
    )(xp, xs, ogp, ogs, u, u, e1, e2, g, wc, g_ffn, wr, br, *weights)


def _route_stages(x, g, wr, br, xs_ref, rinfo_ref, cinfo_ref):
    tm = MOE_TILE
    rows = MOE_SORTED_ROWS
    hn = _rmsnorm(x, g)
    hn_hi, hn_lo = _split2(hn)
    wr_hi, wr_lo = _split2(wr)
    logits = _dg(hn_hi, wr_hi) + _dg(hn_lo, wr_hi) + _dg(hn_hi, wr_lo) + br
    yield

    lane = lax.broadcasted_iota(I32, (tm, LANES), 1)
    neg = -jnp.inf
    big = jnp.int32(LANES)

    def first_lane(mask):
        return jnp.min(jnp.where(mask, lane, big), axis=1, keepdims=True)

    is_group = (lane >= GROUP_LANE0) & (lane < GROUP_LANE0 + N_GROUPS)
    gl = jnp.where(is_group, logits, neg)
    gmax = jnp.max(gl, axis=1, keepdims=True)
    g_idx = first_lane(gl == gmax) - GROUP_LANE0
    g_w = 1.0 / jnp.sum(jnp.exp(gl - gmax), axis=1, keepdims=True)
    yield

    in_group = (lane < N_EXPERTS) & ((lane >> (EXPERTS_PER_GROUP.bit_length() - 1)) == g_idx)
    el = jnp.where(in_group, logits, neg)
    emax = jnp.max(el, axis=1, keepdims=True)
    ee = jnp.exp(el - emax)
    prob = ee / jnp.sum(ee, axis=1, keepdims=True)
    prob = jnp.where(in_group, prob, -1.0)
    yield
    p1 = jnp.max(prob, axis=1, keepdims=True)
    i1 = first_lane(prob == p1)
    prob2 = jnp.where(lane == i1, -1.0, prob)
    p2 = jnp.max(prob2, axis=1, keepdims=True)
    i2 = first_lane(prob2 == p2)
    psum = p1 + p2
    w1 = p1 / psum * g_w
    w2 = p2 / psum * g_w
    yield

    oh1 = (lane == i1).astype(BF16)
    oh2 = (lane == i2).astype(BF16)
    before = (lax.broadcasted_iota(I32, (tm, tm), 1) < lax.broadcasted_iota(I32, (tm, tm), 0)).astype(BF16)
    c1 = _dg(before, oh1)
    c2 = _dg(before, oh2)
    oh1f = oh1.astype(F32)
    oh2f = oh2.astype(F32)
    cnt1 = jnp.sum(oh1f, axis=0, keepdims=True)
    cnt2 = jnp.sum(oh2f, axis=0, keepdims=True)
    chunks = jnp.floor((cnt1 + cnt2 + (BF16_ROWS - 1)) * (1.0 / BF16_ROWS))
    excl = (lax.broadcasted_iota(I32, (LANES, LANES), 0) < lax.broadcasted_iota(I32, (LANES, LANES), 1)).astype(BF16)
    chunk0 = _dg(jnp.broadcast_to(chunks, (BF16_ROWS, LANES)).astype(BF16), excl)[0:1, :]
    yield
    base = chunk0 * BF16_ROWS
    pos1 = jnp.sum(oh1f * (base + c1), axis=1, keepdims=True)
    pos2 = jnp.sum(oh2f * (base + cnt1 + c2), axis=1, keepdims=True)

    slot = lax.broadcasted_iota(I32, (tm, rows), 1)
    place = ((slot == pos1.astype(I32)) | (slot == pos2.astype(I32))).astype(BF16)
    xs_ref[0] = _dg(place, hn_hi, _TN).astype(BF16)

    rinfo_ref[...] = jnp.where(lane == 0, pos1, jnp.where(lane == 1, pos2, jnp.where(lane == 2, w1, jnp.where(lane == 3, w2, 0.0))))
    r8 = lax.broadcasted_iota(I32, (SUBLANES, LANES), 0)
    cinfo_ref[0] = jnp.where(r8 == 0, chunk0, jnp.where(r8 == 1, chunks, 0.0)).astype(I32)


def _ffn_kernel(order_ref, first_ref, left_ref, blk0_ref, xs_hbm, wg_hbm, wu_hbm, wd_hbm, ys_in_hbm, ys_hbm,
                lhs_buf, res_buf, wg_stage, wu_stage, wd_stage, wgu_bf, wd_bf, in_sem, out_sem, w_sem, *, layer):
    del ys_in_hbm
    nb = MOE_BLOCK_CHUNKS
    rows = BF16_ROWS
    nblocks = blk0_ref[N_EXPERTS]

    spare0 = xs_hbm.shape[0]

    def gather_start(blk, slot):
        first, left = first_ref[blk], left_ref[blk]
        for i in range(nb):
            src = order_ref[first + jnp.where(i < left, i, 0)]
            pltpu.make_async_copy(xs_hbm.at[src], lhs_buf.at[slot, pl.ds(i * rows, rows)], in_sem.at[slot]).start()

    def scatter_start(blk, slot):
        first, left = first_ref[blk], left_ref[blk]
        for i in range(nb):
            dst = jnp.where(i < left, order_ref[first + jnp.where(i < left, i, 0)], spare0 + slot * nb + i)
            pltpu.make_async_copy(res_buf.at[slot, pl.ds(i * rows, rows)], ys_hbm.at[dst], out_sem.at[slot]).start()

    def gather_wait(slot):
        pltpu.make_async_copy(lhs_buf.at[slot], lhs_buf.at[slot], in_sem.at[slot]).wait()

    def scatter_wait(slot):
        pltpu.make_async_copy(res_buf.at[slot], res_buf.at[slot], out_sem.at[slot]).wait()

    def weight_copies(e, slot):
        return [pltpu.make_async_copy(src.at[layer, e], dst.at[slot], w_sem.at[slot])
                for src, dst in ((wg_hbm, wg_stage), (wu_hbm, wu_stage), (wd_hbm, wd_stage))]

    for cp in weight_copies(0, 0):
        cp.start()

    ahead = FFN_GATHER_DEPTH - 1

    @pl.when(nblocks > 0)
    def _():
        for j in range(ahead):
            gather_start(jnp.minimum(j, nblocks - 1), j)

    def expert(e, carry):
        ws = e % 2
        for cp in weight_copies(e, ws):
            cp.wait()
        wgu_bf[:, :D_EXPERT] = wg_stage[ws].astype(BF16)
        wgu_bf[:, D_EXPERT:] = wu_stage[ws].astype(BF16)
        wd_bf[...] = wd_stage[ws].astype(BF16)

        @pl.when(e + 1 < N_EXPERTS)
        def _():
            for cp in weight_copies(e + 1, 1 - ws):
                cp.start()

        def block(blk, c):
            slot = blk % 2

            @pl.when(blk >= 2)
            def _():
                scatter_wait(slot)

            gslot = blk % FFN_GATHER_DEPTH
            gather_wait(gslot)
            gather_start(jnp.minimum(blk + ahead, nblocks - 1), (blk + ahead) % FFN_GATHER_DEPTH)
            gu = _dg(lhs_buf[gslot], wgu_bf[...])
            gate = gu[:, :D_EXPERT]
            hid = gate * jax.nn.sigmoid(gate) * gu[:, D_EXPERT:]
            res_buf[slot] = _dg(hid.astype(BF16), wd_bf[...]).astype(BF16)
            scatter_start(blk, slot)
            return c

        lax.fori_loop(blk0_ref[e], blk0_ref[e + 1], block, 0)
        return carry

    lax.fori_loop(0, N_EXPERTS, expert, 0)

    @pl.when(nblocks > 0)
    def _():
        for j in range(ahead):
            gather_wait((nblocks + j) % FFN_GATHER_DEPTH)

    for back in (2, 1):
        @pl.when(nblocks >= back)
        def _():
            blk = nblocks - back
            scatter_wait(blk % 2)


def _ffn(xs, zeros, order, first, left, blk0, w_gate, w_up, w_down, layer):
    nt, srows, d = xs.shape
    assert zeros.shape == (nt + 1, srows, d) and 2 * MOE_BLOCK_CHUNKS <= MOE_CHUNKS
    chunked = (nt * MOE_CHUNKS, BF16_ROWS, d)
    chunked_out = ((nt + 1) * MOE_CHUNKS, BF16_ROWS, d)
    brows = MOE_BLOCK_CHUNKS * BF16_ROWS
    anywhere = pl.BlockSpec(memory_space=pl.ANY)
    f = w_gate.shape[-1]
    ys = pl.pallas_call(
        functools.partial(_ffn_kernel, layer=layer),
        grid_spec=pltpu.PrefetchScalarGridSpec(
            num_scalar_prefetch=4,
            grid=(1,),
            in_specs=[anywhere] * 5,
            out_specs=anywhere,
            scratch_shapes=[pltpu.VMEM((FFN_GATHER_DEPTH, brows, d), BF16), pltpu.VMEM((2, brows, d), BF16),
                            pltpu.VMEM((2, d, f), F32), pltpu.VMEM((2, d, f), F32), pltpu.VMEM((2, f, d), F32),
                            pltpu.VMEM((d, 2 * f), BF16), pltpu.VMEM((f, d), BF16),
                            pltpu.SemaphoreType.DMA((FFN_GATHER_DEPTH,)), pltpu.SemaphoreType.DMA((2,)),
                            pltpu.SemaphoreType.DMA((2,))],
        ),
        out_shape=jax.ShapeDtypeStruct(chunked_out, BF16),
        input_output_aliases={8: 0},
        compiler_params=_params(("arbitrary",)),
        name="expert_ffn",
    )(order, first, left, blk0, xs.reshape(chunked), w_gate, w_up, w_down, zeros.reshape(chunked_out))
    return ys.reshape(nt + 1, srows, d)


def _block_tables(cinfo):
    nt = cinfo.shape[0]
    nb = MOE_BLOCK_CHUNKS
    nchunks = nt * MOE_CHUNKS
    chunk0 = cinfo[:, 0, :N_EXPERTS]
    chunks = cinfo[:, 1, :N_EXPERTS]
    ends = chunk0 + chunks
    c = jnp.arange(MOE_CHUNKS, dtype=I32)
    label = jnp.sum((c[None, :, None] >= ends[:, None, :]).astype(I32), axis=-1)
    order = jnp.argsort(label.reshape(-1), stable=True).astype(I32)
    cnt = chunks.sum(axis=0).astype(I32)
    cstart = jnp.cumsum(cnt) - cnt
    nblk = (cnt + nb - 1) // nb
    blk0 = jnp.concatenate([jnp.zeros((1,), I32), jnp.cumsum(nblk).astype(I32)])
    max_blocks = nchunks // nb + N_EXPERTS
    j = jnp.arange(max_blocks, dtype=I32)
    ej = jnp.minimum(jnp.sum((j[:, None] >= blk0[None, 1:]).astype(I32), axis=1), N_EXPERTS - 1)
    mine = (ej[:, None] == jnp.arange(N_EXPERTS, dtype=I32)[None, :]).astype(I32)
    local = j - jnp.sum(mine * blk0[None, :N_EXPERTS], axis=1)
    first = jnp.sum(mine * cstart[None, :], axis=1) + local * nb
    left = jnp.sum(mine * cnt[None, :], axis=1) - local * nb
    return order, jnp.clip(first, 0, nchunks - 1).astype(I32), jnp.maximum(left, 0).astype(I32), blk0


def _ple_kernel(x1_ref, ys_ref, rinfo_ref, pp_ref, ps_ref, g_ref, wg_ref, wp_ref, gfin_ref, outp_ref, outs_ref, *,
                final, prompt_tiles):
    tm = MOE_TILE
    rows = MOE_SORTED_ROWS
    i = pl.program_id(0)
    rinfo = rinfo_ref[...]
    pos1 = rinfo[:, 0:1].astype(I32)
    pos2 = rinfo[:, 1:2].astype(I32)
    w1 = rinfo[:, 2:3]
    w2 = rinfo[:, 3:4]
    slot = lax.broadcasted_iota(I32, (tm, rows), 1)
    ys = ys_ref[0]
    y1 = _dg((slot == pos1).astype(BF16), ys)
    y2 = _dg((slot == pos2).astype(BF16), ys)
    x = x1_ref[...] + (w1 * y1 + w2 * y2)
    hn = _rmsnorm(x, g_ref[...]).astype(BF16)
    proj = _dg(_pair_load(i, prompt_tiles, pp_ref, ps_ref).astype(BF16), wp_ref[...])
    x = x + jax.nn.sigmoid(_dg(hn, wg_ref[...])) * proj
    if final:
        x = _rmsnorm(x, gfin_ref[...])

    @pl.when(i < prompt_tiles)
    def _():
        outp_ref[...] = x

    @pl.when(i >= prompt_tiles)
    def _():
        outs_ref[...] = x


def _ple(x1, ys, rinfo, p_prompt, p_sample, layer, g, wg, wp, gfin, final):
    t, d = x1.shape
    tm = MOE_TILE
    nt = t // tm
    tp, ts = p_prompt.shape[1], p_sample.shape[1]
    assert tp % tm == 0 and ts % tm == 0 and tp + ts == t
    pt = tp // tm
    pd = p_prompt.shape[2]
    ys = ys.reshape(-1, MOE_SORTED_ROWS, d)
    return pl.pallas_call(
        functools.partial(_ple_kernel, final=final, prompt_tiles=pt),
        grid=(nt,),
        in_specs=[pl.BlockSpec((tm, d), lambda i: (i, 0)),
                  pl.BlockSpec((1, MOE_SORTED_ROWS, d), lambda i: (i, 0, 0)),
                  pl.BlockSpec((tm, LANES), lambda i: (i, 0)),
                  pl.BlockSpec((None, tm, pd), lambda i: (layer, jnp.minimum(i, pt - 1), 0)),
                  pl.BlockSpec((None, tm, pd), lambda i: (layer, jnp.maximum(i - pt, 0), 0)),
                  _resident((1, d)), _resident(wg.shape), _resident(wp.shape), _resident((1, d))],
        out_specs=_pair_specs(tm, d, pt),
        out_shape=[jax.ShapeDtypeStruct((tp, d), F32), jax.ShapeDtypeStruct((ts, d), F32)],
        compiler_params=_params(("arbitrary",)),
        name="ple",
    )(x1, ys, rinfo, p_prompt, p_sample, g, wg, wp, gfin)


def _lower_bounds(lb_param):
    s = jax.nn.softmax(lb_param.astype(F32), axis=0)
    c = jnp.cumsum(s, axis=0)
    return c - c[0:1]


def _split_kernel(w_ref, hi_ref, lo_ref):
    hi_ref[...], lo_ref[...] = _split2(w_ref[...])


def _hi_lo(w):
    k, n = w.shape
    tk = ROW_TILE
    assert k % tk == 0
    spec = pl.BlockSpec((tk, n), lambda i: (i, 0))
    return pl.pallas_call(
        _split_kernel,
        grid=(k // tk,),
        in_specs=[spec],
        out_specs=[spec, spec],
        out_shape=[jax.ShapeDtypeStruct((k, n), BF16)] * 2,
        compiler_params=_params(("parallel",)),
        name="split_weight",
    )(w)


def kernel(x_prompt, x_sample, state_hgrn, state_conv, p_prompt, p_sample, g_mix, w_in, hg_lower, g_hg_out, w_br_a, w_conv, w_br_b, w_out, g_ffn, w_router_group, b_router_group, w_router_expert, b_router_expert, w_gate, w_up, w_down, g_ple, w_ple_gate, w_ple_proj, g_final):
    depth = w_in.shape[0]
    bp, seq, d = x_prompt.shape
    bs, steps, _ = x_sample.shape
    assert d == D_MODEL and w_conv.shape[1] == CONV_W
    assert TAIL_ROWS % ROW_TILE == 0 and seq >= TAIL_ROWS
    tp = bp * seq
    ts = bs * steps
    hw = N_HEADS * HEAD_DIM

    xp, xs_rows = x_prompt.reshape(tp, d), x_sample.reshape(ts, d)
    pp = p_prompt.reshape(depth, tp, -1)
    ps = p_sample.reshape(depth, ts, -1)
    lbs = _lower_bounds(hg_lower)
    row = lambda a: a.reshape(1, -1)

    hg_p, cv_p, cv_s = [], [], []
    hg_s = None
    for li in range(depth):
        careful = li < depth - 1
        tm = ROW_TILE if careful else FAST_ROW_TILE
        tail = (seq // tm, TAIL_ROWS // tm, tp // tm) if careful else None
        act = F32 if careful else BF16
        wi = w_in[li]
        w_scan = jnp.concatenate([wi[:, :4 * hw], wi[:, 4 * hw + d:4 * hw + 3 * d]], axis=1)
        w_cbg = jnp.concatenate([wi[:, 4 * hw:4 * hw + d], wi[:, 4 * hw + 3 * d:]], axis=1)
        mix_w = (w_cbg, w_br_a[li], w_br_b[li], w_out[li])
        if careful:
            w_scan, w_scan_lo = _hi_lo(w_scan)
            mix_w, mix_w_lo = zip(*[_hi_lo(w) for w in mix_w])
        else:
            w_scan, w_scan_lo = w_scan.astype(BF16), None
            mix_w, mix_w_lo = [w.astype(BF16) for w in mix_w], None

        qs, lf, kf, v, sog, u = _inproj(xp, xs_rows, row(g_mix[li]), row(lbs[li]), w_scan, w_scan_lo, tail, act, tm)

        gh = row(g_hg_out[li])
        ogp, s_p, zeros = _scan_prompt(qs, lf, kf, v, sog, gh, bp, seq, TAIL_ROWS if careful else None)
        ogs, hg_s = _scan_sample(qs, lf, kf, v, sog, gh, state_hgrn, li, hg_s, tp, steps, careful)
        hg_p.append(s_p)
        keep = CONV_W - 1
        cv_p.append(jnp.stack([lax.slice_in_dim(u, (b + 1) * seq - keep, (b + 1) * seq) for b in range(bp)]))
        cv_s.append(jnp.concatenate([state_conv[li], lax.slice_in_dim(u, tp, tp + ts).reshape(bs, steps, d)],
                                    axis=1)[:, steps:])

        buf = state_conv[li]
        zero = jnp.zeros((bs, steps - 1, d), F32)
        e1 = jnp.concatenate([buf[:, 1:2], zero], axis=1).reshape(ts, d)
        e2 = jnp.concatenate([buf, zero[:, 1:]], axis=1).reshape(ts, d)
        wr = jnp.zeros((d, LANES), F32).at[:, :N_EXPERTS].set(w_router_expert[li])
        wr = wr.at[:, GROUP_LANE0:GROUP_LANE0 + N_GROUPS].set(w_router_group[li])
        br = jnp.zeros((1, LANES), F32).at[0, :N_EXPERTS].set(b_router_expert[li])
        br = br.at[0, GROUP_LANE0:GROUP_LANE0 + N_GROUPS].set(b_router_group[li])
        mix_tail = (seq // MOE_TILE, TAIL_ROWS // MOE_TILE, tp // MOE_TILE) if careful else None
        x1, xsort, rinfo, cinfo = _mix(xp, xs_rows, ogp, ogs, u, e1, e2, row(g_mix[li]), w_conv[li],
                                       row(g_ffn[li]), wr, br, mix_w, mix_w_lo, seq, steps, mix_tail)
        ys = _ffn(xsort, zeros.reshape(-1, MOE_SORTED_ROWS, d), *_block_tables(cinfo), w_gate, w_up, w_down, li)

        xp, xs_rows = _ple(x1, ys, rinfo, pp, ps, li, row(g_ple[li]), w_ple_gate[li].astype(BF16),
                           w_ple_proj[li].astype(BF16), row(g_final), li == depth - 1)

    return (xp.reshape(bp, seq, d), xs_rows.reshape(bs, steps, d), jnp.stack(hg_p), jnp.stack(cv_p), hg_s,
            jnp.stack(cv_s))
```

```python
import functools

import jax
import jax.numpy as jnp
from jax import lax
from jax.experimental import pallas as pl
from jax.experimental.pallas import tpu as pltpu

F32 = jnp.float32
BF16 = jnp.bfloat16
I32 = jnp.int32

D_MODEL = 1024
N_HEADS = 8
HEAD_DIM = 128
CONV_W = 3
N_GROUPS = 4
EXPERTS_PER_GROUP = 8
N_EXPERTS = N_GROUPS * EXPERTS_PER_GROUP
D_EXPERT = 256
EPS = 1e-6

LANES = 128
SUBLANES = 8
BF16_ROWS = 16
VMEM_LIMIT = 56 * 1024 * 1024

ROW_TILE = 256
FAST_ROW_TILE = 512
SCAN_CHUNK = 128
SCAN_STEP_CHUNKS = 2
SCAN_HEAD_UNROLL = 2
TAIL_ROWS = 256
SAMPLE_SEQS = 8
MOE_TILE = 256
MOE_SORTED_ROWS = 1024
MOE_CHUNKS = MOE_SORTED_ROWS // BF16_ROWS
MOE_BLOCK_CHUNKS = 32
FFN_GATHER_DEPTH = 6
GROUP_LANE0 = N_EXPERTS

_NN = (((1,), (0,)), ((), ()))
_NT = (((1,), (1,)), ((), ()))
_TN = (((0,), (0,)), ((), ()))


def _params(sem):
    return pltpu.CompilerParams(dimension_semantics=sem, vmem_limit_bytes=VMEM_LIMIT)


def _resident(shape):
    nd = len(shape)
    return pl.BlockSpec(shape, lambda *_: (0,) * nd, pipeline_mode=pl.Buffered(1))


def _rmsnorm(x, g):
    r = lax.rsqrt(jnp.mean(x * x, axis=-1, keepdims=True) + EPS)
    return (x * r) * g


def _dg(a, b, dims=_NN):
    return lax.dot_general(a, b, dims, preferred_element_type=F32)


def _split2(x):
    hi = x.astype(BF16)
    return hi, (x - hi.astype(F32)).astype(BF16)


def _mm(a, b, precise, dims=_NN):
    if not precise:
        return _dg(a.astype(BF16), b.astype(BF16), dims)
    ah, al = _split2(a.astype(F32))
    bh, bl = _split2(b.astype(F32))
    return _dg(ah, bh, dims) + (_dg(al, bh, dims) + _dg(ah, bl, dims))


def _mmw(a, w_hi, w_lo, precise):
    if not precise:
        return _dg(a.astype(BF16), w_hi)
    ah, al = _split2(a.astype(F32))
    return _dg(ah, w_hi) + (_dg(al, w_hi) + _dg(ah, w_lo))


def _split3(x):
    hi = x.astype(BF16)
    r1 = x - hi.astype(F32)
    mid = r1.astype(BF16)
    lo = (r1 - mid.astype(F32)).astype(BF16)
    return hi, mid, lo


def _dot_exact_lhs(m01, x):
    hi, mid, lo = _split3(x)
    return _dg(m01, hi) + _dg(m01, mid) + _dg(m01, lo)


def _either(precise_pred, body):
    if precise_pred is None:
        body(False)
    else:
        pl.when(precise_pred)(lambda: body(True))
        pl.when(jnp.logical_not(precise_pred))(lambda: body(False))


def _tail_tile(i, tiles_per_seq, tail_tiles, prompt_tiles):
    return (i >= prompt_tiles) | ((i % tiles_per_seq) >= tiles_per_seq - tail_tiles)


def _pair_specs(tm, cols, prompt_tiles):
    return [pl.BlockSpec((tm, cols), lambda i: (jnp.minimum(i, prompt_tiles - 1), 0)),
            pl.BlockSpec((tm, cols), lambda i: (jnp.maximum(i - prompt_tiles, 0), 0))]


def _pair_load(i, prompt_tiles, p_ref, s_ref):
    return jnp.where(i < prompt_tiles, p_ref[...], s_ref[...])


def _w_slice(ref, sl):
    return None if ref is None else ref[:, sl]


def _inproj_kernel(*refs, tail, prompt_tiles):
    if tail is None:
        xp_ref, xs_ref, g_ref, lb_ref, w_ref, qs_ref, lf_ref, kf_ref, v_ref, sog_ref, u_ref = refs
        wl_ref = None
    else:
        xp_ref, xs_ref, g_ref, lb_ref, w_ref, wl_ref, qs_ref, lf_ref, kf_ref, v_ref, sog_ref, u_ref = refs
    d = D_MODEL
    i = pl.program_id(0)

    def body(precise):
        h = _rmsnorm(_pair_load(i, prompt_tiles, xp_ref, xs_ref), g_ref[...])

        def seg(j):
            sl = slice(j * d, (j + 1) * d)
            return _mmw(h, w_ref[:, sl], _w_slice(wl_ref, sl), precise)

        def put(ref, val):
            for hd in range(N_HEADS):
                ref[hd] = val[:, hd * HEAD_DIM:(hd + 1) * HEAD_DIM].astype(ref.dtype)

        q = seg(0)
        fp = seg(1)
        put(qs_ref, q * jax.nn.sigmoid(q))
        vv = seg(2)
        lb = lb_ref[...]
        log_sig = jnp.minimum(fp, 0.0) - jnp.log1p(jnp.exp(-jnp.abs(fp)))
        a = jnp.log(lb)
        b = jnp.log1p(-lb) + log_sig
        put(lf_ref, jnp.maximum(a, b) + jnp.log1p(jnp.exp(-jnp.abs(a - b))))
        og = seg(3)
        put(kf_ref, (1.0 - lb) * jax.nn.sigmoid(-fp))
        put(v_ref, vv)
        cc = seg(4)
        put(sog_ref, og * jax.nn.sigmoid(og))
        u_ref[...] = cc * seg(5)

    _either(None if tail is None else _tail_tile(i, *tail), body)


def _inproj(xp, xs, g, lb, w, w_lo, tail, act_dtype, tm):
    d = D_MODEL
    assert xp.shape[0] % tm == 0 and xs.shape[0] % tm == 0 and xp.shape[1] == d
    pt = xp.shape[0] // tm
    t = xp.shape[0] + xs.shape[0]
    row = pl.BlockSpec((tm, d), lambda i: (i, 0))
    heads = pl.BlockSpec((N_HEADS, tm, HEAD_DIM), lambda i: (0, i, 0))
    ws = [w] if tail is None else [w, w_lo]
    return pl.pallas_call(
        functools.partial(_inproj_kernel, tail=tail, prompt_tiles=pt),
        grid=(t // tm,),
        in_specs=_pair_specs(tm, d, pt) + [_resident((1, d)), _resident((1, d))] + [_resident(a.shape) for a in ws],
        out_specs=[heads] * 5 + [row],
        out_shape=[jax.ShapeDtypeStruct((N_HEADS, t, HEAD_DIM), dt) for dt in (F32, F32, F32, act_dtype, act_dtype)]
        + [jax.ShapeDtypeStruct((t, d), F32)],
        compiler_params=_params(("parallel",)),
        name="inproj",
    )(xp, xs, g, lb, *ws)


def _score_masks(n_rows, block):
    xr = (lax.broadcasted_iota(I32, (n_rows, n_rows), 0)
          ^ lax.broadcasted_iota(I32, (n_rows, n_rows), 1))
    masks = {0: (xr == 0).astype(F32)}
    b = 1
    while 2 * b < n_rows and b < block:
        masks[b] = (xr < 2 * b).astype(F32)
        b *= 2
    return masks


def _level_operands(q, k, g, cum, n_rows, block):
    ri = lax.broadcasted_iota(I32, (n_rows, LANES), 0)
    b = 1
    while b < block:
        if b >= SUBLANES:
            shape3 = (n_rows // (2 * b), 2 * b, LANES)
            c3, q3, k3 = cum.reshape(shape3), q.reshape(shape3), k.reshape(shape3)
            mid = c3[:, b - 1:b, :]
            zero = jnp.zeros((shape3[0], b, LANES), F32)
            qt = jnp.concatenate([zero, q3[:, b:, :] * jnp.exp(c3[:, b:, :] - mid)], axis=1).reshape(n_rows, LANES)
            kt = jnp.concatenate([k3[:, :b, :] * jnp.exp(mid - c3[:, :b, :]), zero], axis=1).reshape(n_rows, LANES)
        else:
            upper = (ri & b) != 0
            if b == 1:
                z = jnp.where(upper, g, 0.0)
            elif b == 2:
                g_prev = pltpu.roll(g, 1, 0)
                g_next = pltpu.roll(g, n_rows - 1, 0)
                m4 = ri & 3
                z = jnp.where(m4 == 2, g, jnp.where(m4 == 3, g + g_prev, jnp.where(m4 == 0, g_next, 0.0)))
            else:
                c3 = cum.reshape(n_rows // (2 * b), 2 * b, LANES)
                mid = jnp.broadcast_to(c3[:, b - 1:b, :], c3.shape).reshape(n_rows, LANES)
                z = jnp.where(upper, cum - mid, mid - cum)
            e = jnp.exp(z)
            qt = jnp.where(upper, q * e, 0.0)
            kt = jnp.where(upper, 0.0, k * e)
        yield b, qt, kt
        b *= 2


def _intra_scores(q, k, g, cum, n_rows, block, precise, masks):
    s = masks[0] * jnp.sum(q * k, axis=1, keepdims=True)
    for b, qt, kt in _level_operands(q, k, g, cum, n_rows, block):
        prod = _mm(qt, kt, precise, _NT)
        s = s + (prod * masks[b] if b in masks else prod)
    return s


def _decay_columns(e_row):
    hi = e_row.astype(BF16).astype(F32)
    lo = e_row - hi
    r = lax.broadcasted_iota(I32, (BF16_ROWS, LANES), 0)
    stacked = jnp.where(r == 0, hi, jnp.where(r == 1, lo, 0.0)).astype(BF16)
    return _dg(stacked, jnp.ones((BF16_ROWS, LANES), BF16), _TN)


def _head_out(o, gh, sog, dtype):
    on = o * lax.rsqrt(jnp.mean(o * o, axis=-1, keepdims=True) + EPS) * gh
    return (on * sog.astype(F32)).astype(dtype)


def _scan_step_staged(qs_ref, lf_ref, kf_ref, v_ref, sog_ref, gh_ref, og_ref, s_scr,
                      cum_scr, qe_scr, ke_scr, qt_scr, kt_scr, sc_scr):
    n = SCAN_CHUNK
    rows = n * SCAN_STEP_CHUNKS
    shift = n.bit_length() - 1
    ri = lax.broadcasted_iota(I32, (rows, rows), 0)
    ci = lax.broadcasted_iota(I32, (rows, rows), 1)
    tri = (((ri >> shift) == (ci >> shift)) & (ri >= ci)).astype(BF16)
    for h in range(N_HEADS):
        cum_scr[h] = _dot_exact_lhs(tri, lf_ref[h])
    masks = _score_masks(n, n)
    chunks = [slice(j * n, (j + 1) * n) for j in range(SCAN_STEP_CHUNKS)]

    def head_lanes(h):
        if isinstance(h, int):
            return slice(h * HEAD_DIM, (h + 1) * HEAD_DIM)
        return pl.ds(pl.multiple_of(h * HEAD_DIM, HEAD_DIM), HEAD_DIM)

    def operands(h):
        hs = head_lanes(h)
        for rs in chunks:
            q, g, k, cum = qs_ref[h, rs, :], lf_ref[h, rs, :], kf_ref[h, rs, :], cum_scr[h, rs, :]
            qe_scr[h, rs, :] = (q * jnp.exp(cum)).astype(BF16)
            ke_scr[h, rs, :] = (k * jnp.exp(cum[n - 1:n, :] - cum)).astype(BF16)
            yield
            for level, (_, qt, kt) in enumerate(_level_operands(q, k, g, cum, n, n)):
                qt_scr[level, h, rs, :] = qt.astype(BF16)
                kt_scr[level, h, rs, :] = kt.astype(BF16)
                yield

    def matmuls(h):
        hs = head_lanes(h)
        for rs in chunks:
            s = masks[0] * jnp.sum(qs_ref[h, rs, :] * kf_ref[h, rs, :], axis=1, keepdims=True)
            b = 1
            for level in range(qt_scr.shape[0]):
                prod = _dg(qt_scr[level, h, rs, :], kt_scr[level, h, rs, :], _NT)
                s = s + (prod * masks[b] if b in masks else prod)
                b *= 2
                yield
            sc_scr[h, rs, :] = s.astype(BF16)
        s_old = s_scr[h]
        for rs in chunks:
            v = v_ref[h, rs, :].astype(BF16)
            o = _dg(qe_scr[h, rs, :], s_old.astype(BF16)) + _dg(sc_scr[h, rs, :], v)
            yield
            tot = cum_scr[h, rs.stop - 1:rs.stop, :]
            s_old = _decay_columns(jnp.exp(tot)) * s_old + _dg(ke_scr[h, rs, :], v, _TN)
            yield
            og_ref[rs, hs] = _head_out(o, gh_ref[:, hs], sog_ref[h, rs, :], og_ref.dtype)
            yield
        s_scr[h] = s_old

    def interleave(*phases):
        phases = list(phases)
        while phases:
            for phase in list(phases):
                if next(phase, phases) is phases:
                    phases.remove(phase)

    interleave(operands(0))
    for h in range(N_HEADS - 1):
        interleave(operands(h + 1), matmuls(h))
    interleave(matmuls(N_HEADS - 1))


def _scan_prompt_kernel(qs_ref, lf_ref, kf_ref, v_ref, sog_ref, gh_ref, og_ref, sfin_ref, zeros_ref, s_scr,
                        *stage_scr, tail_steps):
    c = pl.program_id(1)
    nc = pl.num_programs(1)
    n = SCAN_CHUNK
    zeros_ref[...] = jnp.zeros_like(zeros_ref)

    @pl.when(c == 0)
    def _():
        s_scr[...] = jnp.zeros_like(s_scr)

    def body(precise):
        if not precise:
            _scan_step_staged(qs_ref, lf_ref, kf_ref, v_ref, sog_ref, gh_ref, og_ref, s_scr, *stage_scr)
            return
        tri = (lax.broadcasted_iota(I32, (n, n), 0) >= lax.broadcasted_iota(I32, (n, n), 1)).astype(BF16)
        masks = _score_masks(n, n)

        def head(h, carry):
            hs = pl.ds(pl.multiple_of(h * HEAD_DIM, HEAD_DIM), HEAD_DIM)
            s_old = s_scr[h]
            for j in range(SCAN_STEP_CHUNKS):
                rs = slice(j * n, (j + 1) * n)
                q, g, k, v = qs_ref[h, rs, :], lf_ref[h, rs, :], kf_ref[h, rs, :], v_ref[h, rs, :]
                cum = _dot_exact_lhs(tri, g)
                tot = cum[n - 1:n, :]
                o = _mm(q * jnp.exp(cum), s_old, precise)
                o = o + _mm(_intra_scores(q, k, g, cum, n, n, precise, masks), v, precise)
                ke = k * jnp.exp(tot - cum)
                s_old = _decay_columns(jnp.exp(tot)) * s_old + _mm(ke, v, precise, _TN)
                og_ref[rs, hs] = _head_out(o, gh_ref[:, hs], sog_ref[h, rs, :], og_ref.dtype)
            s_scr[h] = s_old
            return carry

        lax.fori_loop(0, N_HEADS, head, 0)

    _either(None if tail_steps is None else c >= nc - tail_steps, body)

    @pl.when(c == nc - 1)
    def _():
        sfin_ref[0] = s_scr[...]


def _scan_prompt(qs, lf, kf, v, sog, gh, batch, seq, tail_rows):
    t = qs.shape[1]
    d = N_HEADS * HEAD_DIM
    n = SCAN_CHUNK * SCAN_STEP_CHUNKS
    assert seq % n == 0 and (tail_rows is None or tail_rows % n == 0)
    nc = seq // n
    levels = SCAN_CHUNK.bit_length() - 1
    per_head = (N_HEADS, n, HEAD_DIM)
    row = pl.BlockSpec((n, d), lambda b, c: (b * nc + c, 0))
    heads = pl.BlockSpec(per_head, lambda b, c: (0, b * nc + c, 0))
    sorted_rows = (t // MOE_TILE + 1) * MOE_SORTED_ROWS
    zrows = sorted_rows // (batch * nc)
    assert zrows * batch * nc == sorted_rows and zrows % BF16_ROWS == 0
    return pl.pallas_call(
        functools.partial(_scan_prompt_kernel, tail_steps=None if tail_rows is None else tail_rows // n),
        grid=(batch, nc),
        in_specs=[heads] * 5 + [_resident((1, d))],
        out_specs=[row, pl.BlockSpec((1, N_HEADS, HEAD_DIM, HEAD_DIM), lambda b, c: (b, 0, 0, 0)),
                   pl.BlockSpec((zrows, d), lambda b, c: (b * nc + c, 0))],
        out_shape=[jax.ShapeDtypeStruct((batch * seq, d), v.dtype),
                   jax.ShapeDtypeStruct((batch, N_HEADS, HEAD_DIM, HEAD_DIM), F32),
                   jax.ShapeDtypeStruct((sorted_rows, d), BF16)],
        scratch_shapes=[pltpu.VMEM((N_HEADS, HEAD_DIM, HEAD_DIM), F32),
                        pltpu.VMEM(per_head, F32),
                        pltpu.VMEM(per_head, BF16), pltpu.VMEM(per_head, BF16),
                        pltpu.VMEM((levels,) + per_head, BF16), pltpu.VMEM((levels,) + per_head, BF16),
                        pltpu.VMEM((N_HEADS, n, SCAN_CHUNK), BF16)],
        compiler_params=_params(("parallel", "arbitrary")),
        name="scan_prompt",
    )(qs, lf, kf, v, sog, gh)


def _scan_sample_kernel(*refs, steps, precise, layer, chained):
    if chained:
        qs_ref, lf_ref, kf_ref, v_ref, sog_ref, gh_ref, s_ref, _, og_ref, snew_ref = refs
    else:
        qs_ref, lf_ref, kf_ref, v_ref, sog_ref, gh_ref, s_ref, og_ref, all_ref = refs
        for other in range(all_ref.shape[0]):
            if other != layer:
                all_ref[other] = jnp.zeros(all_ref.shape[1:], F32)
        snew_ref = all_ref.at[layer]
    n = SAMPLE_SEQS * steps
    shift = steps.bit_length() - 1
    ri = lax.broadcasted_iota(I32, (n, n), 0)
    ci = lax.broadcasted_iota(I32, (n, n), 1)
    same = (ri >> shift) == (ci >> shift)
    tri = (same & (ri >= ci)).astype(BF16)
    tri_after = (same & (ci > ri)).astype(BF16)
    seq_of_row = lax.broadcasted_iota(I32, (n, LANES), 0) >> shift
    masks = _score_masks(n, steps)

    def head(h, carry):
        hs = pl.ds(pl.multiple_of(h * HEAD_DIM, HEAD_DIM), HEAD_DIM)
        q, g, k, v = qs_ref[h], lf_ref[h], kf_ref[h], v_ref[h]
        cum = _dot_exact_lhs(tri, g)
        after = _dot_exact_lhs(tri_after, g)
        o = _mm(_intra_scores(q, k, g, cum, n, steps, precise, masks), v, precise)
        qe = q * jnp.exp(cum)
        ke = k * jnp.exp(after)
        for s in range(SAMPLE_SEQS):
            mine = seq_of_row == s
            s_old = s_ref[s, h]
            o = o + jnp.where(mine, _mm(qe, s_old, precise), 0.0)
            kv = _mm(jnp.where(mine, ke, 0.0), v, precise, _TN)
            last = (s + 1) * steps - 1
            snew_ref[s, h] = _decay_columns(jnp.exp(cum[last:last + 1, :])) * s_old + kv
        og_ref[:, hs] = _head_out(o, gh_ref[:, hs], sog_ref[h], og_ref.dtype)
        return carry

    lax.fori_loop(0, N_HEADS, head, 0)


def _scan_sample(qs, lf, kf, v, sog, gh, states, layer, new_states, row0, steps, precise):
    d = N_HEADS * HEAD_DIM
    depth, nseq = states.shape[:2]
    assert steps & (steps - 1) == 0 and steps >= 4, "sample block must be a power of two >= 4"
    n = SAMPLE_SEQS * steps
    assert nseq % SAMPLE_SEQS == 0 and row0 % n == 0 and n % BF16_ROWS == 0
    b0 = row0 // n
    chained = new_states is not None
    row = pl.BlockSpec((N_HEADS, n, HEAD_DIM), lambda i: (0, b0 + i, 0))
    tail = (SAMPLE_SEQS, N_HEADS, HEAD_DIM, HEAD_DIM)
    st = pl.BlockSpec((None,) + tail, lambda i: (layer, i, 0, 0, 0))
    st_all = pl.BlockSpec((depth,) + tail, lambda i: (0, i, 0, 0, 0))
    extra = [new_states] if chained else []
    return pl.pallas_call(
        functools.partial(_scan_sample_kernel, steps=steps, precise=precise, layer=layer, chained=chained),
        grid=(nseq // SAMPLE_SEQS,),
        in_specs=[row, row, row, row, row, _resident((1, d)), st] + [pl.BlockSpec(memory_space=pl.ANY)] * len(extra),
        out_specs=[pl.BlockSpec((n, d), lambda i: (i, 0)), st if chained else st_all],
        out_shape=[jax.ShapeDtypeStruct((nseq * steps, d), v.dtype), jax.ShapeDtypeStruct(states.shape, F32)],
        input_output_aliases={7: 1} if chained else {},
        compiler_params=_params(("parallel",)),
        name="scan_sample",
    )(qs, lf, kf, v, sog, gh, states, *extra)


def _mix_kernel(*refs, tm, tiles_per_seq, prompt_tiles, steps, tail):
    if tail is None:
        (xp_ref, xs_ref, ogp_ref, ogs_ref, u_ref, halo_ref, e1_ref, e2_ref, g_ref, wc_ref, gf_ref, wr_ref, br_ref,
         w_cbg_ref, wa_ref, wb_ref, wo_ref, x1_ref, sorted_ref, rinfo_ref, cinfo_ref, prev_scr) = refs
        l_cbg_ref = la_ref = lb_ref = lo_ref = None
    else:
        (xp_ref, xs_ref, ogp_ref, ogs_ref, u_ref, halo_ref, e1_ref, e2_ref, g_ref, wc_ref, gf_ref, wr_ref, br_ref,
         w_cbg_ref, wa_ref, wb_ref, wo_ref, l_cbg_ref, la_ref, lb_ref, lo_ref,
         x1_ref, sorted_ref, rinfo_ref, cinfo_ref, prev_scr) = refs
    d = D_MODEL
    i = pl.program_id(0)

    @pl.when(i == 0)
    def _():
        prev_scr[...] = jnp.zeros_like(prev_scr)

    def full(ref):
        return None if ref is None else ref[...]

    def body(precise):
        x = _pair_load(i, prompt_tiles, xp_ref, xs_ref)
        h = _rmsnorm(x, g_ref[...])
        u = u_ref[...]
        ri = lax.broadcasted_iota(I32, (tm, d), 0)
        r1 = pltpu.roll(u, 1, 0)
        r2 = pltpu.roll(u, 2, 0)
        keep = jnp.where(i % tiles_per_seq == 0, 0.0, 1.0)
        halo = halo_ref[...]
        h_last = halo[7:8, :] * keep
        h_prev = halo[6:7, :] * keep
        is_sample = i >= prompt_tiles
        t_in_seq = ri & (steps - 1)
        prev1 = jnp.where(is_sample,
                          jnp.where(t_in_seq == 0, e1_ref[...], r1),
                          jnp.where(ri == 0, h_last, r1))
        prev2 = jnp.where(is_sample,
                          jnp.where(t_in_seq < 2, e2_ref[...], r2),
                          jnp.where(ri == 0, h_prev, jnp.where(ri == 1, h_last, r2)))
        wc = wc_ref[...]
        conv = prev2 * wc[0:1, :] + prev1 * wc[1:2, :] + u * wc[2:3, :]

        def seg(j):
            sl = slice(j * d, (j + 1) * d)
            return _mmw(h, w_cbg_ref[:, sl], _w_slice(l_cbg_ref, sl), precise)

        router = _route_stages(prev_scr[...], gf_ref[...], wr_ref[...], br_ref[...],
                               sorted_ref, rinfo_ref, cinfo_ref)
        cb = seg(0)
        next(router)
        y_b = _mmw(cb * conv, wb_ref[...], full(lb_ref), precise)
        next(router)
        y_a = _mmw(_pair_load(i, prompt_tiles, ogp_ref, ogs_ref), wa_ref[...], full(la_ref), precise)
        next(router)
        ga = seg(1)
        next(router)
        gb = seg(2)
        next(router)
        m = jax.nn.sigmoid(ga) * y_a + jax.nn.sigmoid(gb) * y_b
        x1 = x + _mmw(m, wo_ref[...], full(lo_ref), precise)
        for _ in router:
            pass
        x1_ref[...] = x1
        prev_scr[...] = x1

    _either(None if tail is None else _tail_tile(i, *tail), body)


def _mix(xp, xs, ogp, ogs, u, e1, e2, g, wc, g_ffn, wr, br, ws, ws_lo, seq, steps, tail):
    t, d = u.shape
    tm = MOE_TILE
    prompt_rows = xp.shape[0]
    assert seq % tm == 0 and prompt_rows % tm == 0 and (t - prompt_rows) % tm == 0 and tm % steps == 0
    pt = prompt_rows // tm
    nt = t // tm
    cur = lambda i: jnp.minimum(i, nt - 1)
    prev = lambda i: jnp.maximum(i - 1, 0)
    prompt = pl.BlockSpec((tm, d), lambda i: (jnp.minimum(i, pt - 1), 0))
    sample = pl.BlockSpec((tm, d), lambda i: (jnp.clip(i - pt, 0, nt - pt - 1), 0))
    row = pl.BlockSpec((tm, d), lambda i: (cur(i), 0))
    halo = pl.BlockSpec((SUBLANES, d), lambda i: (jnp.maximum(cur(i) * (tm // SUBLANES) - 1, 0), 0))
    weights = list(ws) + ([] if tail is None else list(ws_lo))
    return pl.pallas_call(
        functools.partial(_mix_kernel, tm=tm, tiles_per_seq=seq // tm, prompt_tiles=pt, steps=steps, tail=tail),
        grid=(nt + 1,),
        in_specs=[prompt, sample, prompt, sample, row, halo, sample, sample, _resident((1, d)), _resident(wc.shape),
                  _resident((1, d)), _resident(wr.shape), _resident(br.shape)]
        + [_resident(a.shape) for a in weights],
        out_specs=[row,
                   pl.BlockSpec((1, MOE_SORTED_ROWS, d), lambda i: (prev(i), 0, 0)),
                   pl.BlockSpec((tm, LANES), lambda i: (prev(i), 0)),
                   pl.BlockSpec((1, SUBLANES, LANES), lambda i: (prev(i), 0, 0))],
        out_shape=[jax.ShapeDtypeStruct((t, d), F32),
                   jax.ShapeDtypeStruct((nt, MOE_SORTED_ROWS, d), BF16),
                   jax.ShapeDtypeStruct((t, LANES), F32),
                   jax.ShapeDtypeStruct((nt, SUBLANES, LANES), I32)],
        scratch_shapes=[pltpu.VMEM((tm, d), F32)],
        compiler_params=_params(("arbitrary",)),
        name="mix_route",
    )(xp, xs, ogp, ogs, u, u, e1, e2, g, wc, g_ffn, wr, br, *weights)


def _route_stages(x, g, wr, br, xs_ref, rinfo_ref, cinfo_ref):
    tm = MOE_TILE
    rows = MOE_SORTED_ROWS
    hn = _rmsnorm(x, g)
    hn_hi, hn_lo = _split2(hn)
    wr_hi, wr_lo = _split2(wr)
    logits = _dg(hn_hi, wr_hi) + _dg(hn_lo, wr_hi) + _dg(hn_hi, wr_lo) + br
    yield

    lane = lax.broadcasted_iota(I32, (tm, LANES), 1)
    neg = -jnp.inf
    big = jnp.int32(LANES)

    def first_lane(mask):
        return jnp.min(jnp.where(mask, lane, big), axis=1, keepdims=True)

    is_group = (lane >= GROUP_LANE0) & (lane < GROUP_LANE0 + N_GROUPS)
    gl = jnp.where(is_group, logits, neg)
    gmax = jnp.max(gl, axis=1, keepdims=True)
    g_idx = first_lane(gl == gmax) - GROUP_LANE0
    g_w = 1.0 / jnp.sum(jnp.exp(gl - gmax), axis=1, keepdims=True)
    yield

    in_group = (lane < N_EXPERTS) & ((lane >> (EXPERTS_PER_GROUP.bit_length() - 1)) == g_idx)
    el = jnp.where(in_group, logits, neg)
    emax = jnp.max(el, axis=1, keepdims=True)
    ee = jnp.exp(el - emax)
    prob = ee / jnp.sum(ee, axis=1, keepdims=True)
    prob = jnp.where(in_group, prob, -1.0)
    yield
    p1 = jnp.max(prob, axis=1, keepdims=True)
    i1 = first_lane(prob == p1)
    prob2 = jnp.where(lane == i1, -1.0, prob)
    p2 = jnp.max(prob2, axis=1, keepdims=True)
    i2 = first_lane(prob2 == p2)
    psum = p1 + p2
    w1 = p1 / psum * g_w
    w2 = p2 / psum * g_w
    yield

    oh1 = (lane == i1).astype(BF16)
    oh2 = (lane == i2).astype(BF16)
    before = (lax.broadcasted_iota(I32, (tm, tm), 1) < lax.broadcasted_iota(I32, (tm, tm), 0)).astype(BF16)
    c1 = _dg(before, oh1)
    c2 = _dg(before, oh2)
    oh1f = oh1.astype(F32)
    oh2f = oh2.astype(F32)
    cnt1 = jnp.sum(oh1f, axis=0, keepdims=True)
    cnt2 = jnp.sum(oh2f, axis=0, keepdims=True)
    chunks = jnp.floor((cnt1 + cnt2 + (BF16_ROWS - 1)) * (1.0 / BF16_ROWS))
    excl = (lax.broadcasted_iota(I32, (LANES, LANES), 0) < lax.broadcasted_iota(I32, (LANES, LANES), 1)).astype(BF16)
    chunk0 = _dg(jnp.broadcast_to(chunks, (BF16_ROWS, LANES)).astype(BF16), excl)[0:1, :]
    yield
    base = chunk0 * BF16_ROWS
    pos1 = jnp.sum(oh1f * (base + c1), axis=1, keepdims=True)
    pos2 = jnp.sum(oh2f * (base + cnt1 + c2), axis=1, keepdims=True)

    slot = lax.broadcasted_iota(I32, (tm, rows), 1)
    place = ((slot == pos1.astype(I32)) | (slot == pos2.astype(I32))).astype(BF16)
    xs_ref[0] = _dg(place, hn_hi, _TN).astype(BF16)

    rinfo_ref[...] = jnp.where(lane == 0, pos1, jnp.where(lane == 1, pos2, jnp.where(lane == 2, w1, jnp.where(lane == 3, w2, 0.0))))
    r8 = lax.broadcasted_iota(I32, (SUBLANES, LANES), 0)
    cinfo_ref[0] = jnp.where(r8 == 0, chunk0, jnp.where(r8 == 1, chunks, 0.0)).astype(I32)


def _ffn_kernel(order_ref, first_ref, left_ref, blk0_ref, xs_hbm, wg_hbm, wu_hbm, wd_hbm, ys_in_hbm, ys_hbm,
                lhs_buf, res_buf, wg_stage, wu_stage, wd_stage, wgu_bf, wd_bf, in_sem, out_sem, w_sem, *, layer):
    del ys_in_hbm
    nb = MOE_BLOCK_CHUNKS
    rows = BF16_ROWS
    nblocks = blk0_ref[N_EXPERTS]

    spare0 = xs_hbm.shape[0]

    def gather_start(blk, slot):
        first, left = first_ref[blk], left_ref[blk]
        for i in range(nb):
            src = order_ref[first + jnp.where(i < left, i, 0)]
            pltpu.make_async_copy(xs_hbm.at[src], lhs_buf.at[slot, pl.ds(i * rows, rows)], in_sem.at[slot]).start()

    def scatter_start(blk, slot):
        first, left = first_ref[blk], left_ref[blk]
        for i in range(nb):
            dst = jnp.where(i < left, order_ref[first + jnp.where(i < left, i, 0)], spare0 + slot * nb + i)
            pltpu.make_async_copy(res_buf.at[slot, pl.ds(i * rows, rows)], ys_hbm.at[dst], out_sem.at[slot]).start()

    def gather_wait(slot):
        pltpu.make_async_copy(lhs_buf.at[slot], lhs_buf.at[slot], in_sem.at[slot]).wait()

    def scatter_wait(slot):
        pltpu.make_async_copy(res_buf.at[slot], res_buf.at[slot], out_sem.at[slot]).wait()

    def weight_copies(e, slot):
        return [pltpu.make_async_copy(src.at[layer, e], dst.at[slot], w_sem.at[slot])
                for src, dst in ((wg_hbm, wg_stage), (wu_hbm, wu_stage), (wd_hbm, wd_stage))]

    for cp in weight_copies(0, 0):
        cp.start()

    ahead = FFN_GATHER_DEPTH - 1

    @pl.when(nblocks > 0)
    def _():
        for j in range(ahead):
            gather_start(jnp.minimum(j, nblocks - 1), j)

    def expert(e, carry):
        ws = e % 2
        for cp in weight_copies(e, ws):
            cp.wait()
        wgu_bf[:, :D_EXPERT] = wg_stage[ws].astype(BF16)
        wgu_bf[:, D_EXPERT:] = wu_stage[ws].astype(BF16)
        wd_bf[...] = wd_stage[ws].astype(BF16)

        @pl.when(e + 1 < N_EXPERTS)
        def _():
            for cp in weight_copies(e + 1, 1 - ws):
                cp.start()

        def block(blk, c):
            slot = blk % 2

            @pl.when(blk >= 2)
            def _():
                scatter_wait(slot)

            gslot = blk % FFN_GATHER_DEPTH
            gather_wait(gslot)
            gather_start(jnp.minimum(blk + ahead, nblocks - 1), (blk + ahead) % FFN_GATHER_DEPTH)
            gu = _dg(lhs_buf[gslot], wgu_bf[...])
            gate = gu[:, :D_EXPERT]
            hid = gate * jax.nn.sigmoid(gate) * gu[:, D_EXPERT:]
            res_buf[slot] = _dg(hid.astype(BF16), wd_bf[...]).astype(BF16)
            scatter_start(blk, slot)
            return c

        lax.fori_loop(blk0_ref[e], blk0_ref[e + 1], block, 0)
        return carry

    lax.fori_loop(0, N_EXPERTS, expert, 0)

    @pl.when(nblocks > 0)
    def _():
        for j in range(ahead):
            gather_wait((nblocks + j) % FFN_GATHER_DEPTH)

    for back in (2, 1):
        @pl.when(nblocks >= back)
        def _():
            blk = nblocks - back
            scatter_wait(blk % 2)


def _ffn(xs, zeros, order, first, left, blk0, w_gate, w_up, w_down, layer):
    nt, srows, d = xs.shape
    assert zeros.shape == (nt + 1, srows, d) and 2 * MOE_BLOCK_CHUNKS <= MOE_CHUNKS
    chunked = (nt * MOE_CHUNKS, BF16_ROWS, d)
    chunked_out = ((nt + 1) * MOE_CHUNKS, BF16_ROWS, d)
    brows = MOE_BLOCK_CHUNKS * BF16_ROWS
    anywhere = pl.BlockSpec(memory_space=pl.ANY)
    f = w_gate.shape[-1]
    ys = pl.pallas_call(
        functools.partial(_ffn_kernel, layer=layer),
        grid_spec=pltpu.PrefetchScalarGridSpec(
            num_scalar_prefetch=4,
            grid=(1,),
            in_specs=[anywhere] * 5,
            out_specs=anywhere,
            scratch_shapes=[pltpu.VMEM((FFN_GATHER_DEPTH, brows, d), BF16), pltpu.VMEM((2, brows, d), BF16),
                            pltpu.VMEM((2, d, f), F32), pltpu.VMEM((2, d, f), F32), pltpu.VMEM((2, f, d), F32),
                            pltpu.VMEM((d, 2 * f), BF16), pltpu.VMEM((f, d), BF16),
                            pltpu.SemaphoreType.DMA((FFN_GATHER_DEPTH,)), pltpu.SemaphoreType.DMA((2,)),
                            pltpu.SemaphoreType.DMA((2,))],
        ),
        out_shape=jax.ShapeDtypeStruct(chunked_out, BF16),
        input_output_aliases={8: 0},
        compiler_params=_params(("arbitrary",)),
        name="expert_ffn",
    )(order, first, left, blk0, xs.reshape(chunked), w_gate, w_up, w_down, zeros.reshape(chunked_out))
    return ys.reshape(nt + 1, srows, d)


def _block_tables(cinfo):
    nt = cinfo.shape[0]
    nb = MOE_BLOCK_CHUNKS
    nchunks = nt * MOE_CHUNKS
    chunk0 = cinfo[:, 0, :N_EXPERTS]
    chunks = cinfo[:, 1, :N_EXPERTS]
    ends = chunk0 + chunks
    c = jnp.arange(MOE_CHUNKS, dtype=I32)
    label = jnp.sum((c[None, :, None] >= ends[:, None, :]).astype(I32), axis=-1)
    order = jnp.argsort(label.reshape(-1), stable=True).astype(I32)
    cnt = chunks.sum(axis=0).astype(I32)
    cstart = jnp.cumsum(cnt) - cnt
    nblk = (cnt + nb - 1) // nb
    blk0 = jnp.concatenate([jnp.zeros((1,), I32), jnp.cumsum(nblk).astype(I32)])
    max_blocks = nchunks // nb + N_EXPERTS
    j = jnp.arange(max_blocks, dtype=I32)
    ej = jnp.minimum(jnp.sum((j[:, None] >= blk0[None, 1:]).astype(I32), axis=1), N_EXPERTS - 1)
    mine = (ej[:, None] == jnp.arange(N_EXPERTS, dtype=I32)[None, :]).astype(I32)
    local = j - jnp.sum(mine * blk0[None, :N_EXPERTS], axis=1)
    first = jnp.sum(mine * cstart[None, :], axis=1) + local * nb
    left = jnp.sum(mine * cnt[None, :], axis=1) - local * nb
    return order, jnp.clip(first, 0, nchunks - 1).astype(I32), jnp.maximum(left, 0).astype(I32), blk0


def _ple_kernel(x1_ref, ys_ref, rinfo_ref, pp_ref, ps_ref, g_ref, wg_ref, wp_ref, gfin_ref, outp_ref, outs_ref, *,
                final, prompt_tiles):
    tm = MOE_TILE
    rows = MOE_SORTED_ROWS
    i = pl.program_id(0)
    rinfo = rinfo_ref[...]
    pos1 = rinfo[:, 0:1].astype(I32)
    pos2 = rinfo[:, 1:2].astype(I32)
    w1 = rinfo[:, 2:3]
    w2 = rinfo[:, 3:4]
    slot = lax.broadcasted_iota(I32, (tm, rows), 1)
    ys = ys_ref[0]
    y1 = _dg((slot == pos1).astype(BF16), ys)
    y2 = _dg((slot == pos2).astype(BF16), ys)
    x = x1_ref[...] + (w1 * y1 + w2 * y2)
    hn = _rmsnorm(x, g_ref[...]).astype(BF16)
    proj = _dg(_pair_load(i, prompt_tiles, pp_ref, ps_ref).astype(BF16), wp_ref[...])
    x = x + jax.nn.sigmoid(_dg(hn, wg_ref[...])) * proj
    if final:
        x = _rmsnorm(x, gfin_ref[...])

    @pl.when(i < prompt_tiles)
    def _():
        outp_ref[...] = x

    @pl.when(i >= prompt_tiles)
    def _():
        outs_ref[...] = x


def _ple(x1, ys, rinfo, p_prompt, p_sample, layer, g, wg, wp, gfin, final):
    t, d = x1.shape
    tm = MOE_TILE
    nt = t // tm
    tp, ts = p_prompt.shape[1], p_sample.shape[1]
    assert tp % tm == 0 and ts % tm == 0 and tp + ts == t
    pt = tp // tm
    pd = p_prompt.shape[2]
    ys = ys.reshape(-1, MOE_SORTED_ROWS, d)
    return pl.pallas_call(
        functools.partial(_ple_kernel, final=final, prompt_tiles=pt),
        grid=(nt,),
        in_specs=[pl.BlockSpec((tm, d), lambda i: (i, 0)),
                  pl.BlockSpec((1, MOE_SORTED_ROWS, d), lambda i: (i, 0, 0)),
                  pl.BlockSpec((tm, LANES), lambda i: (i, 0)),
                  pl.BlockSpec((None, tm, pd), lambda i: (layer, jnp.minimum(i, pt - 1), 0)),
                  pl.BlockSpec((None, tm, pd), lambda i: (layer, jnp.maximum(i - pt, 0), 0)),
                  _resident((1, d)), _resident(wg.shape), _resident(wp.shape), _resident((1, d))],
        out_specs=_pair_specs(tm, d, pt),
        out_shape=[jax.ShapeDtypeStruct((tp, d), F32), jax.ShapeDtypeStruct((ts, d), F32)],
        compiler_params=_params(("arbitrary",)),
        name="ple",
    )(x1, ys, rinfo, p_prompt, p_sample, g, wg, wp, gfin)


def _lower_bounds(lb_param):
    s = jax.nn.softmax(lb_param.astype(F32), axis=0)
    c = jnp.cumsum(s, axis=0)
    return c - c[0:1]


def _split_kernel(w_ref, hi_ref, lo_ref):
    hi_ref[...], lo_ref[...] = _split2(w_ref[...])


def _hi_lo(w):
    k, n = w.shape
    tk = ROW_TILE
    assert k % tk == 0
    spec = pl.BlockSpec((tk, n), lambda i: (i, 0))
    return pl.pallas_call(
        _split_kernel,
        grid=(k // tk,),
        in_specs=[spec],
        out_specs=[spec, spec],
        out_shape=[jax.ShapeDtypeStruct((k, n), BF16)] * 2,
        compiler_params=_params(("parallel",)),
        name="split_weight",
    )(w)


def kernel(x_prompt, x_sample, state_hgrn, state_conv, p_prompt, p_sample, g_mix, w_in, hg_lower, g_hg_out, w_br_a, w_conv, w_br_b, w_out, g_ffn, w_router_group, b_router_group, w_router_expert, b_router_expert, w_gate, w_up, w_down, g_ple, w_ple_gate, w_ple_proj, g_final):
    depth = w_in.shape[0]
    bp, seq, d = x_prompt.shape
    bs, steps, _ = x_sample.shape
    assert d == D_MODEL and w_conv.shape[1] == CONV_W
    assert TAIL_ROWS % ROW_TILE == 0 and seq >= TAIL_ROWS
    tp = bp * seq
    ts = bs * steps
    hw = N_HEADS * HEAD_DIM

    xp, xs_rows = x_prompt.reshape(tp, d), x_sample.reshape(ts, d)
    pp = p_prompt.reshape(depth, tp, -1)
    ps = p_sample.reshape(depth, ts, -1)
    lbs = _lower_bounds(hg_lower)
    row = lambda a: a.reshape(1, -1)

    hg_p, cv_p, cv_s = [], [], []
    hg_s = None
    for li in range(depth):
        careful = li < depth - 1
        tm = ROW_TILE if careful else FAST_ROW_TILE
        tail = (seq // tm, TAIL_ROWS // tm, tp // tm) if careful else None
        act = F32 if careful else BF16
        wi = w_in[li]
        w_scan = jnp.concatenate([wi[:, :4 * hw], wi[:, 4 * hw + d:4 * hw + 3 * d]], axis=1)
        w_cbg = jnp.concatenate([wi[:, 4 * hw:4 * hw + d], wi[:, 4 * hw + 3 * d:]], axis=1)
        mix_w = (w_cbg, w_br_a[li], w_br_b[li], w_out[li])
        if careful:
            w_scan, w_scan_lo = _hi_lo(w_scan)
            mix_w, mix_w_lo = zip(*[_hi_lo(w) for w in mix_w])
        else:
            w_scan, w_scan_lo = w_scan.astype(BF16), None
            mix_w, mix_w_lo = [w.astype(BF16) for w in mix_w], None

        qs, lf, kf, v, sog, u = _inproj(xp, xs_rows, row(g_mix[li]), row(lbs[li]), w_scan, w_scan_lo, tail, act, tm)

        gh = row(g_hg_out[li])
        ogp, s_p, zeros = _scan_prompt(qs, lf, kf, v, sog, gh, bp, seq, TAIL_ROWS if careful else None)
        ogs, hg_s = _scan_sample(qs, lf, kf, v, sog, gh, state_hgrn, li, hg_s, tp, steps, careful)
        hg_p.append(s_p)
        keep = CONV_W - 1
        cv_p.append(jnp.stack([lax.slice_in_dim(u, (b + 1) * seq - keep, (b + 1) * seq) for b in range(bp)]))
        cv_s.append(jnp.concatenate([state_conv[li], lax.slice_in_dim(u, tp, tp + ts).reshape(bs, steps, d)],
                                    axis=1)[:, steps:])

        buf = state_conv[li]
        zero = jnp.zeros((bs, steps - 1, d), F32)
        e1 = jnp.concatenate([buf[:, 1:2], zero], axis=1).reshape(ts, d)
        e2 = jnp.concatenate([buf, zero[:, 1:]], axis=1).reshape(ts, d)
        wr = jnp.zeros((d, LANES), F32).at[:, :N_EXPERTS].set(w_router_expert[li])
        wr = wr.at[:, GROUP_LANE0:GROUP_LANE0 + N_GROUPS].set(w_router_group[li])
        br = jnp.zeros((1, LANES), F32).at[0, :N_EXPERTS].set(b_router_expert[li])
        br = br.at[0, GROUP_LANE0:GROUP_LANE0 + N_GROUPS].set(b_router_group[li])
        mix_tail = (seq // MOE_TILE, TAIL_ROWS // MOE_TILE, tp // MOE_TILE) if careful else None
        x1, xsort, rinfo, cinfo = _mix(xp, xs_rows, ogp, ogs, u, e1, e2, row(g_mix[li]), w_conv[li],
                                       row(g_ffn[li]), wr, br, mix_w, mix_w_lo, seq, steps, mix_tail)
        ys = _ffn(xsort, zeros.reshape(-1, MOE_SORTED_ROWS, d), *_block_tables(cinfo), w_gate, w_up, w_down, li)

        xp, xs_rows = _ple(x1, ys, rinfo, pp, ps, li, row(g_ple[li]), w_ple_gate[li].astype(BF16),
                           w_ple_proj[li].astype(BF16), row(g_final), li == depth - 1)

    return (xp.reshape(bp, seq, d), xs_rows.reshape(bs, steps, d), jnp.stack(hg_p), jnp.stack(cv_p), hg_s,
            jnp.stack(cv_s))
```

```python
import functools

import jax
import jax.numpy as jnp
from jax import lax
from jax.experimental import pallas as pl
from jax.experimental.pallas import tpu as pltpu

F32 = jnp.float32
BF16 = jnp.bfloat16
I32 = jnp.int32

D_MODEL = 1024
N_HEADS = 8
HEAD_DIM = 128
CONV_W = 3
N_GROUPS = 4
EXPERTS_PER_GROUP = 8
N_EXPERTS = N_GROUPS * EXPERTS_PER_GROUP
D_EXPERT = 256
EPS = 1e-6

LANES = 128
SUBLANES = 8
BF16_ROWS = 16
VMEM_LIMIT = 56 * 1024 * 1024

ROW_TILE = 256
FAST_ROW_TILE = 512
SCAN_CHUNK = 128
SCAN_STEP_CHUNKS = 2
SCAN_HEAD_UNROLL = 2
TAIL_ROWS = 256
SAMPLE_SEQS = 16
MOE_TILE = 256
MOE_SORTED_ROWS = 1024
MOE_CHUNKS = MOE_SORTED_ROWS // BF16_ROWS
MOE_BLOCK_CHUNKS = 32
FFN_GATHER_DEPTH = 6
GROUP_LANE0 = N_EXPERTS

_NN = (((1,), (0,)), ((), ()))
_NT = (((1,), (1,)), ((), ()))
_TN = (((0,), (0,)), ((), ()))


def _params(sem):
    return pltpu.CompilerParams(dimension_semantics=sem, vmem_limit_bytes=VMEM_LIMIT)


def _resident(shape):
    nd = len(shape)
    return pl.BlockSpec(shape, lambda *_: (0,) * nd, pipeline_mode=pl.Buffered(1))


def _rmsnorm(x, g):
    r = lax.rsqrt(jnp.mean(x * x, axis=-1, keepdims=True) + EPS)
    return (x * r) * g


def _dg(a, b, dims=_NN):
    return lax.dot_general(a, b, dims, preferred_element_type=F32)


def _split2(x):
    hi = x.astype(BF16)
    return hi, (x - hi.astype(F32)).astype(BF16)


def _mm(a, b, precise, dims=_NN):
    if not precise:
        return _dg(a.astype(BF16), b.astype(BF16), dims)
    ah, al = _split2(a.astype(F32))
    bh, bl = _split2(b.astype(F32))
    return _dg(ah, bh, dims) + (_dg(al, bh, dims) + _dg(ah, bl, dims))


def _mmw(a, w_hi, w_lo, precise):
    if not precise:
        return _dg(a.astype(BF16), w_hi)
    ah, al = _split2(a.astype(F32))
    return _dg(ah, w_hi) + (_dg(al, w_hi) + _dg(ah, w_lo))


def _split3(x):
    hi = x.astype(BF16)
    r1 = x - hi.astype(F32)
    mid = r1.astype(BF16)
    lo = (r1 - mid.astype(F32)).astype(BF16)
    return hi, mid, lo


def _dot_exact_lhs(m01, x):
    hi, mid, lo = _split3(x)
    return _dg(m01, hi) + _dg(m01, mid) + _dg(m01, lo)


def _either(precise_pred, body):
    if precise_pred is None:
        body(False)
    else:
        pl.when(precise_pred)(lambda: body(True))
        pl.when(jnp.logical_not(precise_pred))(lambda: body(False))


def _tail_tile(i, tiles_per_seq, tail_tiles, prompt_tiles):
    return (i >= prompt_tiles) | ((i % tiles_per_seq) >= tiles_per_seq - tail_tiles)


def _pair_specs(tm, cols, prompt_tiles):
    return [pl.BlockSpec((tm, cols), lambda i: (jnp.minimum(i, prompt_tiles - 1), 0)),
            pl.BlockSpec((tm, cols), lambda i: (jnp.maximum(i - prompt_tiles, 0), 0))]


def _pair_load(i, prompt_tiles, p_ref, s_ref):
    return jnp.where(i < prompt_tiles, p_ref[...], s_ref[...])


def _w_slice(ref, sl):
    return None if ref is None else ref[:, sl]


def _inproj_kernel(*refs, tail, prompt_tiles):
    if tail is None:
        xp_ref, xs_ref, g_ref, lb_ref, w_ref, qs_ref, lf_ref, kf_ref, v_ref, sog_ref, u_ref = refs
        wl_ref = None
    else:
        xp_ref, xs_ref, g_ref, lb_ref, w_ref, wl_ref, qs_ref, lf_ref, kf_ref, v_ref, sog_ref, u_ref = refs
    d = D_MODEL
    i = pl.program_id(0)

    def body(precise):
        h = _rmsnorm(_pair_load(i, prompt_tiles, xp_ref, xs_ref), g_ref[...])

        def seg(j):
            sl = slice(j * d, (j + 1) * d)
            return _mmw(h, w_ref[:, sl], _w_slice(wl_ref, sl), precise)

        def put(ref, val):
            for hd in range(N_HEADS):
                ref[hd] = val[:, hd * HEAD_DIM:(hd + 1) * HEAD_DIM].astype(ref.dtype)

        q = seg(0)
        fp = seg(1)
        put(qs_ref, q * jax.nn.sigmoid(q))
        vv = seg(2)
        lb = lb_ref[...]
        log_sig = jnp.minimum(fp, 0.0) - jnp.log1p(jnp.exp(-jnp.abs(fp)))
        a = jnp.log(lb)
        b = jnp.log1p(-lb) + log_sig
        put(lf_ref, jnp.maximum(a, b) + jnp.log1p(jnp.exp(-jnp.abs(a - b))))
        og = seg(3)
        put(kf_ref, (1.0 - lb) * jax.nn.sigmoid(-fp))
        put(v_ref, vv)
        cc = seg(4)
        put(sog_ref, og * jax.nn.sigmoid(og))
        u_ref[...] = cc * seg(5)

    _either(None if tail is None else _tail_tile(i, *tail), body)


def _inproj(xp, xs, g, lb, w, w_lo, tail, act_dtype, tm):
    d = D_MODEL
    assert xp.shape[0] % tm == 0 and xs.shape[0] % tm == 0 and xp.shape[1] == d
    pt = xp.shape[0] // tm
    t = xp.shape[0] + xs.shape[0]
    row = pl.BlockSpec((tm, d), lambda i: (i, 0))
    heads = pl.BlockSpec((N_HEADS, tm, HEAD_DIM), lambda i: (0, i, 0))
    ws = [w] if tail is None else [w, w_lo]
    return pl.pallas_call(
        functools.partial(_inproj_kernel, tail=tail, prompt_tiles=pt),
        grid=(t // tm,),
        in_specs=_pair_specs(tm, d, pt) + [_resident((1, d)), _resident((1, d))] + [_resident(a.shape) for a in ws],
        out_specs=[heads] * 5 + [row],
        out_shape=[jax.ShapeDtypeStruct((N_HEADS, t, HEAD_DIM), dt) for dt in (F32, F32, F32, act_dtype, act_dtype)]
        + [jax.ShapeDtypeStruct((t, d), F32)],
        compiler_params=_params(("parallel",)),
        name="inproj",
    )(xp, xs, g, lb, *ws)


def _score_masks(n_rows, block):
    xr = (lax.broadcasted_iota(I32, (n_rows, n_rows), 0)
          ^ lax.broadcasted_iota(I32, (n_rows, n_rows), 1))
    masks = {0: (xr == 0).astype(F32)}
    b = 1
    while 2 * b < n_rows and b < block:
        masks[b] = (xr < 2 * b).astype(F32)
        b *= 2
    return masks


def _level_operands(q, k, g, cum, n_rows, block):
    ri = lax.broadcasted_iota(I32, (n_rows, LANES), 0)
    b = 1
    while b < block:
        if b >= SUBLANES:
            shape3 = (n_rows // (2 * b), 2 * b, LANES)
            c3, q3, k3 = cum.reshape(shape3), q.reshape(shape3), k.reshape(shape3)
            mid = c3[:, b - 1:b, :]
            zero = jnp.zeros((shape3[0], b, LANES), F32)
            qt = jnp.concatenate([zero, q3[:, b:, :] * jnp.exp(c3[:, b:, :] - mid)], axis=1).reshape(n_rows, LANES)
            kt = jnp.concatenate([k3[:, :b, :] * jnp.exp(mid - c3[:, :b, :]), zero], axis=1).reshape(n_rows, LANES)
        else:
            upper = (ri & b) != 0
            if b == 1:
                z = jnp.where(upper, g, 0.0)
            elif b == 2:
                g_prev = pltpu.roll(g, 1, 0)
                g_next = pltpu.roll(g, n_rows - 1, 0)
                m4 = ri & 3
                z = jnp.where(m4 == 2, g, jnp.where(m4 == 3, g + g_prev, jnp.where(m4 == 0, g_next, 0.0)))
            else:
                c3 = cum.reshape(n_rows // (2 * b), 2 * b, LANES)
                mid = jnp.broadcast_to(c3[:, b - 1:b, :], c3.shape).reshape(n_rows, LANES)
                z = jnp.where(upper, cum - mid, mid - cum)
            e = jnp.exp(z)
            qt = jnp.where(upper, q * e, 0.0)
            kt = jnp.where(upper, 0.0, k * e)
        yield b, qt, kt
        b *= 2


def _intra_scores(q, k, g, cum, n_rows, block, precise, masks):
    s = masks[0] * jnp.sum(q * k, axis=1, keepdims=True)
    for b, qt, kt in _level_operands(q, k, g, cum, n_rows, block):
        prod = _mm(qt, kt, precise, _NT)
        s = s + (prod * masks[b] if b in masks else prod)
    return s


def _decay_columns(e_row):
    hi = e_row.astype(BF16).astype(F32)
    lo = e_row - hi
    r = lax.broadcasted_iota(I32, (BF16_ROWS, LANES), 0)
    stacked = jnp.where(r == 0, hi, jnp.where(r == 1, lo, 0.0)).astype(BF16)
    return _dg(stacked, jnp.ones((BF16_ROWS, LANES), BF16), _TN)


def _head_out(o, gh, sog, dtype):
    on = o * lax.rsqrt(jnp.mean(o * o, axis=-1, keepdims=True) + EPS) * gh
    return (on * sog.astype(F32)).astype(dtype)


def _scan_step_staged(qs_ref, lf_ref, kf_ref, v_ref, sog_ref, gh_ref, og_ref, s_scr,
                      cum_scr, qe_scr, ke_scr, qt_scr, kt_scr, sc_scr):
    n = SCAN_CHUNK
    rows = n * SCAN_STEP_CHUNKS
    shift = n.bit_length() - 1
    ri = lax.broadcasted_iota(I32, (rows, rows), 0)
    ci = lax.broadcasted_iota(I32, (rows, rows), 1)
    tri = (((ri >> shift) == (ci >> shift)) & (ri >= ci)).astype(BF16)
    for h in range(N_HEADS):
        cum_scr[h] = _dot_exact_lhs(tri, lf_ref[h])
    masks = _score_masks(n, n)
    chunks = [slice(j * n, (j + 1) * n) for j in range(SCAN_STEP_CHUNKS)]

    def head_lanes(h):
        if isinstance(h, int):
            return slice(h * HEAD_DIM, (h + 1) * HEAD_DIM)
        return pl.ds(pl.multiple_of(h * HEAD_DIM, HEAD_DIM), HEAD_DIM)

    def operands(h):
        hs = head_lanes(h)
        for rs in chunks:
            q, g, k, cum = qs_ref[h, rs, :], lf_ref[h, rs, :], kf_ref[h, rs, :], cum_scr[h, rs, :]
            qe_scr[h, rs, :] = (q * jnp.exp(cum)).astype(BF16)
            ke_scr[h, rs, :] = (k * jnp.exp(cum[n - 1:n, :] - cum)).astype(BF16)
            yield
            for level, (_, qt, kt) in enumerate(_level_operands(q, k, g, cum, n, n)):
                qt_scr[level, h, rs, :] = qt.astype(BF16)
                kt_scr[level, h, rs, :] = kt.astype(BF16)
                yield

    def matmuls(h):
        hs = head_lanes(h)
        for rs in chunks:
            s = masks[0] * jnp.sum(qs_ref[h, rs, :] * kf_ref[h, rs, :], axis=1, keepdims=True)
            b = 1
            for level in range(qt_scr.shape[0]):
                prod = _dg(qt_scr[level, h, rs, :], kt_scr[level, h, rs, :], _NT)
                s = s + (prod * masks[b] if b in masks else prod)
                b *= 2
                yield
            sc_scr[h, rs, :] = s.astype(BF16)
        s_old = s_scr[h]
        for rs in chunks:
            v = v_ref[h, rs, :].astype(BF16)
            o = _dg(qe_scr[h, rs, :], s_old.astype(BF16)) + _dg(sc_scr[h, rs, :], v)
            yield
            tot = cum_scr[h, rs.stop - 1:rs.stop, :]
            s_old = _decay_columns(jnp.exp(tot)) * s_old + _dg(ke_scr[h, rs, :], v, _TN)
            yield
            og_ref[rs, hs] = _head_out(o, gh_ref[:, hs], sog_ref[h, rs, :], og_ref.dtype)
            yield
        s_scr[h] = s_old

    def interleave(*phases):
        phases = list(phases)
        while phases:
            for phase in list(phases):
                if next(phase, phases) is phases:
                    phases.remove(phase)

    interleave(operands(0))
    for h in range(N_HEADS - 1):
        interleave(operands(h + 1), matmuls(h))
    interleave(matmuls(N_HEADS - 1))


def _scan_prompt_kernel(qs_ref, lf_ref, kf_ref, v_ref, sog_ref, gh_ref, og_ref, sfin_ref, zeros_ref, s_scr,
                        *stage_scr, tail_steps):
    c = pl.program_id(1)
    nc = pl.num_programs(1)
    n = SCAN_CHUNK
    zeros_ref[...] = jnp.zeros_like(zeros_ref)

    @pl.when(c == 0)
    def _():
        s_scr[...] = jnp.zeros_like(s_scr)

    def body(precise):
        if not precise:
            _scan_step_staged(qs_ref, lf_ref, kf_ref, v_ref, sog_ref, gh_ref, og_ref, s_scr, *stage_scr)
            return
        tri = (lax.broadcasted_iota(I32, (n, n), 0) >= lax.broadcasted_iota(I32, (n, n), 1)).astype(BF16)
        masks = _score_masks(n, n)

        def head(h, carry):
            hs = pl.ds(pl.multiple_of(h * HEAD_DIM, HEAD_DIM), HEAD_DIM)
            s_old = s_scr[h]
            for j in range(SCAN_STEP_CHUNKS):
                rs = slice(j * n, (j + 1) * n)
                q, g, k, v = qs_ref[h, rs, :], lf_ref[h, rs, :], kf_ref[h, rs, :], v_ref[h, rs, :]
                cum = _dot_exact_lhs(tri, g)
                tot = cum[n - 1:n, :]
                o = _mm(q * jnp.exp(cum), s_old, precise)
                o = o + _mm(_intra_scores(q, k, g, cum, n, n, precise, masks), v, precise)
                ke = k * jnp.exp(tot - cum)
                s_old = _decay_columns(jnp.exp(tot)) * s_old + _mm(ke, v, precise, _TN)
                og_ref[rs, hs] = _head_out(o, gh_ref[:, hs], sog_ref[h, rs, :], og_ref.dtype)
            s_scr[h] = s_old
            return carry

        lax.fori_loop(0, N_HEADS, head, 0)

    _either(None if tail_steps is None else c >= nc - tail_steps, body)

    @pl.when(c == nc - 1)
    def _():
        sfin_ref[0] = s_scr[...]


def _scan_prompt(qs, lf, kf, v, sog, gh, batch, seq, tail_rows):
    t = qs.shape[1]
    d = N_HEADS * HEAD_DIM
    n = SCAN_CHUNK * SCAN_STEP_CHUNKS
    assert seq % n == 0 and (tail_rows is None or tail_rows % n == 0)
    nc = seq // n
    levels = SCAN_CHUNK.bit_length() - 1
    per_head = (N_HEADS, n, HEAD_DIM)
    row = pl.BlockSpec((n, d), lambda b, c: (b * nc + c, 0))
    heads = pl.BlockSpec(per_head, lambda b, c: (0, b * nc + c, 0))
    sorted_rows = (t // MOE_TILE + 1) * MOE_SORTED_ROWS
    zrows = sorted_rows // (batch * nc)
    assert zrows * batch * nc == sorted_rows and zrows % BF16_ROWS == 0
    return pl.pallas_call(
        functools.partial(_scan_prompt_kernel, tail_steps=None if tail_rows is None else tail_rows // n),
        grid=(batch, nc),
        in_specs=[heads] * 5 + [_resident((1, d))],
        out_specs=[row, pl.BlockSpec((1, N_HEADS, HEAD_DIM, HEAD_DIM), lambda b, c: (b, 0, 0, 0)),
                   pl.BlockSpec((zrows, d), lambda b, c: (b * nc + c, 0))],
        out_shape=[jax.ShapeDtypeStruct((batch * seq, d), v.dtype),
                   jax.ShapeDtypeStruct((batch, N_HEADS, HEAD_DIM, HEAD_DIM), F32),
                   jax.ShapeDtypeStruct((sorted_rows, d), BF16)],
        scratch_shapes=[pltpu.VMEM((N_HEADS, HEAD_DIM, HEAD_DIM), F32),
                        pltpu.VMEM(per_head, F32),
                        pltpu.VMEM(per_head, BF16), pltpu.VMEM(per_head, BF16),
                        pltpu.VMEM((levels,) + per_head, BF16), pltpu.VMEM((levels,) + per_head, BF16),
                        pltpu.VMEM((N_HEADS, n, SCAN_CHUNK), BF16)],
        compiler_params=_params(("parallel", "arbitrary")),
        name="scan_prompt",
    )(qs, lf, kf, v, sog, gh)


def _scan_sample_kernel(*refs, steps, precise, layer, chained):
    if chained:
        qs_ref, lf_ref, kf_ref, v_ref, sog_ref, gh_ref, s_ref, _, og_ref, snew_ref = refs
    else:
        qs_ref, lf_ref, kf_ref, v_ref, sog_ref, gh_ref, s_ref, og_ref, all_ref = refs
        for other in range(all_ref.shape[0]):
            if other != layer:
                all_ref[other] = jnp.zeros(all_ref.shape[1:], F32)
        snew_ref = all_ref.at[layer]
    n = SAMPLE_SEQS * steps
    shift = steps.bit_length() - 1
    ri = lax.broadcasted_iota(I32, (n, n), 0)
    ci = lax.broadcasted_iota(I32, (n, n), 1)
    same = (ri >> shift) == (ci >> shift)
    tri = (same & (ri >= ci)).astype(BF16)
    tri_after = (same & (ci > ri)).astype(BF16)
    seq_of_row = lax.broadcasted_iota(I32, (n, LANES), 0) >> shift
    masks = _score_masks(n, steps)

    def head(h, carry):
        hs = pl.ds(pl.multiple_of(h * HEAD_DIM, HEAD_DIM), HEAD_DIM)
        q, g, k, v = qs_ref[h], lf_ref[h], kf_ref[h], v_ref[h]
        cum = _dot_exact_lhs(tri, g)
        after = _dot_exact_lhs(tri_after, g)
        o = _mm(_intra_scores(q, k, g, cum, n, steps, precise, masks), v, precise)
        qe = q * jnp.exp(cum)
        ke = k * jnp.exp(after)
        for s in range(SAMPLE_SEQS):
            mine = seq_of_row == s
            s_old = s_ref[s, h]
            o = o + jnp.where(mine, _mm(qe, s_old, precise), 0.0)
            kv = _mm(jnp.where(mine, ke, 0.0), v, precise, _TN)
            last = (s + 1) * steps - 1
            snew_ref[s, h] = _decay_columns(jnp.exp(cum[last:last + 1, :])) * s_old + kv
        og_ref[:, hs] = _head_out(o, gh_ref[:, hs], sog_ref[h], og_ref.dtype)
        return carry

    lax.fori_loop(0, N_HEADS, head, 0)


def _scan_sample(qs, lf, kf, v, sog, gh, states, layer, new_states, row0, steps, precise):
    d = N_HEADS * HEAD_DIM
    depth, nseq = states.shape[:2]
    assert steps & (steps - 1) == 0 and steps >= 4, "sample block must be a power of two >= 4"
    n = SAMPLE_SEQS * steps
    assert nseq % SAMPLE_SEQS == 0 and row0 % n == 0 and n % BF16_ROWS == 0
    b0 = row0 // n
    chained = new_states is not None
    row = pl.BlockSpec((N_HEADS, n, HEAD_DIM), lambda i: (0, b0 + i, 0))
    tail = (SAMPLE_SEQS, N_HEADS, HEAD_DIM, HEAD_DIM)
    st = pl.BlockSpec((None,) + tail, lambda i: (layer, i, 0, 0, 0))
    st_all = pl.BlockSpec((depth,) + tail, lambda i: (0, i, 0, 0, 0))
    extra = [new_states] if chained else []
    return pl.pallas_call(
        functools.partial(_scan_sample_kernel, steps=steps, precise=precise, layer=layer, chained=chained),
        grid=(nseq // SAMPLE_SEQS,),
        in_specs=[row, row, row, row, row, _resident((1, d)), st] + [pl.BlockSpec(memory_space=pl.ANY)] * len(extra),
        out_specs=[pl.BlockSpec((n, d), lambda i: (i, 0)), st if chained else st_all],
        out_shape=[jax.ShapeDtypeStruct((nseq * steps, d), v.dtype), jax.ShapeDtypeStruct(states.shape, F32)],
        input_output_aliases={7: 1} if chained else {},
        compiler_params=_params(("parallel",)),
        name="scan_sample",
    )(qs, lf, kf, v, sog, gh, states, *extra)


def _mix_kernel(*refs, tm, tiles_per_seq, prompt_tiles, steps, tail):
    if tail is None:
        (xp_ref, xs_ref, ogp_ref, ogs_ref, u_ref, halo_ref, e1_ref, e2_ref, g_ref, wc_ref, gf_ref, wr_ref, br_ref,
         w_cbg_ref, wa_ref, wb_ref, wo_ref, x1_ref, sorted_ref, rinfo_ref, cinfo_ref, prev_scr) = refs
        l_cbg_ref = la_ref = lb_ref = lo_ref = None
    else:
        (xp_ref, xs_ref, ogp_ref, ogs_ref, u_ref, halo_ref, e1_ref, e2_ref, g_ref, wc_ref, gf_ref, wr_ref, br_ref,
         w_cbg_ref, wa_ref, wb_ref, wo_ref, l_cbg_ref, la_ref, lb_ref, lo_ref,
         x1_ref, sorted_ref, rinfo_ref, cinfo_ref, prev_scr) = refs
    d = D_MODEL
    i = pl.program_id(0)

    @pl.when(i == 0)
    def _():
        prev_scr[...] = jnp.zeros_like(prev_scr)

    def full(ref):
        return None if ref is None else ref[...]

    def body(precise):
        x = _pair_load(i, prompt_tiles, xp_ref, xs_ref)
        h = _rmsnorm(x, g_ref[...])
        u = u_ref[...]
        ri = lax.broadcasted_iota(I32, (tm, d), 0)
        r1 = pltpu.roll(u, 1, 0)
        r2 = pltpu.roll(u, 2, 0)
        keep = jnp.where(i % tiles_per_seq == 0, 0.0, 1.0)
        halo = halo_ref[...]
        h_last = halo[7:8, :] * keep
        h_prev = halo[6:7, :] * keep
        is_sample = i >= prompt_tiles
        t_in_seq = ri & (steps - 1)
        prev1 = jnp.where(is_sample,
                          jnp.where(t_in_seq == 0, e1_ref[...], r1),
                          jnp.where(ri == 0, h_last, r1))
        prev2 = jnp.where(is_sample,
                          jnp.where(t_in_seq < 2, e2_ref[...], r2),
                          jnp.where(ri == 0, h_prev, jnp.where(ri == 1, h_last, r2)))
        wc = wc_ref[...]
        conv = prev2 * wc[0:1, :] + prev1 * wc[1:2, :] + u * wc[2:3, :]

        def seg(j):
            sl = slice(j * d, (j + 1) * d)
            return _mmw(h, w_cbg_ref[:, sl], _w_slice(l_cbg_ref, sl), precise)

        router = _route_stages(prev_scr[...], gf_ref[...], wr_ref[...], br_ref[...],
                               sorted_ref, rinfo_ref, cinfo_ref)
        cb = seg(0)
        next(router)
        y_b = _mmw(cb * conv, wb_ref[...], full(lb_ref), precise)
        next(router)
        y_a = _mmw(_pair_load(i, prompt_tiles, ogp_ref, ogs_ref), wa_ref[...], full(la_ref), precise)
        next(router)
        ga = seg(1)
        next(router)
        gb = seg(2)
        next(router)
        m = jax.nn.sigmoid(ga) * y_a + jax.nn.sigmoid(gb) * y_b
        x1 = x + _mmw(m, wo_ref[...], full(lo_ref), precise)
        for _ in router:
            pass
        x1_ref[...] = x1
        prev_scr[...] = x1

    _either(None if tail is None else _tail_tile(i, *tail), body)


def _mix(xp, xs, ogp, ogs, u, e1, e2, g, wc, g_ffn, wr, br, ws, ws_lo, seq, steps, tail):
    t, d = u.shape
    tm = MOE_TILE
    prompt_rows = xp.shape[0]
    assert seq % tm == 0 and prompt_rows % tm == 0 and (t - prompt_rows) % tm == 0 and tm % steps == 0
    pt = prompt_rows // tm
    nt = t // tm
    cur = lambda i: jnp.minimum(i, nt - 1)
    prev = lambda i: jnp.maximum(i - 1, 0)
    prompt = pl.BlockSpec((tm, d), lambda i: (jnp.minimum(i, pt - 1), 0))
    sample = pl.BlockSpec((tm, d), lambda i: (jnp.clip(i - pt, 0, nt - pt - 1), 0))
    row = pl.BlockSpec((tm, d), lambda i: (cur(i), 0))
    halo = pl.BlockSpec((SUBLANES, d), lambda i: (jnp.maximum(cur(i) * (tm // SUBLANES) - 1, 0), 0))
    weights = list(ws) + ([] if tail is None else list(ws_lo))
    return pl.pallas_call(
        functools.partial(_mix_kernel, tm=tm, tiles_per_seq=seq // tm, prompt_tiles=pt, steps=steps, tail=tail),
        grid=(nt + 1,),
        in_specs=[prompt, sample, prompt, sample, row, halo, sample, sample, _resident((1, d)), _resident(wc.shape),
                  _resident((1, d)), _resident(wr.shape), _resident(br.shape)]
        + [_resident(a.shape) for a in weights],
        out_specs=[row,
                   pl.BlockSpec((1, MOE_SORTED_ROWS, d), lambda i: (prev(i), 0, 0)),
                   pl.BlockSpec((tm, LANES), lambda i: (prev(i), 0)),
                   pl.BlockSpec((1, SUBLANES, LANES), lambda i: (prev(i), 0, 0))],
        out_shape=[jax.ShapeDtypeStruct((t, d), F32),
                   jax.ShapeDtypeStruct((nt, MOE_SORTED_ROWS, d), BF16),
                   jax.ShapeDtypeStruct((t, LANES), F32),
                   jax.ShapeDtypeStruct((nt, SUBLANES, LANES), I32)],
        scratch_shapes=[pltpu.VMEM((tm, d), F32)],
        compiler_params=_params(("arbitrary",)),
        name="mix_route",
    )(xp, xs, ogp, ogs, u, u, e1, e2, g, wc, g_ffn, wr, br, *weights)


def _route_stages(x, g, wr, br, xs_ref, rinfo_ref, cinfo_ref):
    tm = MOE_TILE
    rows = MOE_SORTED_ROWS
    hn = _rmsnorm(x, g)
    hn_hi, hn_lo = _split2(hn)
    wr_hi, wr_lo = _split2(wr)
    logits = _dg(hn_hi, wr_hi) + _dg(hn_lo, wr_hi) + _dg(hn_hi, wr_lo) + br
    yield

    lane = lax.broadcasted_iota(I32, (tm, LANES), 1)
    neg = -jnp.inf
    big = jnp.int32(LANES)

    def first_lane(mask):
        return jnp.min(jnp.where(mask, lane, big), axis=1, keepdims=True)

    is_group = (lane >= GROUP_LANE0) & (lane < GROUP_LANE0 + N_GROUPS)
    gl = jnp.where(is_group, logits, neg)
    gmax = jnp.max(gl, axis=1, keepdims=True)
    g_idx = first_lane(gl == gmax) - GROUP_LANE0
    g_w = 1.0 / jnp.sum(jnp.exp(gl - gmax), axis=1, keepdims=True)
    yield

    in_group = (lane < N_EXPERTS) & ((lane >> (EXPERTS_PER_GROUP.bit_length() - 1)) == g_idx)
    el = jnp.where(in_group, logits, neg)
    emax = jnp.max(el, axis=1, keepdims=True)
    ee = jnp.exp(el - emax)
    prob = ee / jnp.sum(ee, axis=1, keepdims=True)
    prob = jnp.where(in_group, prob, -1.0)
    yield
    p1 = jnp.max(prob, axis=1, keepdims=True)
    i1 = first_lane(prob == p1)
    prob2 = jnp.where(lane == i1, -1.0, prob)
    p2 = jnp.max(prob2, axis=1, keepdims=True)
    i2 = first_lane(prob2 == p2)
    psum = p1 + p2
    w1 = p1 / psum * g_w
    w2 = p2 / psum * g_w
    yield

    oh1 = (lane == i1).astype(BF16)
    oh2 = (lane == i2).astype(BF16)
    before = (lax.broadcasted_iota(I32, (tm, tm), 1) < lax.broadcasted_iota(I32, (tm, tm), 0)).astype(BF16)
    c1 = _dg(before, oh1)
    c2 = _dg(before, oh2)
    oh1f = oh1.astype(F32)
    oh2f = oh2.astype(F32)
    cnt1 = jnp.sum(oh1f, axis=0, keepdims=True)
    cnt2 = jnp.sum(oh2f, axis=0, keepdims=True)
    chunks = jnp.floor((cnt1 + cnt2 + (BF16_ROWS - 1)) * (1.0 / BF16_ROWS))
    excl = (lax.broadcasted_iota(I32, (LANES, LANES), 0) < lax.broadcasted_iota(I32, (LANES, LANES), 1)).astype(BF16)
    chunk0 = _dg(jnp.broadcast_to(chunks, (BF16_ROWS, LANES)).astype(BF16), excl)[0:1, :]
    yield
    base = chunk0 * BF16_ROWS
    pos1 = jnp.sum(oh1f * (base + c1), axis=1, keepdims=True)
    pos2 = jnp.sum(oh2f * (base + cnt1 + c2), axis=1, keepdims=True)

    slot = lax.broadcasted_iota(I32, (tm, rows), 1)
    place = ((slot == pos1.astype(I32)) | (slot == pos2.astype(I32))).astype(BF16)
    xs_ref[0] = _dg(place, hn_hi, _TN).astype(BF16)

    rinfo_ref[...] = jnp.where(lane == 0, pos1, jnp.where(lane == 1, pos2, jnp.where(lane == 2, w1, jnp.where(lane == 3, w2, 0.0))))
    r8 = lax.broadcasted_iota(I32, (SUBLANES, LANES), 0)
    cinfo_ref[0] = jnp.where(r8 == 0, chunk0, jnp.where(r8 == 1, chunks, 0.0)).astype(I32)


def _ffn_kernel(order_ref, first_ref, left_ref, blk0_ref, xs_hbm, wg_hbm, wu_hbm, wd_hbm, ys_in_hbm, ys_hbm,
                lhs_buf, res_buf, wg_stage, wu_stage, wd_stage, wgu_bf, wd_bf, in_sem, out_sem, w_sem, *, layer):
    del ys_in_hbm
    nb = MOE_BLOCK_CHUNKS
    rows = BF16_ROWS
    nblocks = blk0_ref[N_EXPERTS]

    spare0 = xs_hbm.shape[0]

    def gather_start(blk, slot):
        first, left = first_ref[blk], left_ref[blk]
        for i in range(nb):
            src = order_ref[first + jnp.where(i < left, i, 0)]
            pltpu.make_async_copy(xs_hbm.at[src], lhs_buf.at[slot, pl.ds(i * rows, rows)], in_sem.at[slot]).start()

    def scatter_start(blk, slot):
        first, left = first_ref[blk], left_ref[blk]
        for i in range(nb):
            dst = jnp.where(i < left, order_ref[first + jnp.where(i < left, i, 0)], spare0 + slot * nb + i)
            pltpu.make_async_copy(res_buf.at[slot, pl.ds(i * rows, rows)], ys_hbm.at[dst], out_sem.at[slot]).start()

    def gather_wait(slot):
        pltpu.make_async_copy(lhs_buf.at[slot], lhs_buf.at[slot], in_sem.at[slot]).wait()

    def scatter_wait(slot):
        pltpu.make_async_copy(res_buf.at[slot], res_buf.at[slot], out_sem.at[slot]).wait()

    def weight_copies(e, slot):
        return [pltpu.make_async_copy(src.at[layer, e], dst.at[slot], w_sem.at[slot])
                for src, dst in ((wg_hbm, wg_stage), (wu_hbm, wu_stage), (wd_hbm, wd_stage))]

    for cp in weight_copies(0, 0):
        cp.start()

    ahead = FFN_GATHER_DEPTH - 1

    @pl.when(nblocks > 0)
    def _():
        for j in range(ahead):
            gather_start(jnp.minimum(j, nblocks - 1), j)

    def expert(e, carry):
        ws = e % 2
        for cp in weight_copies(e, ws):
            cp.wait()
        wgu_bf[:, :D_EXPERT] = wg_stage[ws].astype(BF16)
        wgu_bf[:, D_EXPERT:] = wu_stage[ws].astype(BF16)
        wd_bf[...] = wd_stage[ws].astype(BF16)

        @pl.when(e + 1 < N_EXPERTS)
        def _():
            for cp in weight_copies(e + 1, 1 - ws):
                cp.start()

        def block(blk, c):
            slot = blk % 2

            @pl.when(blk >= 2)
            def _():
                scatter_wait(slot)

            gslot = blk % FFN_GATHER_DEPTH
            gather_wait(gslot)
            gather_start(jnp.minimum(blk + ahead, nblocks - 1), (blk + ahead) % FFN_GATHER_DEPTH)
            gu = _dg(lhs_buf[gslot], wgu_bf[...])
            gate = gu[:, :D_EXPERT]
            hid = gate * jax.nn.sigmoid(gate) * gu[:, D_EXPERT:]
            res_buf[slot] = _dg(hid.astype(BF16), wd_bf[...]).astype(BF16)
            scatter_start(blk, slot)
            return c

        lax.fori_loop(blk0_ref[e], blk0_ref[e + 1], block, 0)
        return carry

    lax.fori_loop(0, N_EXPERTS, expert, 0)

    @pl.when(nblocks > 0)
    def _():
        for j in range(ahead):
            gather_wait((nblocks + j) % FFN_GATHER_DEPTH)

    for back in (2, 1):
        @pl.when(nblocks >= back)
        def _():
            blk = nblocks - back
            scatter_wait(blk % 2)


def _ffn(xs, zeros, order, first, left, blk0, w_gate, w_up, w_down, layer):
    nt, srows, d = xs.shape
    assert zeros.shape == (nt + 1, srows, d) and 2 * MOE_BLOCK_CHUNKS <= MOE_CHUNKS
    chunked = (nt * MOE_CHUNKS, BF16_ROWS, d)
    chunked_out = ((nt + 1) * MOE_CHUNKS, BF16_ROWS, d)
    brows = MOE_BLOCK_CHUNKS * BF16_ROWS
    anywhere = pl.BlockSpec(memory_space=pl.ANY)
    f = w_gate.shape[-1]
    ys = pl.pallas_call(
        functools.partial(_ffn_kernel, layer=layer),
        grid_spec=pltpu.PrefetchScalarGridSpec(
            num_scalar_prefetch=4,
            grid=(1,),
            in_specs=[anywhere] * 5,
            out_specs=anywhere,
            scratch_shapes=[pltpu.VMEM((FFN_GATHER_DEPTH, brows, d), BF16), pltpu.VMEM((2, brows, d), BF16),
                            pltpu.VMEM((2, d, f), F32), pltpu.VMEM((2, d, f), F32), pltpu.VMEM((2, f, d), F32),
                            pltpu.VMEM((d, 2 * f), BF16), pltpu.VMEM((f, d), BF16),
                            pltpu.SemaphoreType.DMA((FFN_GATHER_DEPTH,)), pltpu.SemaphoreType.DMA((2,)),
                            pltpu.SemaphoreType.DMA((2,))],
        ),
        out_shape=jax.ShapeDtypeStruct(chunked_out, BF16),
        input_output_aliases={8: 0},
        compiler_params=_params(("arbitrary",)),
        name="expert_ffn",
    )(order, first, left, blk0, xs.reshape(chunked), w_gate, w_up, w_down, zeros.reshape(chunked_out))
    return ys.reshape(nt + 1, srows, d)


def _block_tables(cinfo):
    nt = cinfo.shape[0]
    nb = MOE_BLOCK_CHUNKS
    nchunks = nt * MOE_CHUNKS
    chunk0 = cinfo[:, 0, :N_EXPERTS]
    chunks = cinfo[:, 1, :N_EXPERTS]
    ends = chunk0 + chunks
    c = jnp.arange(MOE_CHUNKS, dtype=I32)
    label = jnp.sum((c[None, :, None] >= ends[:, None, :]).astype(I32), axis=-1)
    order = jnp.argsort(label.reshape(-1), stable=True).astype(I32)
    cnt = chunks.sum(axis=0).astype(I32)
    cstart = jnp.cumsum(cnt) - cnt
    nblk = (cnt + nb - 1) // nb
    blk0 = jnp.concatenate([jnp.zeros((1,), I32), jnp.cumsum(nblk).astype(I32)])
    max_blocks = nchunks // nb + N_EXPERTS
    j = jnp.arange(max_blocks, dtype=I32)
    ej = jnp.minimum(jnp.sum((j[:, None] >= blk0[None, 1:]).astype(I32), axis=1), N_EXPERTS - 1)
    mine = (ej[:, None] == jnp.arange(N_EXPERTS, dtype=I32)[None, :]).astype(I32)
    local = j - jnp.sum(mine * blk0[None, :N_EXPERTS], axis=1)
    first = jnp.sum(mine * cstart[None, :], axis=1) + local * nb
    left = jnp.sum(mine * cnt[None, :], axis=1) - local * nb
    return order, jnp.clip(first, 0, nchunks - 1).astype(I32), jnp.maximum(left, 0).astype(I32), blk0


def _ple_kernel(x1_ref, ys_ref, rinfo_ref, pp_ref, ps_ref, g_ref, wg_ref, wp_ref, gfin_ref, outp_ref, outs_ref, *,
                final, prompt_tiles):
    tm = MOE_TILE
    rows = MOE_SORTED_ROWS
    i = pl.program_id(0)
    rinfo = rinfo_ref[...]
    pos1 = rinfo[:, 0:1].astype(I32)
    pos2 = rinfo[:, 1:2].astype(I32)
    w1 = rinfo[:, 2:3]
    w2 = rinfo[:, 3:4]
    slot = lax.broadcasted_iota(I32, (tm, rows), 1)
    ys = ys_ref[0]
    y1 = _dg((slot == pos1).astype(BF16), ys)
    y2 = _dg((slot == pos2).astype(BF16), ys)
    x = x1_ref[...] + (w1 * y1 + w2 * y2)
    hn = _rmsnorm(x, g_ref[...]).astype(BF16)
    proj = _dg(_pair_load(i, prompt_tiles, pp_ref, ps_ref).astype(BF16), wp_ref[...])
    x = x + jax.nn.sigmoid(_dg(hn, wg_ref[...])) * proj
    if final:
        x = _rmsnorm(x, gfin_ref[...])

    @pl.when(i < prompt_tiles)
    def _():
        outp_ref[...] = x

    @pl.when(i >= prompt_tiles)
    def _():
        outs_ref[...] = x


def _ple(x1, ys, rinfo, p_prompt, p_sample, layer, g, wg, wp, gfin, final):
    t, d = x1.shape
    tm = MOE_TILE
    nt = t // tm
    tp, ts = p_prompt.shape[1], p_sample.shape[1]
    assert tp % tm == 0 and ts % tm == 0 and tp + ts == t
    pt = tp // tm
    pd = p_prompt.shape[2]
    ys = ys.reshape(-1, MOE_SORTED_ROWS, d)
    return pl.pallas_call(
        functools.partial(_ple_kernel, final=final, prompt_tiles=pt),
        grid=(nt,),
        in_specs=[pl.BlockSpec((tm, d), lambda i: (i, 0)),
                  pl.BlockSpec((1, MOE_SORTED_ROWS, d), lambda i: (i, 0, 0)),
                  pl.BlockSpec((tm, LANES), lambda i: (i, 0)),
                  pl.BlockSpec((None, tm, pd), lambda i: (layer, jnp.minimum(i, pt - 1), 0)),
                  pl.BlockSpec((None, tm, pd), lambda i: (layer, jnp.maximum(i - pt, 0), 0)),
                  _resident((1, d)), _resident(wg.shape), _resident(wp.shape), _resident((1, d))],
        out_specs=_pair_specs(tm, d, pt),
        out_shape=[jax.ShapeDtypeStruct((tp, d), F32), jax.ShapeDtypeStruct((ts, d), F32)],
        compiler_params=_params(("arbitrary",)),
        name="ple",
    )(x1, ys, rinfo, p_prompt, p_sample, g, wg, wp, gfin)


def _lower_bounds(lb_param):
    s = jax.nn.softmax(lb_param.astype(F32), axis=0)
    c = jnp.cumsum(s, axis=0)
    return c - c[0:1]


def _split_kernel(w_ref, hi_ref, lo_ref):
    hi_ref[...], lo_ref[...] = _split2(w_ref[...])


def _hi_lo(w):
    k, n = w.shape
    tk = ROW_TILE
    assert k % tk == 0
    spec = pl.BlockSpec((tk, n), lambda i: (i, 0))
    return pl.pallas_call(
        _split_kernel,
        grid=(k // tk,),
        in_specs=[spec],
        out_specs=[spec, spec],
        out_shape=[jax.ShapeDtypeStruct((k, n), BF16)] * 2,
        compiler_params=_params(("parallel",)),
        name="split_weight",
    )(w)


def kernel(x_prompt, x_sample, state_hgrn, state_conv, p_prompt, p_sample, g_mix, w_in, hg_lower, g_hg_out, w_br_a, w_conv, w_br_b, w_out, g_ffn, w_router_group, b_router_group, w_router_expert, b_router_expert, w_gate, w_up, w_down, g_ple, w_ple_gate, w_ple_proj, g_final):
    depth = w_in.shape[0]
    bp, seq, d = x_prompt.shape
    bs, steps, _ = x_sample.shape
    assert d == D_MODEL and w_conv.shape[1] == CONV_W
    assert TAIL_ROWS % ROW_TILE == 0 and seq >= TAIL_ROWS
    tp = bp * seq
    ts = bs * steps
    hw = N_HEADS * HEAD_DIM

    xp, xs_rows = x_prompt.reshape(tp, d), x_sample.reshape(ts, d)
    pp = p_prompt.reshape(depth, tp, -1)
    ps = p_sample.reshape(depth, ts, -1)
    lbs = _lower_bounds(hg_lower)
    row = lambda a: a.reshape(1, -1)

    hg_p, cv_p, cv_s = [], [], []
    hg_s = None
    for li in range(depth):
        careful = li < depth - 1
        tm = ROW_TILE if careful else FAST_ROW_TILE
        tail = (seq // tm, TAIL_ROWS // tm, tp // tm) if careful else None
        act = F32 if careful else BF16
        wi = w_in[li]
        w_scan = jnp.concatenate([wi[:, :4 * hw], wi[:, 4 * hw + d:4 * hw + 3 * d]], axis=1)
        w_cbg = jnp.concatenate([wi[:, 4 * hw:4 * hw + d], wi[:, 4 * hw + 3 * d:]], axis=1)
        mix_w = (w_cbg, w_br_a[li], w_br_b[li], w_out[li])
        if careful:
            w_scan, w_scan_lo = _hi_lo(w_scan)
            mix_w, mix_w_lo = zip(*[_hi_lo(w) for w in mix_w])
        else:
            w_scan, w_scan_lo = w_scan.astype(BF16), None
            mix_w, mix_w_lo = [w.astype(BF16) for w in mix_w], None

        qs, lf, kf, v, sog, u = _inproj(xp, xs_rows, row(g_mix[li]), row(lbs[li]), w_scan, w_scan_lo, tail, act, tm)

        gh = row(g_hg_out[li])
        ogp, s_p, zeros = _scan_prompt(qs, lf, kf, v, sog, gh, bp, seq, TAIL_ROWS if careful else None)
        ogs, hg_s = _scan_sample(qs, lf, kf, v, sog, gh, state_hgrn, li, hg_s, tp, steps, careful)
        hg_p.append(s_p)
        keep = CONV_W - 1
        cv_p.append(jnp.stack([lax.slice_in_dim(u, (b + 1) * seq - keep, (b + 1) * seq) for b in range(bp)]))
        cv_s.append(jnp.concatenate([state_conv[li], lax.slice_in_dim(u, tp, tp + ts).reshape(bs, steps, d)],
                                    axis=1)[:, steps:])

        buf = state_conv[li]
        zero = jnp.zeros((bs, steps - 1, d), F32)
        e1 = jnp.concatenate([buf[:, 1:2], zero], axis=1).reshape(ts, d)
        e2 = jnp.concatenate([buf, zero[:, 1:]], axis=1).reshape(ts, d)
        wr = jnp.zeros((d, LANES), F32).at[:, :N_EXPERTS].set(w_router_expert[li])
        wr = wr.at[:, GROUP_LANE0:GROUP_LANE0 + N_GROUPS].set(w_router_group[li])
        br = jnp.zeros((1, LANES), F32).at[0, :N_EXPERTS].set(b_router_expert[li])
        br = br.at[0, GROUP_LANE0:GROUP_LANE0 + N_GROUPS].set(b_router_group[li])
        mix_tail = (seq // MOE_TILE, TAIL_ROWS // MOE_TILE, tp // MOE_TILE) if careful else None
        x1, xsort, rinfo, cinfo = _mix(xp, xs_rows, ogp, ogs, u, e1, e2, row(g_mix[li]), w_conv[li],
                                       row(g_ffn[li]), wr, br, mix_w, mix_w_lo, seq, steps, mix_tail)
        ys = _ffn(xsort, zeros.reshape(-1, MOE_SORTED_ROWS, d), *_block_tables(cinfo), w_gate, w_up, w_down, li)

        xp, xs_rows = _ple(x1, ys, rinfo, pp, ps, li, row(g_ple[li]), w_ple_gate[li].astype(BF16),
                           w_ple_proj[li].astype(BF16), row(g_final), li == depth - 1)

    return (xp.reshape(bp, seq, d), xs_rows.reshape(bs, steps, d), jnp.stack(hg_p), jnp.stack(cv_p), hg_s,
            jnp.stack(cv_s))
```
